```python
import math
import jax
import jax.numpy as jnp
from jax import lax
import numpy as np

D_MODEL = 1024
BATCH = 8
SEQ = 2048
DEPTH = 4
DEC_BATCH = 128
DEC_SEQ = 4
PAST_LEN = 2048
PAGE_SIZE = 128

N_MIXERS = 3
LAYER_KIND = tuple(i % N_MIXERS for i in range(DEPTH))
LAYER_SLOT = tuple(LAYER_KIND[:i].count(LAYER_KIND[i]) for i in range(DEPTH))
N_NSA = LAYER_KIND.count(0)
N_S5 = LAYER_KIND.count(1)
N_HG = LAYER_KIND.count(2)

D_FF = 2816
PLE_DIM = 256
RMS_EPS = 1e-6

NSA_HEADS = 16
NSA_KV_HEADS = 4
HEAD_DIM = 64
NSA_HPG = NSA_HEADS // NSA_KV_HEADS
NSA_KV_W = NSA_KV_HEADS * HEAD_DIM
NSA_Q_W = NSA_HEADS * HEAD_DIM
NSA_IN = NSA_Q_W + 6 * NSA_KV_W + 3 * NSA_HEADS
CMP_BLOCK = 32
CMP_STRIDE = 16
SLC_BLOCK = 64
SLC_TOPN = 16
WINDOW = 512
Q_BLOCK = 128
FORCE_BONUS = 1e4
NEG_INF = -1e30
ROPE_THETA = 500000.0
ROPE_DIMS = HEAD_DIM // 4

S5_GROUP_CH = 16
S5_GROUPS = D_MODEL // S5_GROUP_CH
S5_STATE = 64

HG_DK = 128
HG_HEADS = D_MODEL // HG_DK
HG_DV = D_MODEL // HG_HEADS
HG_CHUNK = 64

kernel_name = 'nsa_s5_hgrn2_macaron_hybrid_step'


def rms_norm(x, g):
    x32 = x.astype(jnp.float32)
    y = x32 * lax.rsqrt(jnp.mean(x32 * x32, axis=-1, keepdims=True) + RMS_EPS)
    return (y * g.astype(jnp.float32)).astype(x.dtype)


def swiglu(x, w_in, w_out):
    a, b = jnp.split(x @ w_in, 2, axis=-1)
    return (jax.nn.silu(a) * b) @ w_out


def partial_rope(x, pos):
    half = ROPE_DIMS // 2
    inv = ROPE_THETA ** (-jnp.arange(half, dtype=jnp.float32) / half)
    ang = pos.astype(jnp.float32)[:, None] * inv[None, :]
    cos = jnp.cos(ang)[None, :, None, :]
    sin = jnp.sin(ang)[None, :, None, :]
    x32 = x.astype(jnp.float32)
    x1 = x32[..., :half]
    x2 = x32[..., half:ROPE_DIMS]
    out = jnp.concatenate([x1 * cos - x2 * sin, x2 * cos + x1 * sin, x32[..., ROPE_DIMS:]], axis=-1)
    return out.astype(x.dtype)


def nsa_compressed(q, kc_raw, vc_raw, pe, w, k_gain, qpos):
    N, L = kc_raw.shape[:2]
    T = q.shape[1]
    r_ov = CMP_BLOCK // CMP_STRIDE
    n_chunk = -(-L // CMP_STRIDE)
    nc = n_chunk - r_ov + 1
    pad = n_chunk * CMP_STRIDE - L

    def compress(r, pe_c, w_c):
        r = jnp.pad(r, ((0, 0), (0, pad), (0, 0), (0, 0)))
        ch = r.reshape(N, n_chunk, CMP_STRIDE, NSA_KV_HEADS, HEAD_DIM)
        blk = jnp.concatenate([ch[:, j:j + nc] for j in range(r_ov)], axis=2)
        return jnp.einsum('nclgd,lde->ncge', blk + pe_c[None, None, :, None, :], w_c)

    k_cmp = rms_norm(compress(kc_raw, pe[0], w[0]), k_gain)
    v_cmp = compress(vc_raw, pe[1], w[1])
    qg = q.reshape(N, T, NSA_KV_HEADS, NSA_HPG, HEAD_DIM)
    s = jnp.einsum('ntghd,ncgd->ntghc', qg, k_cmp).astype(jnp.float32) * HEAD_DIM ** -0.5
    blk_end = jnp.arange(nc) * CMP_STRIDE + CMP_BLOCK - 1
    mask = (blk_end[None, :] <= qpos[:, None])[None, :, None, None, :]
    p = jax.nn.softmax(jnp.where(mask, s, NEG_INF), axis=-1) * mask
    o = jnp.einsum('ntghc,ncgd->ntghd', p.astype(v_cmp.dtype), v_cmp)
    return o.reshape(N, T, NSA_HEADS, HEAD_DIM), p


def nsa_select(p_cmp, qpos, L):
    nc = p_cmp.shape[-1]
    n_sel = -(-L // SLC_BLOCK)
    c_start = jnp.arange(nc) * CMP_STRIDE
    s_start = jnp.arange(n_sel) * SLC_BLOCK
    ov = (jnp.minimum(c_start[:, None] + CMP_BLOCK, s_start[None, :] + SLC_BLOCK)
          - jnp.maximum(c_start[:, None], s_start[None, :]))
    ov = jnp.clip(ov, 0, None).astype(jnp.float32) / CMP_BLOCK
    imp = jnp.einsum('ntghc,cs->ntgs', p_cmp, ov)
    qblk = (qpos // SLC_BLOCK)[:, None]
    sidx = jnp.arange(n_sel)[None, :]
    valid = (sidx <= qblk)[None, :, None, :]
    forced = ((sidx == 0) | (sidx == qblk) | (sidx == qblk - 1))[None, :, None, :]
    score = jnp.where(valid, imp + jnp.where(forced, FORCE_BONUS, 0.0), NEG_INF)
    _, idx = lax.top_k(score, min(SLC_TOPN, n_sel))
    return idx


def nsa_sparse(q_r, sel_idx, ks, vs, kw, vw, q_start, kw_start):
    N, T = q_r.shape[:2]
    L = ks.shape[1]
    n_sel = -(-L // SLC_BLOCK)
    pad = n_sel * SLC_BLOCK - L

    def to_blocks(r):
        r = jnp.pad(r, ((0, 0), (0, pad), (0, 0), (0, 0)))
        return r.reshape(N, n_sel, SLC_BLOCK, NSA_KV_HEADS, HEAD_DIM).transpose(0, 3, 1, 2, 4)

    ks_b = to_blocks(ks)
    vs_b = to_blocks(vs)
    front = ((0, 0), (WINDOW, 0), (0, 0), (0, 0))
    kw_p = jnp.pad(kw, front)
    vw_p = jnp.pad(vw, front)
    qb = min(Q_BLOCK, T)
    nb = T // qb
    span = WINDOW + qb - 1
    scale = HEAD_DIM ** -0.5
    g_idx = jnp.arange(NSA_KV_HEADS)[None, :, None]

    def one_block(item):
        n = item // nb
        t0 = (item % nb) * qb
        qpos = q_start + t0 + jnp.arange(qb)
        q = lax.dynamic_slice_in_dim(q_r[n], t0, qb, 0).reshape(qb, NSA_KV_HEADS, NSA_HPG, HEAD_DIM)
        idx = lax.dynamic_slice_in_dim(sel_idx[n], t0, qb, 0)
        kb = ks_b[n][g_idx, idx]
        vb = vs_b[n][g_idx, idx]
        kpos = idx[..., None] * SLC_BLOCK + jnp.arange(SLC_BLOCK)
        s = jnp.einsum('qghd,qgnkd->qghnk', q, kb).astype(jnp.float32) * scale
        m = (kpos <= qpos[:, None, None, None])[:, :, None]
        p = jax.nn.softmax(jnp.where(m, s, NEG_INF), axis=(-2, -1))
        o_s = jnp.einsum('qghnk,qgnkd->qghd', p.astype(vb.dtype), vb)
        start = q_start - kw_start + t0 + 1
        kwin = lax.dynamic_slice_in_dim(kw_p[n], start, span, 0)
        vwin = lax.dynamic_slice_in_dim(vw_p[n], start, span, 0)
        kwpos = q_start + t0 - WINDOW + 1 + jnp.arange(span)
        mw = ((kwpos[None, :] <= qpos[:, None]) & (kwpos[None, :] > qpos[:, None] - WINDOW)
              & (kwpos[None, :] >= kw_start))
        sw = jnp.einsum('qghd,kgd->qghk', q, kwin).astype(jnp.float32) * scale
        pw = jax.nn.softmax(jnp.where(mw[:, None, None, :], sw, NEG_INF), axis=-1)
        o_w = jnp.einsum('qghk,kgd->qghd', pw.astype(vwin.dtype), vwin)
        return (o_s.reshape(qb, NSA_HEADS, HEAD_DIM), o_w.reshape(qb, NSA_HEADS, HEAD_DIM))

    o_s, o_w = lax.map(one_block, jnp.arange(N * nb))
    return (o_s.reshape(N, T, NSA_HEADS, HEAD_DIM), o_w.reshape(N, T, NSA_HEADS, HEAD_DIM))


def nsa_mix(xn, pos0, past_rows, past_win, w_in, w_out, qk_gain, cmp_pe, cmp_w):
    N, T, _ = xn.shape
    q, kv, gl = jnp.split(xn @ w_in, [NSA_Q_W, NSA_Q_W + 6 * NSA_KV_W], axis=-1)
    q = rms_norm(q.reshape(N, T, NSA_HEADS, HEAD_DIM), qk_gain[0])
    kv = kv.reshape(N, T, 6, NSA_KV_HEADS, HEAD_DIM)
    gates = jax.nn.sigmoid(gl.astype(jnp.float32)).reshape(N, T, 3, NSA_HEADS, 1)
    pos = pos0 + jnp.arange(T)
    k_slc = partial_rope(rms_norm(kv[:, :, 2], qk_gain[2]), pos)
    k_win = partial_rope(rms_norm(kv[:, :, 4], qk_gain[3]), pos)
    new_rows = jnp.stack([kv[:, :, 0], kv[:, :, 1], k_slc, kv[:, :, 3]], axis=2)
    new_win = jnp.stack([k_win, kv[:, :, 5]], axis=2)
    rows = new_rows if past_rows is None else jnp.concatenate([past_rows, new_rows], axis=1)
    win = new_win if past_win is None else jnp.concatenate([past_win, new_win], axis=1)
    L = rows.shape[1]
    Lw = win.shape[1]
    o_cmp, p_cmp = nsa_compressed(q, rows[:, :, 0], rows[:, :, 1], cmp_pe, cmp_w, qk_gain[1], pos)
    sel_idx = nsa_select(p_cmp, pos, L)
    o_slc, o_win = nsa_sparse(partial_rope(q, pos), sel_idx, rows[:, :, 2], rows[:, :, 3],
                              win[:, :, 0], win[:, :, 1], L - T, L - Lw)
    o = gates[:, :, 0] * o_cmp + gates[:, :, 1] * o_slc + gates[:, :, 2] * o_win
    y = o.astype(xn.dtype).reshape(N, T, D_MODEL) @ w_out
    buf = min(WINDOW, T) if past_win is None else past_win.shape[1]
    return y.astype(xn.dtype), new_rows, win[:, Lw - buf:]


def _cmul_scan_op(e1, e2):
    a1r, a1i, b1r, b1i = e1
    a2r, a2i, b2r, b2i = e2
    return (a1r * a2r - a1i * a2i,
            a1r * a2i + a1i * a2r,
            a2r * b1r - a2i * b1i + b2r,
            a2r * b1i + a2i * b1r + b2i)


def s5_mix(u, h0, lam_re, lam_im, log_dt, b_re, b_im, c_re, c_im, d_skip, w_glu):
    N, T, _ = u.shape
    u32 = u.astype(jnp.float32)
    ug = u32.reshape(N, T, S5_GROUPS, S5_GROUP_CH)
    dt = jnp.exp(log_dt.astype(jnp.float32))[:, None]
    lr = jnp.minimum(lam_re.astype(jnp.float32), -1e-4)
    li = lam_im.astype(jnp.float32)
    mag = jnp.exp(lr * dt)
    a_re = mag * jnp.cos(li * dt)
    a_im = mag * jnp.sin(li * dt)
    den = lr * lr + li * li
    z_re = ((a_re - 1.0) * lr + a_im * li) / den
    z_im = (a_im * lr - (a_re - 1.0) * li) / den
    bb_re = z_re[..., None] * b_re - z_im[..., None] * b_im
    bb_im = z_re[..., None] * b_im + z_im[..., None] * b_re
    bu_re = jnp.einsum('ntgc,gpc->tngp', ug, bb_re)
    bu_im = jnp.einsum('ntgc,gpc->tngp', ug, bb_im)
    if h0 is not None:
        h0r = h0[:, 0].astype(jnp.float32)
        h0i = h0[:, 1].astype(jnp.float32)
        bu_re = bu_re.at[0].add(a_re * h0r - a_im * h0i)
        bu_im = bu_im.at[0].add(a_re * h0i + a_im * h0r)
    ar = jnp.broadcast_to(a_re, (T, 1) + a_re.shape)
    ai = jnp.broadcast_to(a_im, (T, 1) + a_im.shape)
    _, _, h_re, h_im = lax.associative_scan(_cmul_scan_op, (ar, ai, bu_re, bu_im), axis=0)
    y = (jnp.einsum('tngp,gcp->ntgc', h_re, c_re) - jnp.einsum('tngp,gcp->ntgc', h_im, c_im))
    y = y.reshape(N, T, D_MODEL) + d_skip * u32
    z = jax.nn.gelu(y)
    a, b = jnp.split(z @ w_glu, 2, axis=-1)
    out = a * jax.nn.sigmoid(b)
    h_last = jnp.stack([h_re[-1], h_im[-1]], axis=1)
    return out.astype(u.dtype), h_last


def gated_chunk_scan(q, k, v, logf, s0):
    N, T, H, dk = q.shape
    dv = v.shape[-1]
    C = min(HG_CHUNK, T)
    nc = -(-T // C)
    pad = nc * C - T

    def chunks(a):
        a = jnp.pad(a, ((0, 0), (0, pad), (0, 0), (0, 0)))
        return a.reshape(N, nc, C, H, a.shape[-1]).transpose(1, 0, 3, 2, 4)

    causal = jnp.tril(jnp.ones((C, C), dtype=bool))

    def step(S, xs):
        qc, kc, vc, lc = xs
        G = jnp.cumsum(lc, axis=2)
        diff = jnp.where(causal[:, :, None], G[:, :, :, None, :] - G[:, :, None, :, :], NEG_INF)
        A = jnp.einsum('nhtsd,nhsd->nhts', qc[:, :, :, None, :] * jnp.exp(diff), kc)
        o = (jnp.einsum('nhcd,nhde->nhce', qc * jnp.exp(G), S)
             + jnp.einsum('nhts,nhse->nhte', A, vc))
        G_last = G[:, :, -1]
        S = (jnp.exp(G_last)[..., None] * S
             + jnp.einsum('nhsd,nhse->nhde', kc * jnp.exp(G_last[:, :, None] - G), vc))
        return S, o

    S, o = lax.scan(step, s0, (chunks(q), chunks(k), chunks(v), chunks(logf)))
    o = o.transpose(1, 0, 3, 2, 4).reshape(N, nc * C, H, dv)[:, :T]
    return o, S


def hgrn_mix(xn, s0, w_in, w_out, o_gain, lb):
    N, T, _ = xn.shape
    q, fz, v, g = jnp.split(xn @ w_in, 4, axis=-1)
    f = lb + (1.0 - lb) * jax.nn.sigmoid(fz.astype(jnp.float32))

    def heads(a, d):
        return a.reshape(N, T, HG_HEADS, d)

    o, s_last = gated_chunk_scan(heads(q.astype(jnp.float32), HG_DK), heads(1.0 - f, HG_DK),
                                 heads(v.astype(jnp.float32), HG_DV), heads(jnp.log(f), HG_DK),
                                 s0.astype(jnp.float32))
    o = rms_norm(o, o_gain) * heads(jax.nn.silu(g.astype(jnp.float32)), HG_DV)
    y = o.reshape(N, T, D_MODEL) @ w_out
    return y.astype(xn.dtype), s_last


def setup_inputs(seed: int = 0) -> dict:
    key = jax.random.key(seed)
    k = jax.random.split(key, 40)
    f32 = jnp.float32

    def nrm(kk, shape, scale=1.0):
        return jax.random.normal(kk, shape, f32) * scale

    n_pages = PAST_LEN // PAGE_SIZE
    n_used = DEC_BATCH * n_pages
    n_phys = n_used + -(-n_used // 4)
    w_buf = min(WINDOW, PAST_LEN)
    page_table = jax.random.permutation(k[6], n_phys)[:n_used].reshape(DEC_BATCH, n_pages).astype(jnp.int32)
    lam_im0 = jnp.pi * jnp.arange(S5_STATE, dtype=f32)
    return {
        'x_prompt': nrm(k[0], (BATCH, SEQ, D_MODEL)),
        'x_sample': nrm(k[1], (DEC_BATCH, DEC_SEQ, D_MODEL)),
        'cache_nsa': nrm(k[2], (N_NSA, n_phys, PAGE_SIZE, 4, NSA_KV_HEADS, HEAD_DIM)),
        'state_nsa_win': nrm(k[3], (N_NSA, DEC_BATCH, w_buf, 2, NSA_KV_HEADS, HEAD_DIM)),
        'state_s5': nrm(k[4], (N_S5, DEC_BATCH, 2, S5_GROUPS, S5_STATE), 0.5),
        'state_hgrn': nrm(k[5], (N_HG, DEC_BATCH, HG_HEADS, HG_DK, HG_DV), 0.3),
        'page_table': page_table,
        'p_prompt': nrm(k[7], (DEPTH, BATCH, SEQ, PLE_DIM)),
        'p_sample': nrm(k[8], (DEPTH, DEC_BATCH, DEC_SEQ, PLE_DIM)),
        'norm_gain': 1.0 + nrm(k[9], (DEPTH, 4, D_MODEL), 0.02),
        'ffn_w_in': nrm(k[10], (DEPTH, 2, D_MODEL, 2 * D_FF), D_MODEL ** -0.5),
        'ffn_w_out': nrm(k[11], (DEPTH, 2, D_FF, D_MODEL), D_FF ** -0.5),
        'ple_w_gate': nrm(k[12], (DEPTH, D_MODEL, D_MODEL), D_MODEL ** -0.5),
        'ple_w_proj': nrm(k[13], (DEPTH, PLE_DIM, D_MODEL), 0.5 * PLE_DIM ** -0.5),
        'nsa_w_in': nrm(k[14], (N_NSA, D_MODEL, NSA_IN), D_MODEL ** -0.5),
        'nsa_w_out': nrm(k[15], (N_NSA, NSA_Q_W, D_MODEL), NSA_Q_W ** -0.5),
        'nsa_qk_gain': 1.0 + nrm(k[16], (N_NSA, 4, HEAD_DIM), 0.02),
        'nsa_cmp_pe': nrm(k[17], (N_NSA, 2, CMP_BLOCK, HEAD_DIM), 0.1),
        'nsa_cmp_w': nrm(k[18], (N_NSA, 2, CMP_BLOCK, HEAD_DIM, HEAD_DIM), (CMP_BLOCK * HEAD_DIM) ** -0.5),
        's5_lam_re': -0.5 + nrm(k[19], (N_S5, S5_GROUPS, S5_STATE), 0.01),
        's5_lam_im': lam_im0 + nrm(k[20], (N_S5, S5_GROUPS, S5_STATE), 0.01),
        's5_log_dt': jax.random.uniform(k[21], (N_S5, S5_GROUPS), f32, math.log(1e-3), math.log(1e-1)),
        's5_b_re': nrm(k[22], (N_S5, S5_GROUPS, S5_STATE, S5_GROUP_CH), (2 * S5_GROUP_CH) ** -0.5),
        's5_b_im': nrm(k[23], (N_S5, S5_GROUPS, S5_STATE, S5_GROUP_CH), (2 * S5_GROUP_CH) ** -0.5),
        's5_c_re': nrm(k[24], (N_S5, S5_GROUPS, S5_GROUP_CH, S5_STATE), (2 * S5_STATE) ** -0.5),
        's5_c_im': nrm(k[25], (N_S5, S5_GROUPS, S5_GROUP_CH, S5_STATE), (2 * S5_STATE) ** -0.5),
        's5_d': nrm(k[26], (N_S5, D_MODEL), 0.5),
        's5_w_glu': nrm(k[27], (N_S5, D_MODEL, 2 * D_MODEL), D_MODEL ** -0.5),
        'hg_w_in': nrm(k[28], (N_HG, D_MODEL, 4 * D_MODEL), D_MODEL ** -0.5),
        'hg_w_out': nrm(k[29], (N_HG, D_MODEL, D_MODEL), D_MODEL ** -0.5),
        'hg_o_gain': 1.0 + nrm(k[30], (N_HG, HG_DV), 0.02),
        'hg_lb_raw': nrm(k[31], (DEPTH, D_MODEL), 0.1),
    }


def reference(x_prompt, x_sample, cache_nsa, state_nsa_win, state_s5, state_hgrn, page_table,
              p_prompt, p_sample, norm_gain, ffn_w_in, ffn_w_out, ple_w_gate, ple_w_proj,
              nsa_w_in, nsa_w_out, nsa_qk_gain, nsa_cmp_pe, nsa_cmp_w,
              s5_lam_re, s5_lam_im, s5_log_dt, s5_b_re, s5_b_im, s5_c_re, s5_c_im, s5_d, s5_w_glu,
              hg_w_in, hg_w_out, hg_o_gain, hg_lb_raw):
    past_len = page_table.shape[1] * PAGE_SIZE
    lb_sm = jax.nn.softmax(hg_lb_raw.astype(jnp.float32), axis=0)
    lower_bounds = jnp.cumsum(lb_sm, axis=0) - lb_sm[0]

    def trunk(x, p, pos0, paged, win_state, s5_state, hg_state):
        N = x.shape[0]
        rows_o, win_o, s5_o, hg_o = [], [], [], []
        for i in range(DEPTH):
            kind = LAYER_KIND[i]
            j = LAYER_SLOT[i]
            g = norm_gain[i]
            x = x + 0.5 * swiglu(rms_norm(x, g[0]), ffn_w_in[i, 0], ffn_w_out[i, 0])
            xn = rms_norm(x, g[1])
            if kind == 0:
                past_rows = None
                if paged is not None:
                    pg = paged[j][page_table]
                    past_rows = pg.reshape((pg.shape[0], pg.shape[1] * pg.shape[2]) + pg.shape[3:])
                past_win = None if win_state is None else win_state[j]
                y, r, w = nsa_mix(xn, pos0, past_rows, past_win, nsa_w_in[j], nsa_w_out[j],
                                  nsa_qk_gain[j], nsa_cmp_pe[j], nsa_cmp_w[j])
                rows_o.append(r)
                win_o.append(w)
            elif kind == 1:
                h0 = None if s5_state is None else s5_state[j]
                y, s = s5_mix(xn, h0, s5_lam_re[j], s5_lam_im[j], s5_log_dt[j], s5_b_re[j], s5_b_im[j],
                              s5_c_re[j], s5_c_im[j], s5_d[j], s5_w_glu[j])
                s5_o.append(s)
            else:
                if hg_state is None:
                    s0 = jnp.zeros((N, HG_HEADS, HG_DK, HG_DV), jnp.float32)
                else:
                    s0 = hg_state[j]
                y, s = hgrn_mix(xn, s0, hg_w_in[j], hg_w_out[j], hg_o_gain[j], lower_bounds[i])
                hg_o.append(s)
            x = x + y
            x = x + 0.5 * swiglu(rms_norm(x, g[2]), ffn_w_in[i, 1], ffn_w_out[i, 1])
            gate = jax.nn.sigmoid(rms_norm(x, g[3]) @ ple_w_gate[i])
            x = x + gate * (p[i] @ ple_w_proj[i])
        return x, jnp.stack(rows_o), jnp.stack(win_o), jnp.stack(s5_o), jnp.stack(hg_o)

    y_p, rows_p, win_p, s5_p, hg_p = trunk(x_prompt, p_prompt, 0, None, None, None, None)
    y_s, rows_s, win_s, s5_s, hg_s = trunk(x_sample, p_sample, past_len, cache_nsa, state_nsa_win,
                                           state_s5, state_hgrn)
    return (y_p, y_s, rows_p, win_p, s5_p, hg_p, rows_s, win_s, s5_s, hg_s)
```

```python
import functools
import math

import jax
import jax.numpy as jnp
from jax import lax
from jax.experimental import pallas as pl
from jax.experimental.pallas import tpu as pltpu

F32 = jnp.float32
MXU_DTYPE = jnp.bfloat16

D_MODEL = 1024
DEPTH = 4
PAGE_SIZE = 128
D_FF = 2816
PLE_DIM = 256
RMS_EPS = 1e-6
LAYER_KIND = (0, 1, 2, 0)
LAYER_SLOT = (0, 0, 0, 1)

NSA_HEADS = 16
NSA_KV_HEADS = 4
HEAD_DIM = 64
NSA_HPG = NSA_HEADS // NSA_KV_HEADS
NSA_KV_W = NSA_KV_HEADS * HEAD_DIM
NSA_Q_W = NSA_HEADS * HEAD_DIM
NSA_IN = NSA_Q_W + 6 * NSA_KV_W + 3 * NSA_HEADS
CMP_BLOCK = 32
CMP_STRIDE = 16
SLC_BLOCK = 64
SLC_TOPN = 16
WINDOW = 512
Q_BLOCK = 128
FORCE_BONUS = 1e4
NEG_INF = -1e30
ROPE_THETA = 500000.0
ROPE_DIMS = HEAD_DIM // 4

S5_GROUP_CH = 16
S5_GROUPS = D_MODEL // S5_GROUP_CH
S5_STATE = 64

HG_DK = 128
HG_HEADS = D_MODEL // HG_DK
HG_DV = D_MODEL // HG_HEADS
HG_CHUNK = 64

V7X_VMEM_BYTES = 64 * 1024 * 1024
VMEM_LIMIT = V7X_VMEM_BYTES - 8 * 1024 * 1024
ROW_TILE = 512
FF_CHUNK = 256


def _resident(shape):
    return pl.BlockSpec(shape, lambda *_: (0,) * len(shape), pipeline_mode=pl.Buffered(1))


def _row_tile(rows):
    return max(t for t in range(8, ROW_TILE + 1, 8) if rows % t == 0)


def _rows(tm, width):
    return pl.BlockSpec((tm, width), lambda i: (i, 0))


def _params():
    return pltpu.CompilerParams(dimension_semantics=("parallel",), vmem_limit_bytes=VMEM_LIMIT)


def _rms(x, g):
    return x * lax.rsqrt(jnp.mean(x * x, axis=-1, keepdims=True) + RMS_EPS) * g


def _mm(a, b):
    return jnp.dot(a.astype(MXU_DTYPE), b.astype(MXU_DTYPE), preferred_element_type=F32)


def _ffn_body(x_ref, g_ref, gn_ref, win_ref, wout_ref, o_ref, on_ref):
    x = x_ref[...]
    xb = _rms(x, g_ref[...]).astype(MXU_DTYPE)
    acc = jnp.zeros(x.shape, F32)
    for c in range(D_FF // FF_CHUNK):
        lo = c * FF_CHUNK
        a = _mm(xb, win_ref[:, lo:lo + FF_CHUNK])
        b = _mm(xb, win_ref[:, D_FF + lo:D_FF + lo + FF_CHUNK])
        h = a * jax.nn.sigmoid(a) * b
        acc = acc + _mm(h, wout_ref[lo:lo + FF_CHUNK, :])
    y = x + 0.5 * acc
    o_ref[...] = y
    on_ref[...] = _rms(y, gn_ref[...])


def ffn_step(x, g, g_next, w_in, w_out):
    rows = x.shape[0]
    tm = _row_tile(rows)
    out = jax.ShapeDtypeStruct((rows, D_MODEL), F32)
    return pl.pallas_call(
        _ffn_body,
        grid=(rows // tm,),
        in_specs=[_rows(tm, D_MODEL), _resident((1, D_MODEL)), _resident((1, D_MODEL)),
                  _resident((D_MODEL, 2 * D_FF)), _resident((D_FF, D_MODEL))],
        out_specs=[_rows(tm, D_MODEL), _rows(tm, D_MODEL)],
        out_shape=[out, out],
        compiler_params=_params(),
    )(x, g.reshape(1, -1), g_next.reshape(1, -1), w_in.astype(MXU_DTYPE), w_out.astype(MXU_DTYPE))


def _ple_body(x_ref, xn_ref, p_ref, wg_ref, wp_ref, o_ref):
    gate = jax.nn.sigmoid(_mm(xn_ref[...], wg_ref[...]))
    o_ref[...] = x_ref[...] + gate * _mm(p_ref[...], wp_ref[...])


def ple_step(x, xn, p, w_gate, w_proj):
    rows = x.shape[0]
    tm = _row_tile(rows)
    return pl.pallas_call(
        _ple_body,
        grid=(rows // tm,),
        in_specs=[_rows(tm, D_MODEL), _rows(tm, D_MODEL), _rows(tm, PLE_DIM),
                  _resident((D_MODEL, D_MODEL)), _resident((PLE_DIM, D_MODEL))],
        out_specs=_rows(tm, D_MODEL),
        out_shape=jax.ShapeDtypeStruct((rows, D_MODEL), F32),
        compiler_params=_params(),
    )(x, xn, p, w_gate.astype(MXU_DTYPE), w_proj.astype(MXU_DTYPE))


def _proj_body(a_ref, w_ref, o_ref):
    o_ref[...] = _mm(a_ref[...], w_ref[...])


def proj(a, w):
    rows, k = a.shape
    n = w.shape[1]
    tm = _row_tile(rows)
    return pl.pallas_call(
        _proj_body,
        grid=(rows // tm,),
        in_specs=[_rows(tm, k), _resident((k, n))],
        out_specs=_rows(tm, n),
        out_shape=jax.ShapeDtypeStruct((rows, n), F32),
        compiler_params=_params(),
    )(a, w.astype(MXU_DTYPE))


def _resid_body(x_ref, a_ref, w_ref, o_ref):
    o_ref[...] = x_ref[...] + _mm(a_ref[...], w_ref[...])


def resid_proj(x, a, w):
    rows, k = a.shape
    tm = _row_tile(rows)
    return pl.pallas_call(
        _resid_body,
        grid=(rows // tm,),
        in_specs=[_rows(tm, D_MODEL), _rows(tm, k), _resident((k, D_MODEL))],
        out_specs=_rows(tm, D_MODEL),
        out_shape=jax.ShapeDtypeStruct((rows, D_MODEL), F32),
        compiler_params=_params(),
    )(x, a, w.astype(MXU_DTYPE))


def rms_norm(x, g):
    x32 = x.astype(jnp.float32)
    y = x32 * lax.rsqrt(jnp.mean(x32 * x32, axis=-1, keepdims=True) + RMS_EPS)
    return (y * g.astype(jnp.float32)).astype(x.dtype)


def partial_rope(x, pos):
    half = ROPE_DIMS // 2
    inv = ROPE_THETA ** (-jnp.arange(half, dtype=jnp.float32) / half)
    ang = pos.astype(jnp.float32)[:, None] * inv[None, :]
    cos = jnp.cos(ang)[None, :, None, :]
    sin = jnp.sin(ang)[None, :, None, :]
    x32 = x.astype(jnp.float32)
    x1 = x32[..., :half]
    x2 = x32[..., half:ROPE_DIMS]
    out = jnp.concatenate([x1 * cos - x2 * sin, x2 * cos + x1 * sin, x32[..., ROPE_DIMS:]], axis=-1)
    return out.astype(x.dtype)


def nsa_compressed(q, kc_raw, vc_raw, pe, w, k_gain, qpos):
    N, L = kc_raw.shape[:2]
    T = q.shape[1]
    r_ov = CMP_BLOCK // CMP_STRIDE
    n_chunk = -(-L // CMP_STRIDE)
    nc = n_chunk - r_ov + 1
    pad = n_chunk * CMP_STRIDE - L

    def compress(r, pe_c, w_c):
        r = jnp.pad(r, ((0, 0), (0, pad), (0, 0), (0, 0)))
        ch = r.reshape(N, n_chunk, CMP_STRIDE, NSA_KV_HEADS, HEAD_DIM)
        blk = jnp.concatenate([ch[:, j:j + nc] for j in range(r_ov)], axis=2)
        return jnp.einsum('nclgd,lde->ncge', blk + pe_c[None, None, :, None, :], w_c)

    k_cmp = rms_norm(compress(kc_raw, pe[0], w[0]), k_gain)
    v_cmp = compress(vc_raw, pe[1], w[1])
    qg = q.reshape(N, T, NSA_KV_HEADS, NSA_HPG, HEAD_DIM)
    s = jnp.einsum('ntghd,ncgd->ntghc', qg, k_cmp).astype(jnp.float32) * HEAD_DIM ** -0.5
    blk_end = jnp.arange(nc) * CMP_STRIDE + CMP_BLOCK - 1
    mask = (blk_end[None, :] <= qpos[:, None])[None, :, None, None, :]
    p = jax.nn.softmax(jnp.where(mask, s, NEG_INF), axis=-1) * mask
    o = jnp.einsum('ntghc,ncgd->ntghd', p.astype(v_cmp.dtype), v_cmp)
    return o.reshape(N, T, NSA_HEADS, HEAD_DIM), p


def nsa_select(p_cmp, qpos, L):
    nc = p_cmp.shape[-1]
    n_sel = -(-L // SLC_BLOCK)
    c_start = jnp.arange(nc) * CMP_STRIDE
    s_start = jnp.arange(n_sel) * SLC_BLOCK
    ov = (jnp.minimum(c_start[:, None] + CMP_BLOCK, s_start[None, :] + SLC_BLOCK)
          - jnp.maximum(c_start[:, None], s_start[None, :]))
    ov = jnp.clip(ov, 0, None).astype(jnp.float32) / CMP_BLOCK
    imp = jnp.einsum('ntghc,cs->ntgs', p_cmp, ov)
    qblk = (qpos // SLC_BLOCK)[:, None]
    sidx = jnp.arange(n_sel)[None, :]
    valid = (sidx <= qblk)[None, :, None, :]
    forced = ((sidx == 0) | (sidx == qblk) | (sidx == qblk - 1))[None, :, None, :]
    score = jnp.where(valid, imp + jnp.where(forced, FORCE_BONUS, 0.0), NEG_INF)
    _, idx = lax.top_k(score, min(SLC_TOPN, n_sel))
    return idx


def nsa_sparse(q_r, sel_idx, ks, vs, kw, vw, q_start, kw_start):
    N, T = q_r.shape[:2]
    L = ks.shape[1]
    n_sel = -(-L // SLC_BLOCK)
    pad = n_sel * SLC_BLOCK - L

    def to_blocks(r):
        r = jnp.pad(r, ((0, 0), (0, pad), (0, 0), (0, 0)))
        return r.reshape(N, n_sel, SLC_BLOCK, NSA_KV_HEADS, HEAD_DIM).transpose(0, 3, 1, 2, 4)

    ks_b = to_blocks(ks)
    vs_b = to_blocks(vs)
    front = ((0, 0), (WINDOW, 0), (0, 0), (0, 0))
    kw_p = jnp.pad(kw, front)
    vw_p = jnp.pad(vw, front)
    qb = min(Q_BLOCK, T)
    nb = T // qb
    span = WINDOW + qb - 1
    scale = HEAD_DIM ** -0.5
    g_idx = jnp.arange(NSA_KV_HEADS)[None, :, None]

    def one_block(item):
        n = item // nb
        t0 = (item % nb) * qb
        qpos = q_start + t0 + jnp.arange(qb)
        q = lax.dynamic_slice_in_dim(q_r[n], t0, qb, 0).reshape(qb, NSA_KV_HEADS, NSA_HPG, HEAD_DIM)
        idx = lax.dynamic_slice_in_dim(sel_idx[n], t0, qb, 0)
        kb = ks_b[n][g_idx, idx]
        vb = vs_b[n][g_idx, idx]
        kpos = idx[..., None] * SLC_BLOCK + jnp.arange(SLC_BLOCK)
        s = jnp.einsum('qghd,qgnkd->qghnk', q, kb).astype(jnp.float32) * scale
        m = (kpos <= qpos[:, None, None, None])[:, :, None]
        p = jax.nn.softmax(jnp.where(m, s, NEG_INF), axis=(-2, -1))
        o_s = jnp.einsum('qghnk,qgnkd->qghd', p.astype(vb.dtype), vb)
        start = q_start - kw_start + t0 + 1
        kwin = lax.dynamic_slice_in_dim(kw_p[n], start, span, 0)
        vwin = lax.dynamic_slice_in_dim(vw_p[n], start, span, 0)
        kwpos = q_start + t0 - WINDOW + 1 + jnp.arange(span)
        mw = ((kwpos[None, :] <= qpos[:, None]) & (kwpos[None, :] > qpos[:, None] - WINDOW)
              & (kwpos[None, :] >= kw_start))
        sw = jnp.einsum('qghd,kgd->qghk', q, kwin).astype(jnp.float32) * scale
        pw = jax.nn.softmax(jnp.where(mw[:, None, None, :], sw, NEG_INF), axis=-1)
        o_w = jnp.einsum('qghk,kgd->qghd', pw.astype(vwin.dtype), vwin)
        return (o_s.reshape(qb, NSA_HEADS, HEAD_DIM), o_w.reshape(qb, NSA_HEADS, HEAD_DIM))

    o_s, o_w = lax.map(one_block, jnp.arange(N * nb))
    return (o_s.reshape(N, T, NSA_HEADS, HEAD_DIM), o_w.reshape(N, T, NSA_HEADS, HEAD_DIM))


def nsa_core(pr, pos0, past_rows, past_win, qk_gain, cmp_pe, cmp_w):
    N, T, _ = pr.shape
    q, kv, gl = jnp.split(pr, [NSA_Q_W, NSA_Q_W + 6 * NSA_KV_W], axis=-1)
    q = rms_norm(q.reshape(N, T, NSA_HEADS, HEAD_DIM), qk_gain[0])
    kv = kv.reshape(N, T, 6, NSA_KV_HEADS, HEAD_DIM)
    gates = jax.nn.sigmoid(gl.astype(jnp.float32)).reshape(N, T, 3, NSA_HEADS, 1)
    pos = pos0 + jnp.arange(T)
    k_slc = partial_rope(rms_norm(kv[:, :, 2], qk_gain[2]), pos)
    k_win = partial_rope(rms_norm(kv[:, :, 4], qk_gain[3]), pos)
    new_rows = jnp.stack([kv[:, :, 0], kv[:, :, 1], k_slc, kv[:, :, 3]], axis=2)
    new_win = jnp.stack([k_win, kv[:, :, 5]], axis=2)
    rows = new_rows if past_rows is None else jnp.concatenate([past_rows, new_rows], axis=1)
    win = new_win if past_win is None else jnp.concatenate([past_win, new_win], axis=1)
    L = rows.shape[1]
    Lw = win.shape[1]
    o_cmp, p_cmp = nsa_compressed(q, rows[:, :, 0], rows[:, :, 1], cmp_pe, cmp_w, qk_gain[1], pos)
    sel_idx = nsa_select(p_cmp, pos, L)
    o_slc, o_win = nsa_sparse(partial_rope(q, pos), sel_idx, rows[:, :, 2], rows[:, :, 3],
                              win[:, :, 0], win[:, :, 1], L - T, L - Lw)
    o = gates[:, :, 0] * o_cmp + gates[:, :, 1] * o_slc + gates[:, :, 2] * o_win
    buf = min(WINDOW, T) if past_win is None else past_win.shape[1]
    return o.reshape(N, T, D_MODEL), new_rows, win[:, Lw - buf:]


def _cmul_scan_op(e1, e2):
    a1r, a1i, b1r, b1i = e1
    a2r, a2i, b2r, b2i = e2
    return (a1r * a2r - a1i * a2i,
            a1r * a2i + a1i * a2r,
            a2r * b1r - a2i * b1i + b2r,
            a2r * b1i + a2i * b1r + b2i)


def s5_core(u, h0, lam_re, lam_im, log_dt, b_re, b_im, c_re, c_im, d_skip):
    N, T, _ = u.shape
    u32 = u.astype(jnp.float32)
    ug = u32.reshape(N, T, S5_GROUPS, S5_GROUP_CH)
    dt = jnp.exp(log_dt.astype(jnp.float32))[:, None]
    lr = jnp.minimum(lam_re.astype(jnp.float32), -1e-4)
    li = lam_im.astype(jnp.float32)
    mag = jnp.exp(lr * dt)
    a_re = mag * jnp.cos(li * dt)
    a_im = mag * jnp.sin(li * dt)
    den = lr * lr + li * li
    z_re = ((a_re - 1.0) * lr + a_im * li) / den
    z_im = (a_im * lr - (a_re - 1.0) * li) / den
    bb_re = z_re[..., None] * b_re - z_im[..., None] * b_im
    bb_im = z_re[..., None] * b_im + z_im[..., None] * b_re
    bu_re = jnp.einsum('ntgc,gpc->tngp', ug, bb_re)
    bu_im = jnp.einsum('ntgc,gpc->tngp', ug, bb_im)
    if h0 is not None:
        h0r = h0[:, 0].astype(jnp.float32)
        h0i = h0[:, 1].astype(jnp.float32)
        bu_re = bu_re.at[0].add(a_re * h0r - a_im * h0i)
        bu_im = bu_im.at[0].add(a_re * h0i + a_im * h0r)
    ar = jnp.broadcast_to(a_re, (T, 1) + a_re.shape)
    ai = jnp.broadcast_to(a_im, (T, 1) + a_im.shape)
    _, _, h_re, h_im = lax.associative_scan(_cmul_scan_op, (ar, ai, bu_re, bu_im), axis=0)
    y = (jnp.einsum('tngp,gcp->ntgc', h_re, c_re) - jnp.einsum('tngp,gcp->ntgc', h_im, c_im))
    y = y.reshape(N, T, D_MODEL) + d_skip * u32
    z = jax.nn.gelu(y)
    h_last = jnp.stack([h_re[-1], h_im[-1]], axis=1)
    return z, h_last


def gated_chunk_scan(q, k, v, logf, s0):
    N, T, H, dk = q.shape
    dv = v.shape[-1]
    C = min(HG_CHUNK, T)
    nc = -(-T // C)
    pad = nc * C - T

    def chunks(a):
        a = jnp.pad(a, ((0, 0), (0, pad), (0, 0), (0, 0)))
        return a.reshape(N, nc, C, H, a.shape[-1]).transpose(1, 0, 3, 2, 4)

    causal = jnp.tril(jnp.ones((C, C), dtype=bool))

    def step(S, xs):
        qc, kc, vc, lc = xs
        G = jnp.cumsum(lc, axis=2)
        diff = jnp.where(causal[:, :, None], G[:, :, :, None, :] - G[:, :, None, :, :], NEG_INF)
        A = jnp.einsum('nhtsd,nhsd->nhts', qc[:, :, :, None, :] * jnp.exp(diff), kc)
        o = (jnp.einsum('nhcd,nhde->nhce', qc * jnp.exp(G), S)
             + jnp.einsum('nhts,nhse->nhte', A, vc))
        G_last = G[:, :, -1]
        S = (jnp.exp(G_last)[..., None] * S
             + jnp.einsum('nhsd,nhse->nhde', kc * jnp.exp(G_last[:, :, None] - G), vc))
        return S, o

    S, o = lax.scan(step, s0, (chunks(q), chunks(k), chunks(v), chunks(logf)))
    o = o.transpose(1, 0, 3, 2, 4).reshape(N, nc * C, H, dv)[:, :T]
    return o, S


def hgrn_core(pr, s0, o_gain, lb):
    N, T, _ = pr.shape
    q, fz, v, g = jnp.split(pr, 4, axis=-1)
    f = lb + (1.0 - lb) * jax.nn.sigmoid(fz.astype(jnp.float32))

    def heads(a, d):
        return a.reshape(N, T, HG_HEADS, d)

    o, s_last = gated_chunk_scan(heads(q.astype(jnp.float32), HG_DK), heads(1.0 - f, HG_DK),
                                 heads(v.astype(jnp.float32), HG_DV), heads(jnp.log(f), HG_DK),
                                 s0.astype(jnp.float32))
    o = rms_norm(o, o_gain) * heads(jax.nn.silu(g.astype(jnp.float32)), HG_DV)
    return o.reshape(N, T, D_MODEL), s_last


def kernel(x_prompt, x_sample, cache_nsa, state_nsa_win, state_s5, state_hgrn, page_table, p_prompt, p_sample, norm_gain, ffn_w_in, ffn_w_out, ple_w_gate, ple_w_proj, nsa_w_in, nsa_w_out, nsa_qk_gain, nsa_cmp_pe, nsa_cmp_w, s5_lam_re, s5_lam_im, s5_log_dt, s5_b_re, s5_b_im, s5_c_re, s5_c_im, s5_d, s5_w_glu, hg_w_in, hg_w_out, hg_o_gain, hg_lb_raw):
    B, T, _ = x_prompt.shape
    Bs, Ts, _ = x_sample.shape
    rp = B * T
    rs = Bs * Ts
    past_len = page_table.shape[1] * PAGE_SIZE
    lb_sm = jax.nn.softmax(hg_lb_raw.astype(F32), axis=0)
    lower_bounds = jnp.cumsum(lb_sm, axis=0) - lb_sm[0]

    x = jnp.concatenate([x_prompt.reshape(rp, D_MODEL), x_sample.reshape(rs, D_MODEL)], axis=0)
    p_all = jnp.concatenate([p_prompt.reshape(DEPTH, rp, PLE_DIM), p_sample.reshape(DEPTH, rs, PLE_DIM)], axis=1)

    def split(a):
        return a[:rp].reshape(B, T, -1), a[rp:].reshape(Bs, Ts, -1)

    def join(a, b):
        return jnp.concatenate([a.reshape(rp, -1), b.reshape(rs, -1)], axis=0)

    outs_p = {0: [], 1: [], 2: [], 3: []}
    outs_s = {0: [], 1: [], 2: [], 3: []}
    for i in range(DEPTH):
        kind = LAYER_KIND[i]
        j = LAYER_SLOT[i]
        g = norm_gain[i]
        x, xn = ffn_step(x, g[0], g[1], ffn_w_in[i, 0], ffn_w_out[i, 0])
        if kind == 0:
            w_in = jnp.pad(nsa_w_in[j], ((0, 0), (0, (-NSA_IN) % 128)))
            pr_p, pr_s = split(proj(xn, w_in)[:, :NSA_IN])
            pg = cache_nsa[j][page_table]
            past_rows = pg.reshape((pg.shape[0], pg.shape[1] * pg.shape[2]) + pg.shape[3:])
            o_p, r_p, w_p = nsa_core(pr_p, 0, None, None, nsa_qk_gain[j], nsa_cmp_pe[j], nsa_cmp_w[j])
            o_s, r_s, w_s = nsa_core(pr_s, past_len, past_rows, state_nsa_win[j], nsa_qk_gain[j],
                                     nsa_cmp_pe[j], nsa_cmp_w[j])
            outs_p[0].append(r_p); outs_p[1].append(w_p)
            outs_s[0].append(r_s); outs_s[1].append(w_s)
            x = resid_proj(x, join(o_p, o_s), nsa_w_out[j])
        elif kind == 1:
            u_p, u_s = split(xn)
            args = (s5_lam_re[j], s5_lam_im[j], s5_log_dt[j], s5_b_re[j], s5_b_im[j], s5_c_re[j], s5_c_im[j], s5_d[j])
            z_p, h_p = s5_core(u_p, None, *args)
            z_s, h_s = s5_core(u_s, state_s5[j], *args)
            outs_p[2].append(h_p); outs_s[2].append(h_s)
            ab = proj(join(z_p, z_s), s5_w_glu[j])
            x = x + ab[:, :D_MODEL] * jax.nn.sigmoid(ab[:, D_MODEL:])
        else:
            pr_p, pr_s = split(proj(xn, hg_w_in[j]))
            o_p, s_p = hgrn_core(pr_p, jnp.zeros((B, HG_HEADS, HG_DK, HG_DV), F32), hg_o_gain[j], lower_bounds[i])
            o_s, s_s = hgrn_core(pr_s, state_hgrn[j], hg_o_gain[j], lower_bounds[i])
            outs_p[3].append(s_p); outs_s[3].append(s_s)
            x = resid_proj(x, join(o_p, o_s), hg_w_out[j])
        x, xn = ffn_step(x, g[2], g[3], ffn_w_in[i, 1], ffn_w_out[i, 1])
        x = ple_step(x, xn, p_all[i], ple_w_gate[i], ple_w_proj[i])

    y_p, y_s = split(x)
    return (y_p, y_s,
            jnp.stack(outs_p[0]), jnp.stack(outs_p[1]), jnp.stack(outs_p[2]), jnp.stack(outs_p[3]),
            jnp.stack(outs_s[0]), jnp.stack(outs_s[1]), jnp.stack(outs_s[2]), jnp.stack(outs_s[3]))
```

```python
import functools
import math

import numpy as np
import jax
import jax.numpy as jnp
from jax import lax
from jax.experimental import pallas as pl
from jax.experimental.pallas import tpu as pltpu

F32 = jnp.float32
MXU_DTYPE = jnp.bfloat16

D_MODEL = 1024
DEPTH = 4
PAGE_SIZE = 128
D_FF = 2816
PLE_DIM = 256
RMS_EPS = 1e-6
LAYER_KIND = (0, 1, 2, 0)
LAYER_SLOT = (0, 0, 0, 1)

NSA_HEADS = 16
NSA_KV_HEADS = 4
HEAD_DIM = 64
NSA_HPG = NSA_HEADS // NSA_KV_HEADS
NSA_KV_W = NSA_KV_HEADS * HEAD_DIM
NSA_Q_W = NSA_HEADS * HEAD_DIM
NSA_IN = NSA_Q_W + 6 * NSA_KV_W + 3 * NSA_HEADS
CMP_BLOCK = 32
CMP_STRIDE = 16
SLC_BLOCK = 64
SLC_TOPN = 16
WINDOW = 512
FORCE_BONUS = 1e4
NEG_INF = -1e30
ROPE_THETA = 500000.0
ROPE_DIMS = HEAD_DIM // 4
ATT_SCALE = HEAD_DIM ** -0.5

S5_GROUP_CH = 16
S5_GROUPS = D_MODEL // S5_GROUP_CH
S5_STATE = 64

HG_DK = 128
HG_HEADS = D_MODEL // HG_DK
HG_DV = D_MODEL // HG_HEADS
HG_CHUNK = 64

V7X_VMEM_BYTES = 64 * 1024 * 1024
VMEM_LIMIT = V7X_VMEM_BYTES - 8 * 1024 * 1024
LANES = 128
ROW_TILE = 512
FF_CHUNK = 256
Q_TILE = 128
KV_TILE = 512
TS_PAD = 8
SOFTMAX_M0 = -1e29


def _resident(shape):
    return pl.BlockSpec(shape, lambda *_: (0,) * len(shape), pipeline_mode=pl.Buffered(1))


def _row_tile(rows):
    return max(t for t in range(8, ROW_TILE + 1, 8) if rows % t == 0)


def _rows(tm, width):
    return pl.BlockSpec((tm, width), lambda i: (i, 0))


def _params(n_axes=1):
    return pltpu.CompilerParams(dimension_semantics=("parallel",) * n_axes, vmem_limit_bytes=VMEM_LIMIT)


def _rms(x, g):
    return x * lax.rsqrt(jnp.mean(x * x, axis=-1, keepdims=True) + RMS_EPS) * g


def _mm(a, b):
    return jnp.dot(a.astype(MXU_DTYPE), b.astype(MXU_DTYPE), preferred_element_type=F32)


def _mm_nt(a, b):
    return lax.dot_general(a.astype(MXU_DTYPE), b.astype(MXU_DTYPE), (((1,), (1,)), ((), ())),
                           preferred_element_type=F32)


def _mm_split(a, b):
    hi = a.astype(MXU_DTYPE)
    lo = (a - hi.astype(F32)).astype(MXU_DTYPE)
    return (jnp.dot(hi, b, preferred_element_type=F32) + jnp.dot(lo, b, preferred_element_type=F32))


def _ffn_body(x_ref, g_ref, gn_ref, win_ref, wout_ref, o_ref, on_ref):
    x = x_ref[...]
    xb = _rms(x, g_ref[...]).astype(MXU_DTYPE)
    acc = jnp.zeros(x.shape, F32)
    for c in range(D_FF // FF_CHUNK):
        lo = c * FF_CHUNK
        a = _mm(xb, win_ref[:, lo:lo + FF_CHUNK])
        b = _mm(xb, win_ref[:, D_FF + lo:D_FF + lo + FF_CHUNK])
        h = a * jax.nn.sigmoid(a) * b
        acc = acc + _mm(h, wout_ref[lo:lo + FF_CHUNK, :])
    y = x + 0.5 * acc
    o_ref[...] = y
    on_ref[...] = _rms(y, gn_ref[...])


def ffn_step(x, g, g_next, w_in, w_out):
    rows = x.shape[0]
    tm = _row_tile(rows)
    out = jax.ShapeDtypeStruct((rows, D_MODEL), F32)
    return pl.pallas_call(
        _ffn_body,
        grid=(rows // tm,),
        in_specs=[_rows(tm, D_MODEL), _resident((1, D_MODEL)), _resident((1, D_MODEL)),
                  _resident((D_MODEL, 2 * D_FF)), _resident((D_FF, D_MODEL))],
        out_specs=[_rows(tm, D_MODEL), _rows(tm, D_MODEL)],
        out_shape=[out, out],
        compiler_params=_params(),
        name="ffn_step",
    )(x, g.reshape(1, -1), g_next.reshape(1, -1), w_in.astype(MXU_DTYPE), w_out.astype(MXU_DTYPE))


def _ple_body(x_ref, xn_ref, p_ref, wg_ref, wp_ref, o_ref):
    gate = jax.nn.sigmoid(_mm(xn_ref[...], wg_ref[...]))
    o_ref[...] = x_ref[...] + gate * _mm(p_ref[...], wp_ref[...])


def ple_step(x, xn, p, w_gate, w_proj):
    rows = x.shape[0]
    tm = _row_tile(rows)
    return pl.pallas_call(
        _ple_body,
        grid=(rows // tm,),
        in_specs=[_rows(tm, D_MODEL), _rows(tm, D_MODEL), _rows(tm, PLE_DIM),
                  _resident((D_MODEL, D_MODEL)), _resident((PLE_DIM, D_MODEL))],
        out_specs=_rows(tm, D_MODEL),
        out_shape=jax.ShapeDtypeStruct((rows, D_MODEL), F32),
        compiler_params=_params(),
        name="ple_step",
    )(x, xn, p, w_gate.astype(MXU_DTYPE), w_proj.astype(MXU_DTYPE))


def _proj_body(a_ref, w_ref, o_ref):
    o_ref[...] = _mm(a_ref[...], w_ref[...])


def proj(a, w):
    rows, k = a.shape
    n = w.shape[1]
    tm = _row_tile(rows)
    return pl.pallas_call(
        _proj_body,
        grid=(rows // tm,),
        in_specs=[_rows(tm, k), _resident((k, n))],
        out_specs=_rows(tm, n),
        out_shape=jax.ShapeDtypeStruct((rows, n), F32),
        compiler_params=_params(),
        name="proj",
    )(a, w.astype(MXU_DTYPE))


def _resid_body(x_ref, a_ref, w_ref, o_ref):
    o_ref[...] = x_ref[...] + _mm(a_ref[...], w_ref[...])


def resid_proj(x, a, w):
    rows, k = a.shape
    tm = _row_tile(rows)
    return pl.pallas_call(
        _resid_body,
        grid=(rows // tm,),
        in_specs=[_rows(tm, D_MODEL), _rows(tm, k), _resident((k, D_MODEL))],
        out_specs=_rows(tm, D_MODEL),
        out_shape=jax.ShapeDtypeStruct((rows, D_MODEL), F32),
        compiler_params=_params(),
        name="resid_proj",
    )(x, a, w.astype(MXU_DTYPE))


def _head_perm():
    idx = np.arange(NSA_Q_W).reshape(NSA_KV_HEADS, NSA_HPG, HEAD_DIM)
    return idx.transpose(1, 0, 2).reshape(-1)


def _gate_expand():
    x = np.zeros((LANES, 3 * NSA_Q_W), np.float32)
    for b in range(3):
        for g in range(NSA_KV_HEADS):
            for j in range(NSA_HPG):
                h = g * NSA_HPG + j
                c0 = b * NSA_Q_W + (j * NSA_KV_HEADS + g) * HEAD_DIM
                x[b * NSA_HEADS + h, c0:c0 + HEAD_DIM] = 1.0
    return x


def _overlap_matrix(nb, length):
    n_sel = -(-length // SLC_BLOCK)
    c0 = np.arange(nb)[:, None] * CMP_STRIDE
    s0 = np.arange(LANES)[None, :] * SLC_BLOCK
    ov = np.clip(np.minimum(c0 + CMP_BLOCK, s0 + SLC_BLOCK) - np.maximum(c0, s0), 0, None) / CMP_BLOCK
    ov = np.where(np.arange(LANES)[None, :] < n_sel, ov, 0.0)
    return ov.astype(np.float32)


def _rope_tables(pos):
    half = ROPE_DIMS // 2
    inv = ROPE_THETA ** (-jnp.arange(half, dtype=F32) / half)
    ang = pos.astype(F32)[:, None] * inv[None, :]
    cos, sin = jnp.cos(ang), jnp.sin(ang)
    ones = jnp.ones((pos.shape[0], HEAD_DIM - ROPE_DIMS), F32)
    zeros = jnp.zeros((pos.shape[0], HEAD_DIM - ROPE_DIMS), F32)
    zh = jnp.zeros_like(sin)
    c = jnp.concatenate([cos, cos, ones], axis=1)
    sa = jnp.concatenate([-sin, zh, zeros], axis=1)
    sb = jnp.concatenate([zh, sin, zeros], axis=1)
    rep = LANES // HEAD_DIM
    return jnp.tile(c, (1, rep)), jnp.tile(sa, (1, rep)), jnp.tile(sb, (1, rep))


def _nsa_proj_body(xn_ref, wq_ref, wkv_ref, wgl_ref, b64_ref, gq_ref, gk_ref, c_ref, sa_ref, sb_ref,
                   qn_ref, qr_ref, rows_ref, win_ref, gates_ref):
    xb = xn_ref[...].astype(MXU_DTYPE)
    c, sa, sb = c_ref[...], sa_ref[...], sb_ref[...]

    def head_norm(v, gain):
        w = v.shape[1]
        ms = _mm_split(v * v, b64_ref[:w, :w])
        return v * lax.rsqrt(ms + RMS_EPS) * gain

    def rope(v):
        w = v.shape[1]
        rep = w // LANES
        ct, sat, sbt = (jnp.concatenate([t] * rep, axis=1) for t in (c, sa, sb))
        return v * ct + pltpu.roll(v, w - ROPE_DIMS // 2, 1) * sat + pltpu.roll(v, ROPE_DIMS // 2, 1) * sbt

    qn = head_norm(_mm(xb, wq_ref[...]), gq_ref[...])
    qn_ref[...] = qn.astype(qn_ref.dtype)
    qr_ref[...] = rope(qn).astype(qr_ref.dtype)
    kv = _mm(xb, wkv_ref[...])
    w = NSA_KV_W
    rows_ref[:, 0:2 * w] = kv[:, 0:2 * w]
    rows_ref[:, 2 * w:3 * w] = rope(head_norm(kv[:, 2 * w:3 * w], gk_ref[0:1, :]))
    rows_ref[:, 3 * w:4 * w] = kv[:, 3 * w:4 * w]
    win_ref[:, 0:w] = rope(head_norm(kv[:, 4 * w:5 * w], gk_ref[1:2, :]))
    win_ref[:, w:2 * w] = kv[:, 5 * w:6 * w]
    gates_ref[...] = jax.nn.sigmoid(_mm(xb, wgl_ref[...]))


def nsa_proj(xn, w_in, qk_gain, pos):
    rows = xn.shape[0]
    tm = _row_tile(rows)
    kvw = 6 * NSA_KV_W
    wq = w_in[:, :NSA_Q_W][:, _head_perm()].astype(MXU_DTYPE)
    wkv = w_in[:, NSA_Q_W:NSA_Q_W + kvw].astype(MXU_DTYPE)
    wgl = jnp.pad(w_in[:, NSA_Q_W + kvw:], ((0, 0), (0, LANES - 3 * NSA_HEADS))).astype(MXU_DTYPE)
    b64 = jnp.asarray(np.kron(np.eye(NSA_HEADS), np.full((HEAD_DIM, HEAD_DIM), 1.0 / HEAD_DIM)), MXU_DTYPE)
    gq = jnp.tile(qk_gain[0], NSA_HEADS).reshape(1, -1)
    gk = jnp.stack([jnp.tile(qk_gain[2], NSA_KV_HEADS), jnp.tile(qk_gain[3], NSA_KV_HEADS)])
    c, sa, sb = _rope_tables(pos)
    return pl.pallas_call(
        _nsa_proj_body,
        grid=(rows // tm,),
        in_specs=[_rows(tm, D_MODEL), _resident((D_MODEL, NSA_Q_W)), _resident((D_MODEL, kvw)),
                  _resident((D_MODEL, LANES)), _resident((NSA_Q_W, NSA_Q_W)), _resident((1, NSA_Q_W)),
                  _resident((2, NSA_KV_W)), _rows(tm, LANES), _rows(tm, LANES), _rows(tm, LANES)],
        out_specs=[_rows(tm, NSA_Q_W), _rows(tm, NSA_Q_W), _rows(tm, 4 * NSA_KV_W), _rows(tm, 2 * NSA_KV_W),
                   _rows(tm, LANES)],
        out_shape=[jax.ShapeDtypeStruct((rows, NSA_Q_W), MXU_DTYPE), jax.ShapeDtypeStruct((rows, NSA_Q_W), MXU_DTYPE),
                   jax.ShapeDtypeStruct((rows, 4 * NSA_KV_W), F32), jax.ShapeDtypeStruct((rows, 2 * NSA_KV_W), F32),
                   jax.ShapeDtypeStruct((rows, LANES), F32)],
        compiler_params=_params(),
        name="nsa_proj",
    )(xn, wq, wkv, wgl, b64, gq, gk, c, sa, sb)


def _nsa_compress_body(pt_ref, *refs, n_pages):
    del pt_ref
    page_refs = refs[:n_pages]
    wk_ref, wv_ref, pek_ref, pev_ref, b64_ref, gk_ref, kc_ref, vc_ref = refs[n_pages:]
    per_page = PAGE_SIZE // CMP_STRIDE
    nb = n_pages * per_page

    def compress(kind, w_ref, pe_ref):
        lo = kind * NSA_KV_W
        first = jnp.zeros((nb, NSA_KV_W), F32)
        second = jnp.zeros((nb, NSA_KV_W), F32)
        for l in range(CMP_STRIDE):
            c0 = l * 4 * NSA_KV_W + lo
            x = jnp.concatenate([r[0, :, c0:c0 + NSA_KV_W] for r in page_refs], axis=0)
            first = first + _mm(x + pe_ref[l:l + 1, :], w_ref[l])
            second = second + _mm(x + pe_ref[CMP_STRIDE + l:CMP_STRIDE + l + 1, :], w_ref[CMP_STRIDE + l])
        return first + pltpu.roll(second, nb - 1, 0)

    kc = compress(0, wk_ref, pek_ref)
    ms = _mm_split(kc * kc, b64_ref[...])
    kc_ref[0] = kc * lax.rsqrt(ms + RMS_EPS) * gk_ref[...]
    vc_ref[0] = compress(1, wv_ref, pev_ref)


def nsa_compress(pages, table, cmp_pe, cmp_w, k_gain):
    n_seq, n_pages = table.shape
    nb = n_pages * (PAGE_SIZE // CMP_STRIDE)
    eye = jnp.eye(NSA_KV_HEADS, dtype=F32)
    w4 = jnp.einsum('gh,klde->klgdhe', eye, cmp_w).reshape(2, CMP_BLOCK, NSA_KV_W, NSA_KV_W).astype(MXU_DTYPE)
    pe4 = jnp.tile(cmp_pe, (1, 1, NSA_KV_HEADS))
    b64 = jnp.asarray(np.kron(np.eye(NSA_KV_HEADS), np.full((HEAD_DIM, HEAD_DIM), 1.0 / HEAD_DIM)), MXU_DTYPE)
    gk = jnp.tile(k_gain, NSA_KV_HEADS).reshape(1, -1)

    per_page = PAGE_SIZE // CMP_STRIDE
    chunk_w = CMP_STRIDE * 4 * NSA_KV_W
    pages = pages.reshape(-1, per_page, chunk_w)

    def page_spec(p):
        return pl.BlockSpec((1, per_page, chunk_w), lambda n, pt: (pt[n, p], 0, 0))

    def const(shape):
        return pl.BlockSpec(shape, lambda n, pt: (0,) * len(shape), pipeline_mode=pl.Buffered(1))

    out = jax.ShapeDtypeStruct((n_seq, nb, NSA_KV_W), F32)
    out_spec = pl.BlockSpec((1, nb, NSA_KV_W), lambda n, pt: (n, 0, 0))
    grid_spec = pltpu.PrefetchScalarGridSpec(
        num_scalar_prefetch=1, grid=(n_seq,),
        in_specs=[page_spec(p) for p in range(n_pages)] + [
            const((CMP_BLOCK, NSA_KV_W, NSA_KV_W)), const((CMP_BLOCK, NSA_KV_W, NSA_KV_W)),
            const((CMP_BLOCK, NSA_KV_W)), const((CMP_BLOCK, NSA_KV_W)), const((NSA_KV_W, NSA_KV_W)),
            const((1, NSA_KV_W))],
        out_specs=[out_spec, out_spec])
    return pl.pallas_call(
        functools.partial(_nsa_compress_body, n_pages=n_pages),
        grid_spec=grid_spec, out_shape=[out, out], compiler_params=_params(), name="nsa_compress",
    )(table, *([pages] * n_pages), w4[0], w4[1], pe4[0], pe4[1], b64, gk)


def _group_masks():
    lane = lax.broadcasted_iota(jnp.int32, (1, NSA_KV_W), 1)
    return [(lane // HEAD_DIM) == g for g in range(NSA_KV_HEADS)]


def _blockdiag(q, bm):
    zero = jnp.zeros((), q.dtype)
    return jnp.concatenate([jnp.where(bm[g], q[:, NSA_KV_W * j:NSA_KV_W * (j + 1)], zero)
                            for g in range(NSA_KV_HEADS) for j in range(NSA_HPG)], axis=0)


def _extract(obd, bm, tq):
    outs = []
    for j in range(NSA_HPG):
        z = jnp.zeros((tq, NSA_KV_W), F32)
        for g in range(NSA_KV_HEADS):
            r0 = (g * NSA_HPG + j) * tq
            z = z + jnp.where(bm[g], obd[r0:r0 + tq], 0.0)
        outs.append(z)
    return jnp.concatenate(outs, axis=1)


def _per_head_rows(a, tq):
    k = a.shape[1]
    a4 = jnp.broadcast_to(a.reshape(NSA_KV_HEADS, 1, tq, k), (NSA_KV_HEADS, NSA_HPG, tq, k))
    return a4.reshape(NSA_HEADS * tq, k)


def _topk_mask(score, n_sel, tq):
    if tq == LANES:
        nsp = -(-n_sel // 8) * 8
        sub = lax.broadcasted_iota(jnp.int32, (nsp, 1), 0)
        outs = []
        for g in range(NSA_KV_HEADS):
            st = score[g * tq:(g + 1) * tq].T[:nsp]
            rank = jnp.zeros(st.shape, F32)
            for s2 in range(n_sel):
                row = st[s2:s2 + 1, :]
                rank = rank + jnp.where(row > st, 1.0, jnp.where(row == st, jnp.where(sub > s2, 1.0, 0.0), 0.0))
            sel_t = jnp.where(rank < SLC_TOPN, 1.0, 0.0)
            sel_t = jnp.concatenate([sel_t, jnp.zeros((LANES - nsp, tq), F32)], axis=0)
            outs.append(sel_t.T)
        return jnp.concatenate(outs, axis=0)
    lane = lax.broadcasted_iota(jnp.int32, (1, LANES), 1)
    rank = jnp.zeros(score.shape, F32)
    for s2 in range(n_sel):
        col = score[:, s2:s2 + 1]
        rank = rank + jnp.where(col > score, 1.0, jnp.where(col == score, jnp.where(lane > s2, 1.0, 0.0), 0.0))
    return jnp.where(rank < SLC_TOPN, 1.0, 0.0)


def _cmp_and_select(qbd_n, kc, vc, ov, pos_base, tq, n_sel):
    nb = kc.shape[0]
    r = lax.broadcasted_iota(jnp.int32, (NSA_HEADS * tq, 1), 0)
    qpos = pos_base + (r & (tq - 1))
    blk_end = lax.broadcasted_iota(jnp.int32, (1, nb), 1) * CMP_STRIDE + (CMP_BLOCK - 1)
    visible = blk_end <= qpos
    s = jnp.where(visible, _mm_nt(qbd_n, kc) * ATT_SCALE, NEG_INF)
    e = jnp.exp(s - jnp.max(s, axis=-1, keepdims=True))
    p = jnp.where(visible, e / jnp.sum(e, axis=-1, keepdims=True), 0.0)
    o = _mm(p, vc)
    psum = jnp.sum(p.reshape(NSA_KV_HEADS, NSA_HPG, tq, nb), axis=1).reshape(NSA_KV_HEADS * tq, nb)
    imp = _mm_split(psum, ov.astype(MXU_DTYPE))
    r4 = lax.broadcasted_iota(jnp.int32, (NSA_KV_HEADS * tq, 1), 0)
    qblk = (pos_base + (r4 & (tq - 1))) // SLC_BLOCK
    sidx = lax.broadcasted_iota(jnp.int32, (1, LANES), 1)
    bonus = jnp.where(sidx == 0, FORCE_BONUS, jnp.where(sidx == qblk, FORCE_BONUS,
                                                        jnp.where(sidx == qblk - 1, FORCE_BONUS, 0.0)))
    score = jnp.where(sidx <= qblk, imp + bonus, NEG_INF)
    return o, _topk_mask(score, n_sel, tq), qpos


def _gated_sum(gates, gx, o_cmp, o_slc, o_win, bm, tq):
    gf = _mm_split(gates, gx)
    return (gf[:, 0:NSA_Q_W] * _extract(o_cmp, bm, tq) + gf[:, NSA_Q_W:2 * NSA_Q_W] * _extract(o_slc, bm, tq)
            + gf[:, 2 * NSA_Q_W:] * _extract(o_win, bm, tq))


def _nsa_prompt_body(qn_ref, qr_ref, gates_ref, kc_ref, vc_ref, kv_ref, win_ref, ov_ref, gx_ref, o_ref,
                     *, seq, n_sel, kt, wk):
    tq = Q_TILE
    t0 = pl.program_id(1) * tq
    bm = _group_masks()
    w = NSA_KV_W
    o_cmp, sel, qpos = _cmp_and_select(_blockdiag(qn_ref[...], bm), kc_ref[0], vc_ref[0], ov_ref[...], t0, tq, n_sel)
    qbd = _blockdiag(qr_ref[...], bm)
    sel_b = sel.astype(MXU_DTYPE)
    sidx = lax.broadcasted_iota(jnp.int32, (LANES, 1), 0)

    def kv_tile(jt, carry):
        m, l, acc = carry
        k0 = pl.multiple_of(jt * kt, kt)
        kpos = k0 + lax.broadcasted_iota(jnp.int32, (1, kt), 1)
        expand = jnp.where((kpos // SLC_BLOCK) == sidx, 1.0, 0.0).astype(MXU_DTYPE)
        bias = (jnp.dot(sel_b, expand, preferred_element_type=F32) - 1.0) * (-NEG_INF)
        s = _mm_nt(qbd, kv_ref[pl.ds(k0, kt), 0:w]) * ATT_SCALE + _per_head_rows(bias, tq)
        s = jnp.where(kpos <= qpos, s, NEG_INF)
        m_new = jnp.maximum(m, jnp.max(s, axis=-1, keepdims=True))
        alpha = jnp.exp(m - m_new)
        e = jnp.exp(s - m_new)
        l = alpha * l + jnp.sum(e, axis=-1, keepdims=True)
        acc = alpha * acc + _mm(e, kv_ref[pl.ds(k0, kt), w:2 * w])
        return m_new, l, acc

    rows = NSA_HEADS * tq
    n_tiles = (t0 + tq + kt - 1) // kt
    m, l, acc = lax.fori_loop(0, n_tiles, kv_tile, (jnp.full((rows, 1), SOFTMAX_M0, F32), jnp.zeros((rows, 1), F32),
                                                    jnp.zeros((rows, w), F32)))
    o_slc = acc * (1.0 / l)

    start = pl.multiple_of(jnp.maximum(t0 + tq - wk, 0), tq)
    kpos = start + lax.broadcasted_iota(jnp.int32, (1, wk), 1)
    s = _mm_nt(qbd, win_ref[pl.ds(start, wk), 0:w]) * ATT_SCALE
    s = jnp.where(kpos <= qpos, s, NEG_INF)
    s = jnp.where(kpos > qpos - WINDOW, s, NEG_INF)
    e = jnp.exp(s - jnp.max(s, axis=-1, keepdims=True))
    o_win = _mm(e, win_ref[pl.ds(start, wk), w:2 * w]) * (1.0 / jnp.sum(e, axis=-1, keepdims=True))
    o_ref[...] = _gated_sum(gates_ref[...], gx_ref[...], o_cmp, o_slc, o_win, bm, tq).astype(o_ref.dtype)


def nsa_attn_prompt(qn, qr, gates, kc, vc, rows_new, win_new, n_seq, seq):
    tq = Q_TILE
    nb = kc.shape[1]
    n_sel = -(-seq // SLC_BLOCK)
    kt = min(KV_TILE, seq)
    wk = min(WINDOW + tq, seq)
    ov = jnp.asarray(_overlap_matrix(nb, seq))
    gx = jnp.asarray(_gate_expand(), MXU_DTYPE)
    per = seq // tq

    def qrows(width):
        return pl.BlockSpec((tq, width), lambda n, t: (n * per + t, 0))

    def per_seq(shape, lane_block=0):
        return pl.BlockSpec(shape, lambda n, t: (n,) + (0,) * (len(shape) - 2) + (lane_block,))

    return pl.pallas_call(
        functools.partial(_nsa_prompt_body, seq=seq, n_sel=n_sel, kt=kt, wk=wk),
        grid=(n_seq, per),
        in_specs=[qrows(NSA_Q_W), qrows(NSA_Q_W), qrows(LANES), per_seq((1, nb, NSA_KV_W)), per_seq((1, nb, NSA_KV_W)),
                  per_seq((seq, 2 * NSA_KV_W), 1), per_seq((seq, 2 * NSA_KV_W)),
                  _resident((nb, LANES)), _resident((LANES, 3 * NSA_Q_W))],
        out_specs=qrows(NSA_Q_W),
        out_shape=jax.ShapeDtypeStruct((n_seq * seq, NSA_Q_W), MXU_DTYPE),
        compiler_params=_params(2),
        name="nsa_attn_prompt",
    )(qn, qr, gates, kc, vc, rows_new, win_new, ov, gx)


def _nsa_sample_body(pt_ref, *refs, n_pages, past_len, ts, n_sel):
    del pt_ref
    page_refs = refs[:n_pages]
    (qn_ref, qr_ref, gates_ref, kc_ref, vc_ref, rnew_ref, wold_ref, wnew_ref, ov_ref, gx_ref, ex_ref,
     o_ref, wout_ref) = refs[n_pages:]
    tq = TS_PAD
    w = NSA_KV_W
    bm = _group_masks()
    o_cmp, sel, qpos = _cmp_and_select(_blockdiag(qn_ref[0].astype(F32), bm), kc_ref[0], vc_ref[0], ov_ref[...],
                                       past_len, tq, n_sel)
    qbd = _blockdiag(qr_ref[0].astype(F32), bm).astype(MXU_DTYPE)
    pad = jnp.zeros((PAGE_SIZE - tq, w), F32)

    bias = (jnp.dot(sel.astype(MXU_DTYPE), ex_ref[...], preferred_element_type=F32) - 1.0) * (-NEG_INF)
    k_new = jnp.concatenate([rnew_ref[0][:, 0:w], pad], axis=0)
    v_new = jnp.concatenate([rnew_ref[0][:, w:2 * w], pad], axis=0)
    s = jnp.concatenate([_mm_nt(qbd, r[0][:, 0:w]) for r in page_refs] + [_mm_nt(qbd, k_new)], axis=1)
    s = s * ATT_SCALE + _per_head_rows(bias, tq)
    kpos = lax.broadcasted_iota(jnp.int32, (1, (n_pages + 1) * PAGE_SIZE), 1)
    s = jnp.where(kpos <= qpos, s, NEG_INF)
    e = jnp.exp(s - jnp.max(s, axis=-1, keepdims=True))
    acc = _mm(e[:, n_pages * PAGE_SIZE:], v_new)
    for p, r in enumerate(page_refs):
        acc = acc + _mm(e[:, p * PAGE_SIZE:(p + 1) * PAGE_SIZE], r[0][:, w:2 * w])
    o_slc = acc * (1.0 / jnp.sum(e, axis=-1, keepdims=True))

    wlen = wold_ref.shape[1]
    kw_new = jnp.concatenate([wnew_ref[0][:, 0:w], pad], axis=0)
    vw_new = jnp.concatenate([wnew_ref[0][:, w:2 * w], pad], axis=0)
    s = jnp.concatenate([_mm_nt(qbd, wold_ref[0][:, 0:w]), _mm_nt(qbd, kw_new)], axis=1) * ATT_SCALE
    kpos = (past_len - wlen) + lax.broadcasted_iota(jnp.int32, (1, wlen + PAGE_SIZE), 1)
    s = jnp.where(kpos <= qpos, s, NEG_INF)
    s = jnp.where(kpos > qpos - WINDOW, s, NEG_INF)
    e = jnp.exp(s - jnp.max(s, axis=-1, keepdims=True))
    acc = _mm(e[:, 0:wlen], wold_ref[0][:, w:2 * w]) + _mm(e[:, wlen:], vw_new)
    o_win = acc * (1.0 / jnp.sum(e, axis=-1, keepdims=True))

    o_ref[0] = _gated_sum(gates_ref[0], gx_ref[...], o_cmp, o_slc, o_win, bm, tq)
    wout_ref[0, 0:wlen - ts, :] = wold_ref[0, ts:wlen, :]
    wout_ref[0, wlen - ts:wlen, :] = wnew_ref[0, 0:ts, :]


def nsa_attn_sample(table, pages, qn, qr, gates, kc, vc, rows_new, win_old, win_new, win_base, past_len, ts):
    n_seq, n_pages = table.shape
    nb = kc.shape[1]
    wlen = win_old.shape[1]
    length = past_len + ts
    n_sel = -(-length // SLC_BLOCK)
    n_keys = (n_pages + 1) * PAGE_SIZE
    ov = jnp.asarray(_overlap_matrix(nb, length))
    gx = jnp.asarray(_gate_expand(), MXU_DTYPE)
    ex = jnp.asarray((np.arange(n_keys)[None, :] // SLC_BLOCK == np.arange(LANES)[:, None]).astype(np.float32), MXU_DTYPE)

    def page_spec(p):
        return pl.BlockSpec((1, PAGE_SIZE, 2 * NSA_KV_W), lambda n, pt: (pt[n, p], 0, 1))

    def per_seq(shape, lane_block=0, base=0):
        return pl.BlockSpec(shape, lambda n, pt: (base + n,) + (0,) * (len(shape) - 2) + (lane_block,))

    def const(shape):
        return pl.BlockSpec(shape, lambda n, pt: (0,) * len(shape), pipeline_mode=pl.Buffered(1))

    grid_spec = pltpu.PrefetchScalarGridSpec(
        num_scalar_prefetch=1, grid=(n_seq,),
        in_specs=[page_spec(p) for p in range(n_pages)] + [
            per_seq((1, TS_PAD, NSA_Q_W)), per_seq((1, TS_PAD, NSA_Q_W)), per_seq((1, TS_PAD, LANES)),
            per_seq((1, nb, NSA_KV_W)), per_seq((1, nb, NSA_KV_W)), per_seq((1, TS_PAD, 2 * NSA_KV_W), 1),
            per_seq((1, wlen, 2 * NSA_KV_W), 0, win_base), per_seq((1, TS_PAD, 2 * NSA_KV_W)),
            const((nb, LANES)), const((LANES, 3 * NSA_Q_W)), const((LANES, n_keys))],
        out_specs=[per_seq((1, TS_PAD, NSA_Q_W)), per_seq((1, wlen, 2 * NSA_KV_W))])
    return pl.pallas_call(
        functools.partial(_nsa_sample_body, n_pages=n_pages, past_len=past_len, ts=ts, n_sel=n_sel),
        grid_spec=grid_spec,
        out_shape=[jax.ShapeDtypeStruct((n_seq, TS_PAD, NSA_Q_W), F32),
                   jax.ShapeDtypeStruct((n_seq, wlen, 2 * NSA_KV_W), F32)],
        compiler_params=_params(), name="nsa_attn_sample",
    )(table, *([pages] * n_pages), qn, qr, gates, kc, vc, rows_new, win_old, win_new, ov, gx, ex)


def rms_norm(x, g):
    x32 = x.astype(jnp.float32)
    y = x32 * lax.rsqrt(jnp.mean(x32 * x32, axis=-1, keepdims=True) + RMS_EPS)
    return (y * g.astype(jnp.float32)).astype(x.dtype)


def _cmul_scan_op(e1, e2):
    a1r, a1i, b1r, b1i = e1
    a2r, a2i, b2r, b2i = e2
    return (a1r * a2r - a1i * a2i,
            a1r * a2i + a1i * a2r,
            a2r * b1r - a2i * b1i + b2r,
            a2r * b1i + a2i * b1r + b2i)


def s5_core(u, h0, lam_re, lam_im, log_dt, b_re, b_im, c_re, c_im, d_skip):
    N, T, _ = u.shape
    u32 = u.astype(jnp.float32)
    ug = u32.reshape(N, T, S5_GROUPS, S5_GROUP_CH)
    dt = jnp.exp(log_dt.astype(jnp.float32))[:, None]
    lr = jnp.minimum(lam_re.astype(jnp.float32), -1e-4)
    li = lam_im.astype(jnp.float32)
    mag = jnp.exp(lr * dt)
    a_re = mag * jnp.cos(li * dt)
    a_im = mag * jnp.sin(li * dt)
    den = lr * lr + li * li
    z_re = ((a_re - 1.0) * lr + a_im * li) / den
    z_im = (a_im * lr - (a_re - 1.0) * li) / den
    bb_re = z_re[..., None] * b_re - z_im[..., None] * b_im
    bb_im = z_re[..., None] * b_im + z_im[..., None] * b_re
    bu_re = jnp.einsum('ntgc,gpc->tngp', ug, bb_re)
    bu_im = jnp.einsum('ntgc,gpc->tngp', ug, bb_im)
    if h0 is not None:
        h0r = h0[:, 0].astype(jnp.float32)
        h0i = h0[:, 1].astype(jnp.float32)
        bu_re = bu_re.at[0].add(a_re * h0r - a_im * h0i)
        bu_im = bu_im.at[0].add(a_re * h0i + a_im * h0r)
    ar = jnp.broadcast_to(a_re, (T, 1) + a_re.shape)
    ai = jnp.broadcast_to(a_im, (T, 1) + a_im.shape)
    _, _, h_re, h_im = lax.associative_scan(_cmul_scan_op, (ar, ai, bu_re, bu_im), axis=0)
    y = (jnp.einsum('tngp,gcp->ntgc', h_re, c_re) - jnp.einsum('tngp,gcp->ntgc', h_im, c_im))
    y = y.reshape(N, T, D_MODEL) + d_skip * u32
    z = jax.nn.gelu(y)
    h_last = jnp.stack([h_re[-1], h_im[-1]], axis=1)
    return z, h_last


def gated_chunk_scan(q, k, v, logf, s0):
    N, T, H, dk = q.shape
    dv = v.shape[-1]
    C = min(HG_CHUNK, T)
    nc = -(-T // C)
    pad = nc * C - T

    def chunks(a):
        a = jnp.pad(a, ((0, 0), (0, pad), (0, 0), (0, 0)))
        return a.reshape(N, nc, C, H, a.shape[-1]).transpose(1, 0, 3, 2, 4)

    causal = jnp.tril(jnp.ones((C, C), dtype=bool))

    def step(S, xs):
        qc, kc, vc, lc = xs
        G = jnp.cumsum(lc, axis=2)
        diff = jnp.where(causal[:, :, None], G[:, :, :, None, :] - G[:, :, None, :, :], NEG_INF)
        A = jnp.einsum('nhtsd,nhsd->nhts', qc[:, :, :, None, :] * jnp.exp(diff), kc)
        o = (jnp.einsum('nhcd,nhde->nhce', qc * jnp.exp(G), S)
             + jnp.einsum('nhts,nhse->nhte', A, vc))
        G_last = G[:, :, -1]
        S = (jnp.exp(G_last)[..., None] * S
             + jnp.einsum('nhsd,nhse->nhde', kc * jnp.exp(G_last[:, :, None] - G), vc))
        return S, o

    S, o = lax.scan(step, s0, (chunks(q), chunks(k), chunks(v), chunks(logf)))
    o = o.transpose(1, 0, 3, 2, 4).reshape(N, nc * C, H, dv)[:, :T]
    return o, S


def hgrn_core(pr, s0, o_gain, lb):
    N, T, _ = pr.shape
    q, fz, v, g = jnp.split(pr, 4, axis=-1)
    f = lb + (1.0 - lb) * jax.nn.sigmoid(fz.astype(jnp.float32))

    def heads(a, d):
        return a.reshape(N, T, HG_HEADS, d)

    o, s_last = gated_chunk_scan(heads(q.astype(jnp.float32), HG_DK), heads(1.0 - f, HG_DK),
                                 heads(v.astype(jnp.float32), HG_DV), heads(jnp.log(f), HG_DK),
                                 s0.astype(jnp.float32))
    o = rms_norm(o, o_gain) * heads(jax.nn.silu(g.astype(jnp.float32)), HG_DV)
    return o.reshape(N, T, D_MODEL), s_last


def kernel(x_prompt, x_sample, cache_nsa, state_nsa_win, state_s5, state_hgrn, page_table, p_prompt, p_sample, norm_gain, ffn_w_in, ffn_w_out, ple_w_gate, ple_w_proj, nsa_w_in, nsa_w_out, nsa_qk_gain, nsa_cmp_pe, nsa_cmp_w, s5_lam_re, s5_lam_im, s5_log_dt, s5_b_re, s5_b_im, s5_c_re, s5_c_im, s5_d, s5_w_glu, hg_w_in, hg_w_out, hg_o_gain, hg_lb_raw):
    B, T, _ = x_prompt.shape
    Bs, Ts, _ = x_sample.shape
    rp = B * T
    rs = Bs * Ts
    n_pages = page_table.shape[1]
    past_len = n_pages * PAGE_SIZE
    n_phys = cache_nsa.shape[1]
    wlen = state_nsa_win.shape[2]
    assert T % Q_TILE == 0 and T % PAGE_SIZE == 0 and T % min(KV_TILE, T) == 0 and Ts <= TS_PAD
    lb_sm = jax.nn.softmax(hg_lb_raw.astype(F32), axis=0)
    lower_bounds = jnp.cumsum(lb_sm, axis=0) - lb_sm[0]

    x = jnp.concatenate([x_prompt.reshape(rp, D_MODEL), x_sample.reshape(rs, D_MODEL)], axis=0)
    p_all = jnp.concatenate([p_prompt.reshape(DEPTH, rp, PLE_DIM), p_sample.reshape(DEPTH, rs, PLE_DIM)], axis=1)
    pos = jnp.concatenate([jnp.tile(jnp.arange(T), B), jnp.tile(past_len + jnp.arange(Ts), Bs)])
    cache_pages = cache_nsa.reshape(-1, PAGE_SIZE, 4 * NSA_KV_W)
    win_state = state_nsa_win.reshape(-1, wlen, 2 * NSA_KV_W)
    prompt_table = (jnp.arange(B, dtype=jnp.int32)[:, None] * (T // PAGE_SIZE)
                    + jnp.arange(T // PAGE_SIZE, dtype=jnp.int32)[None, :])
    out_perm = _head_perm()

    def split(a):
        return a[:rp].reshape(B, T, -1), a[rp:].reshape(Bs, Ts, -1)

    def join(a, b):
        return jnp.concatenate([a.reshape(rp, -1), b.reshape(rs, -1)], axis=0)

    def sample_pad(a):
        return jnp.pad(a[rp:].reshape(Bs, Ts, -1), ((0, 0), (0, TS_PAD - Ts), (0, 0)))

    outs_p = {0: [], 1: [], 2: [], 3: []}
    outs_s = {0: [], 1: [], 2: [], 3: []}
    for i in range(DEPTH):
        kind = LAYER_KIND[i]
        j = LAYER_SLOT[i]
        g = norm_gain[i]
        x, xn = ffn_step(x, g[0], g[1], ffn_w_in[i, 0], ffn_w_out[i, 0])
        if kind == 0:
            qn, qr, rows_new, win_new, gates = nsa_proj(xn, nsa_w_in[j], nsa_qk_gain[j], pos)
            cmp_args = (nsa_cmp_pe[j], nsa_cmp_w[j], nsa_qk_gain[j, 1])
            kc_p, vc_p = nsa_compress(rows_new.reshape(-1, PAGE_SIZE, 4 * NSA_KV_W), prompt_table, *cmp_args)
            o_p = nsa_attn_prompt(qn, qr, gates, kc_p, vc_p, rows_new, win_new, B, T)
            table = page_table.astype(jnp.int32) + j * n_phys
            kc_s, vc_s = nsa_compress(cache_pages, table, *cmp_args)
            o_s, w_s = nsa_attn_sample(table, cache_pages, sample_pad(qn), sample_pad(qr), sample_pad(gates), kc_s, vc_s,
                                       sample_pad(rows_new), win_state, sample_pad(win_new), j * Bs, past_len, Ts)
            r_p, r_s = split(rows_new)
            outs_p[0].append(r_p.reshape(B, T, 4, NSA_KV_HEADS, HEAD_DIM))
            outs_s[0].append(r_s.reshape(Bs, Ts, 4, NSA_KV_HEADS, HEAD_DIM))
            buf = min(WINDOW, T)
            outs_p[1].append(win_new[:rp].reshape(B, T, 2, NSA_KV_HEADS, HEAD_DIM)[:, T - buf:])
            outs_s[1].append(w_s.reshape(Bs, wlen, 2, NSA_KV_HEADS, HEAD_DIM))
            o_all = jnp.concatenate([o_p, o_s[:, :Ts].reshape(rs, NSA_Q_W).astype(o_p.dtype)], axis=0)
            x = resid_proj(x, o_all, nsa_w_out[j][out_perm])
        elif kind == 1:
            u_p, u_s = split(xn)
            args = (s5_lam_re[j], s5_lam_im[j], s5_log_dt[j], s5_b_re[j], s5_b_im[j], s5_c_re[j], s5_c_im[j], s5_d[j])
            z_p, h_p = s5_core(u_p, None, *args)
            z_s, h_s = s5_core(u_s, state_s5[j], *args)
            outs_p[2].append(h_p); outs_s[2].append(h_s)
            ab = proj(join(z_p, z_s), s5_w_glu[j])
            x = x + ab[:, :D_MODEL] * jax.nn.sigmoid(ab[:, D_MODEL:])
        else:
            pr_p, pr_s = split(proj(xn, hg_w_in[j]))
            o_p, s_p = hgrn_core(pr_p, jnp.zeros((B, HG_HEADS, HG_DK, HG_DV), F32), hg_o_gain[j], lower_bounds[i])
            o_s, s_s = hgrn_core(pr_s, state_hgrn[j], hg_o_gain[j], lower_bounds[i])
            outs_p[3].append(s_p); outs_s[3].append(s_s)
            x = resid_proj(x, join(o_p, o_s), hg_w_out[j])
        x, xn = ffn_step(x, g[2], g[3], ffn_w_in[i, 1], ffn_w_out[i, 1])
        x = ple_step(x, xn, p_all[i], ple_w_gate[i], ple_w_proj[i])

    y_p, y_s = split(x)
    return (y_p, y_s,
            jnp.stack(outs_p[0]), jnp.stack(outs_p[1]), jnp.stack(outs_p[2]), jnp.stack(outs_p[3]),
            jnp.stack(outs_s[0]), jnp.stack(outs_s[1]), jnp.stack(outs_s[2]), jnp.stack(outs_s[3]))
```

```python
import functools
import math

import numpy as np
import jax
import jax.numpy as jnp
from jax import lax
from jax.experimental import pallas as pl
from jax.experimental.pallas import tpu as pltpu

F32 = jnp.float32
MXU_DTYPE = jnp.bfloat16

D_MODEL = 1024
DEPTH = 4
PAGE_SIZE = 128
D_FF = 2816
PLE_DIM = 256
RMS_EPS = 1e-6
LAYER_KIND = (0, 1, 2, 0)
LAYER_SLOT = (0, 0, 0, 1)

NSA_HEADS = 16
NSA_KV_HEADS = 4
HEAD_DIM = 64
NSA_HPG = NSA_HEADS // NSA_KV_HEADS
NSA_KV_W = NSA_KV_HEADS * HEAD_DIM
NSA_Q_W = NSA_HEADS * HEAD_DIM
NSA_IN = NSA_Q_W + 6 * NSA_KV_W + 3 * NSA_HEADS
CMP_BLOCK = 32
CMP_STRIDE = 16
SLC_BLOCK = 64
SLC_TOPN = 16
WINDOW = 512
FORCE_BONUS = 1e4
NEG_INF = -1e30
ROPE_THETA = 500000.0
ROPE_DIMS = HEAD_DIM // 4
ATT_SCALE = HEAD_DIM ** -0.5

S5_GROUP_CH = 16
S5_GROUPS = D_MODEL // S5_GROUP_CH
S5_STATE = 64

HG_DK = 128
HG_HEADS = D_MODEL // HG_DK
HG_DV = D_MODEL // HG_HEADS
HG_CHUNK = 64

V7X_VMEM_BYTES = 64 * 1024 * 1024
VMEM_LIMIT = V7X_VMEM_BYTES - 8 * 1024 * 1024
LANES = 128
ROW_TILE = 512
FF_CHUNK = 256
Q_TILE = 128
KV_TILE = 512
TS_PAD = 8
SOFTMAX_M0 = -1e29


def _resident(shape):
    return pl.BlockSpec(shape, lambda *_: (0,) * len(shape), pipeline_mode=pl.Buffered(1))


def _row_tile(rows):
    return max(t for t in range(8, ROW_TILE + 1, 8) if rows % t == 0)


def _rows(tm, width):
    return pl.BlockSpec((tm, width), lambda i: (i, 0))


def _params(n_axes=1):
    return pltpu.CompilerParams(dimension_semantics=("parallel",) * n_axes, vmem_limit_bytes=VMEM_LIMIT)


def _rms(x, g):
    return x * lax.rsqrt(jnp.mean(x * x, axis=-1, keepdims=True) + RMS_EPS) * g


def _mm(a, b):
    return jnp.dot(a.astype(MXU_DTYPE), b.astype(MXU_DTYPE), preferred_element_type=F32)


def _mm_nt(a, b):
    return lax.dot_general(a.astype(MXU_DTYPE), b.astype(MXU_DTYPE), (((1,), (1,)), ((), ())),
                           preferred_element_type=F32)


def _mm_split(a, b):
    hi = a.astype(MXU_DTYPE)
    lo = (a - hi.astype(F32)).astype(MXU_DTYPE)
    return (jnp.dot(hi, b, preferred_element_type=F32) + jnp.dot(lo, b, preferred_element_type=F32))


def _ffn_body(x_ref, g_ref, gn_ref, win_ref, wout_ref, o_ref, on_ref):
    x = x_ref[...]
    xb = _rms(x, g_ref[...]).astype(MXU_DTYPE)
    acc = jnp.zeros(x.shape, F32)
    for c in range(D_FF // FF_CHUNK):
        lo = c * FF_CHUNK
        a = _mm(xb, win_ref[:, lo:lo + FF_CHUNK])
        b = _mm(xb, win_ref[:, D_FF + lo:D_FF + lo + FF_CHUNK])
        h = a * jax.nn.sigmoid(a) * b
        acc = acc + _mm(h, wout_ref[lo:lo + FF_CHUNK, :])
    y = x + 0.5 * acc
    o_ref[...] = y
    on_ref[...] = _rms(y, gn_ref[...])


def ffn_step(x, g, g_next, w_in, w_out):
    rows = x.shape[0]
    tm = _row_tile(rows)
    out = jax.ShapeDtypeStruct((rows, D_MODEL), F32)
    return pl.pallas_call(
        _ffn_body,
        grid=(rows // tm,),
        in_specs=[_rows(tm, D_MODEL), _resident((1, D_MODEL)), _resident((1, D_MODEL)),
                  _resident((D_MODEL, 2 * D_FF)), _resident((D_FF, D_MODEL))],
        out_specs=[_rows(tm, D_MODEL), _rows(tm, D_MODEL)],
        out_shape=[out, out],
        compiler_params=_params(),
        name="ffn_step",
    )(x, g.reshape(1, -1), g_next.reshape(1, -1), w_in.astype(MXU_DTYPE), w_out.astype(MXU_DTYPE))


def _ple_body(x_ref, xn_ref, p_ref, wg_ref, wp_ref, o_ref):
    gate = jax.nn.sigmoid(_mm(xn_ref[...], wg_ref[...]))
    o_ref[...] = x_ref[...] + gate * _mm(p_ref[...], wp_ref[...])


def ple_step(x, xn, p, w_gate, w_proj):
    rows = x.shape[0]
    tm = _row_tile(rows)
    return pl.pallas_call(
        _ple_body,
        grid=(rows // tm,),
        in_specs=[_rows(tm, D_MODEL), _rows(tm, D_MODEL), _rows(tm, PLE_DIM),
                  _resident((D_MODEL, D_MODEL)), _resident((PLE_DIM, D_MODEL))],
        out_specs=_rows(tm, D_MODEL),
        out_shape=jax.ShapeDtypeStruct((rows, D_MODEL), F32),
        compiler_params=_params(),
        name="ple_step",
    )(x, xn, p, w_gate.astype(MXU_DTYPE), w_proj.astype(MXU_DTYPE))


def _proj_body(a_ref, w_ref, o_ref):
    o_ref[...] = _mm(a_ref[...], w_ref[...])


def proj(a, w):
    rows, k = a.shape
    n = w.shape[1]
    tm = _row_tile(rows)
    return pl.pallas_call(
        _proj_body,
        grid=(rows // tm,),
        in_specs=[_rows(tm, k), _resident((k, n))],
        out_specs=_rows(tm, n),
        out_shape=jax.ShapeDtypeStruct((rows, n), F32),
        compiler_params=_params(),
        name="proj",
    )(a, w.astype(MXU_DTYPE))


def _resid_body(x_ref, a_ref, w_ref, o_ref):
    o_ref[...] = x_ref[...] + _mm(a_ref[...], w_ref[...])


def resid_proj(x, a, w):
    rows, k = a.shape
    tm = _row_tile(rows)
    return pl.pallas_call(
        _resid_body,
        grid=(rows // tm,),
        in_specs=[_rows(tm, D_MODEL), _rows(tm, k), _resident((k, D_MODEL))],
        out_specs=_rows(tm, D_MODEL),
        out_shape=jax.ShapeDtypeStruct((rows, D_MODEL), F32),
        compiler_params=_params(),
        name="resid_proj",
    )(x, a, w.astype(MXU_DTYPE))


def _head_perm():
    idx = np.arange(NSA_Q_W).reshape(NSA_KV_HEADS, NSA_HPG, HEAD_DIM)
    return idx.transpose(1, 0, 2).reshape(-1)


def _gate_expand():
    x = np.zeros((LANES, 3 * NSA_Q_W), np.float32)
    for b in range(3):
        for g in range(NSA_KV_HEADS):
            for j in range(NSA_HPG):
                h = g * NSA_HPG + j
                c0 = b * NSA_Q_W + (j * NSA_KV_HEADS + g) * HEAD_DIM
                x[b * NSA_HEADS + h, c0:c0 + HEAD_DIM] = 1.0
    return x


def _overlap_matrix(nb, length):
    n_sel = -(-length // SLC_BLOCK)
    c0 = np.arange(nb)[:, None] * CMP_STRIDE
    s0 = np.arange(LANES)[None, :] * SLC_BLOCK
    ov = np.clip(np.minimum(c0 + CMP_BLOCK, s0 + SLC_BLOCK) - np.maximum(c0, s0), 0, None) / CMP_BLOCK
    ov = np.where(np.arange(LANES)[None, :] < n_sel, ov, 0.0)
    return ov.astype(np.float32)


def _rope_tables(pos):
    half = ROPE_DIMS // 2
    inv = ROPE_THETA ** (-jnp.arange(half, dtype=F32) / half)
    ang = pos.astype(F32)[:, None] * inv[None, :]
    cos, sin = jnp.cos(ang), jnp.sin(ang)
    ones = jnp.ones((pos.shape[0], HEAD_DIM - ROPE_DIMS), F32)
    zeros = jnp.zeros((pos.shape[0], HEAD_DIM - ROPE_DIMS), F32)
    zh = jnp.zeros_like(sin)
    c = jnp.concatenate([cos, cos, ones], axis=1)
    sa = jnp.concatenate([-sin, zh, zeros], axis=1)
    sb = jnp.concatenate([zh, sin, zeros], axis=1)
    rep = LANES // HEAD_DIM
    return jnp.tile(c, (1, rep)), jnp.tile(sa, (1, rep)), jnp.tile(sb, (1, rep))


def _nsa_proj_body(xn_ref, wq_ref, wkv_ref, wgl_ref, b64_ref, gq_ref, gk_ref, c_ref, sa_ref, sb_ref,
                   qn_ref, qr_ref, rows_ref, win_ref, gates_ref):
    xb = xn_ref[...].astype(MXU_DTYPE)
    c, sa, sb = c_ref[...], sa_ref[...], sb_ref[...]

    def head_norm(v, gain):
        w = v.shape[1]
        ms = _mm_split(v * v, b64_ref[:w, :w])
        return v * lax.rsqrt(ms + RMS_EPS) * gain

    def rope(v):
        w = v.shape[1]
        rep = w // LANES
        ct, sat, sbt = (jnp.concatenate([t] * rep, axis=1) for t in (c, sa, sb))
        return v * ct + pltpu.roll(v, w - ROPE_DIMS // 2, 1) * sat + pltpu.roll(v, ROPE_DIMS // 2, 1) * sbt

    qn = head_norm(_mm(xb, wq_ref[...]), gq_ref[...])
    qn_ref[...] = qn.astype(qn_ref.dtype)
    qr_ref[...] = rope(qn).astype(qr_ref.dtype)
    kv = _mm(xb, wkv_ref[...])
    w = NSA_KV_W
    rows_ref[:, 0:2 * w] = kv[:, 0:2 * w]
    rows_ref[:, 2 * w:3 * w] = rope(head_norm(kv[:, 2 * w:3 * w], gk_ref[0:1, :]))
    rows_ref[:, 3 * w:4 * w] = kv[:, 3 * w:4 * w]
    win_ref[:, 0:w] = rope(head_norm(kv[:, 4 * w:5 * w], gk_ref[1:2, :]))
    win_ref[:, w:2 * w] = kv[:, 5 * w:6 * w]
    gates_ref[...] = jax.nn.sigmoid(_mm(xb, wgl_ref[...]))


def nsa_proj(xn, w_in, qk_gain, pos):
    rows = xn.shape[0]
    tm = _row_tile(rows)
    kvw = 6 * NSA_KV_W
    wq = w_in[:, :NSA_Q_W][:, _head_perm()].astype(MXU_DTYPE)
    wkv = w_in[:, NSA_Q_W:NSA_Q_W + kvw].astype(MXU_DTYPE)
    wgl = jnp.pad(w_in[:, NSA_Q_W + kvw:], ((0, 0), (0, LANES - 3 * NSA_HEADS))).astype(MXU_DTYPE)
    b64 = jnp.asarray(np.kron(np.eye(NSA_HEADS), np.full((HEAD_DIM, HEAD_DIM), 1.0 / HEAD_DIM)), MXU_DTYPE)
    gq = jnp.tile(qk_gain[0], NSA_HEADS).reshape(1, -1)
    gk = jnp.stack([jnp.tile(qk_gain[2], NSA_KV_HEADS), jnp.tile(qk_gain[3], NSA_KV_HEADS)])
    c, sa, sb = _rope_tables(pos)
    return pl.pallas_call(
        _nsa_proj_body,
        grid=(rows // tm,),
        in_specs=[_rows(tm, D_MODEL), _resident((D_MODEL, NSA_Q_W)), _resident((D_MODEL, kvw)),
                  _resident((D_MODEL, LANES)), _resident((NSA_Q_W, NSA_Q_W)), _resident((1, NSA_Q_W)),
                  _resident((2, NSA_KV_W)), _rows(tm, LANES), _rows(tm, LANES), _rows(tm, LANES)],
        out_specs=[_rows(tm, NSA_Q_W), _rows(tm, NSA_Q_W), _rows(tm, 4 * NSA_KV_W), _rows(tm, 2 * NSA_KV_W),
                   _rows(tm, LANES)],
        out_shape=[jax.ShapeDtypeStruct((rows, NSA_Q_W), MXU_DTYPE), jax.ShapeDtypeStruct((rows, NSA_Q_W), MXU_DTYPE),
                   jax.ShapeDtypeStruct((rows, 4 * NSA_KV_W), F32), jax.ShapeDtypeStruct((rows, 2 * NSA_KV_W), F32),
                   jax.ShapeDtypeStruct((rows, LANES), F32)],
        compiler_params=_params(),
        name="nsa_proj",
    )(xn, wq, wkv, wgl, b64, gq, gk, c, sa, sb)


def _nsa_compress_body(pt_ref, *refs, n_pages):
    del pt_ref
    page_refs = refs[:n_pages]
    wk_ref, wv_ref, pek_ref, pev_ref, b64_ref, gk_ref, kc_ref, vc_ref = refs[n_pages:]
    per_page = PAGE_SIZE // CMP_STRIDE
    nb = n_pages * per_page

    def compress(kind, w_ref, pe_ref):
        lo = kind * NSA_KV_W
        first = jnp.zeros((nb, NSA_KV_W), F32)
        second = jnp.zeros((nb, NSA_KV_W), F32)
        for l in range(CMP_STRIDE):
            c0 = l * 4 * NSA_KV_W + lo
            x = jnp.concatenate([r[0, :, c0:c0 + NSA_KV_W] for r in page_refs], axis=0)
            first = first + _mm(x + pe_ref[l:l + 1, :], w_ref[l])
            second = second + _mm(x + pe_ref[CMP_STRIDE + l:CMP_STRIDE + l + 1, :], w_ref[CMP_STRIDE + l])
        return first + pltpu.roll(second, nb - 1, 0)

    kc = compress(0, wk_ref, pek_ref)
    ms = _mm_split(kc * kc, b64_ref[...])
    kc_ref[0] = kc * lax.rsqrt(ms + RMS_EPS) * gk_ref[...]
    vc_ref[0] = compress(1, wv_ref, pev_ref)


def nsa_compress(pages, table, cmp_pe, cmp_w, k_gain):
    n_seq, n_pages = table.shape
    nb = n_pages * (PAGE_SIZE // CMP_STRIDE)
    eye = jnp.eye(NSA_KV_HEADS, dtype=F32)
    w4 = jnp.einsum('gh,klde->klgdhe', eye, cmp_w).reshape(2, CMP_BLOCK, NSA_KV_W, NSA_KV_W).astype(MXU_DTYPE)
    pe4 = jnp.tile(cmp_pe, (1, 1, NSA_KV_HEADS))
    b64 = jnp.asarray(np.kron(np.eye(NSA_KV_HEADS), np.full((HEAD_DIM, HEAD_DIM), 1.0 / HEAD_DIM)), MXU_DTYPE)
    gk = jnp.tile(k_gain, NSA_KV_HEADS).reshape(1, -1)

    per_page = PAGE_SIZE // CMP_STRIDE
    chunk_w = CMP_STRIDE * 4 * NSA_KV_W
    pages = pages.reshape(-1, per_page, chunk_w)

    def page_spec(p):
        return pl.BlockSpec((1, per_page, chunk_w), lambda n, pt: (pt[n, p], 0, 0))

    def const(shape):
        return pl.BlockSpec(shape, lambda n, pt: (0,) * len(shape), pipeline_mode=pl.Buffered(1))

    out = jax.ShapeDtypeStruct((n_seq, nb, NSA_KV_W), F32)
    out_spec = pl.BlockSpec((1, nb, NSA_KV_W), lambda n, pt: (n, 0, 0))
    grid_spec = pltpu.PrefetchScalarGridSpec(
        num_scalar_prefetch=1, grid=(n_seq,),
        in_specs=[page_spec(p) for p in range(n_pages)] + [
            const((CMP_BLOCK, NSA_KV_W, NSA_KV_W)), const((CMP_BLOCK, NSA_KV_W, NSA_KV_W)),
            const((CMP_BLOCK, NSA_KV_W)), const((CMP_BLOCK, NSA_KV_W)), const((NSA_KV_W, NSA_KV_W)),
            const((1, NSA_KV_W))],
        out_specs=[out_spec, out_spec])
    return pl.pallas_call(
        functools.partial(_nsa_compress_body, n_pages=n_pages),
        grid_spec=grid_spec, out_shape=[out, out], compiler_params=_params(), name="nsa_compress",
    )(table, *([pages] * n_pages), w4[0], w4[1], pe4[0], pe4[1], b64, gk)


def _group_masks():
    lane = lax.broadcasted_iota(jnp.int32, (1, NSA_KV_W), 1)
    return [(lane // HEAD_DIM) == g for g in range(NSA_KV_HEADS)]


def _blockdiag(q, bm):
    zero = jnp.zeros((), q.dtype)
    return jnp.concatenate([jnp.where(bm[g], q[:, NSA_KV_W * j:NSA_KV_W * (j + 1)], zero)
                            for g in range(NSA_KV_HEADS) for j in range(NSA_HPG)], axis=0)


def _extract(obd, bm, tq):
    outs = []
    for j in range(NSA_HPG):
        z = jnp.zeros((tq, NSA_KV_W), F32)
        for g in range(NSA_KV_HEADS):
            r0 = (g * NSA_HPG + j) * tq
            z = z + jnp.where(bm[g], obd[r0:r0 + tq], 0.0)
        outs.append(z)
    return jnp.concatenate(outs, axis=1)


def _per_head_rows(a, tq):
    k = a.shape[1]
    a4 = jnp.broadcast_to(a.reshape(NSA_KV_HEADS, 1, tq, k), (NSA_KV_HEADS, NSA_HPG, tq, k))
    return a4.reshape(NSA_HEADS * tq, k)


def _topk_mask(score, n_sel, tq):
    if tq == LANES:
        nsp = -(-n_sel // 8) * 8
        sub = lax.broadcasted_iota(jnp.int32, (nsp, 1), 0)
        outs = []
        for g in range(NSA_KV_HEADS):
            st = score[g * tq:(g + 1) * tq].T[:nsp]
            rank = jnp.zeros(st.shape, F32)
            for s2 in range(n_sel):
                row = st[s2:s2 + 1, :]
                rank = rank + jnp.where(row > st, 1.0, jnp.where(row == st, jnp.where(sub > s2, 1.0, 0.0), 0.0))
            sel_t = jnp.where(rank < SLC_TOPN, 1.0, 0.0)
            sel_t = jnp.concatenate([sel_t, jnp.zeros((LANES - nsp, tq), F32)], axis=0)
            outs.append(sel_t.T)
        return jnp.concatenate(outs, axis=0)
    lane = lax.broadcasted_iota(jnp.int32, (1, LANES), 1)
    rank = jnp.zeros(score.shape, F32)
    for s2 in range(n_sel):
        col = score[:, s2:s2 + 1]
        rank = rank + jnp.where(col > score, 1.0, jnp.where(col == score, jnp.where(lane > s2, 1.0, 0.0), 0.0))
    return jnp.where(rank < SLC_TOPN, 1.0, 0.0)


def _cmp_and_select(qbd_n, kc, vc, ov, pos_base, tq, n_sel):
    nb = kc.shape[0]
    r = lax.broadcasted_iota(jnp.int32, (NSA_HEADS * tq, 1), 0)
    qpos = pos_base + (r & (tq - 1))
    blk_end = lax.broadcasted_iota(jnp.int32, (1, nb), 1) * CMP_STRIDE + (CMP_BLOCK - 1)
    visible = blk_end <= qpos
    s = jnp.where(visible, _mm_nt(qbd_n, kc) * ATT_SCALE, NEG_INF)
    e = jnp.exp(s - jnp.max(s, axis=-1, keepdims=True))
    p = jnp.where(visible, e / jnp.sum(e, axis=-1, keepdims=True), 0.0)
    o = _mm(p, vc)
    psum = jnp.sum(p.reshape(NSA_KV_HEADS, NSA_HPG, tq, nb), axis=1).reshape(NSA_KV_HEADS * tq, nb)
    imp = _mm_split(psum, ov.astype(MXU_DTYPE))
    r4 = lax.broadcasted_iota(jnp.int32, (NSA_KV_HEADS * tq, 1), 0)
    qblk = (pos_base + (r4 & (tq - 1))) // SLC_BLOCK
    sidx = lax.broadcasted_iota(jnp.int32, (1, LANES), 1)
    bonus = jnp.where(sidx == 0, FORCE_BONUS, jnp.where(sidx == qblk, FORCE_BONUS,
                                                        jnp.where(sidx == qblk - 1, FORCE_BONUS, 0.0)))
    score = jnp.where(sidx <= qblk, imp + bonus, NEG_INF)
    return o, _topk_mask(score, n_sel, tq), qpos


def _gated_sum(gates, gx, o_cmp, o_slc, o_win, bm, tq):
    gf = _mm_split(gates, gx)
    return (gf[:, 0:NSA_Q_W] * _extract(o_cmp, bm, tq) + gf[:, NSA_Q_W:2 * NSA_Q_W] * _extract(o_slc, bm, tq)
            + gf[:, 2 * NSA_Q_W:] * _extract(o_win, bm, tq))


def _nsa_prompt_body(qn_ref, qr_ref, gates_ref, kc_ref, vc_ref, kv_ref, win_ref, ov_ref, gx_ref, o_ref,
                     *, seq, n_sel, kt, wk):
    tq = Q_TILE
    t0 = pl.program_id(1) * tq
    bm = _group_masks()
    w = NSA_KV_W
    o_cmp, sel, qpos = _cmp_and_select(_blockdiag(qn_ref[...], bm), kc_ref[0], vc_ref[0], ov_ref[...], t0, tq, n_sel)
    qbd = _blockdiag(qr_ref[...], bm)
    sel_b = sel.astype(MXU_DTYPE)
    sidx = lax.broadcasted_iota(jnp.int32, (LANES, 1), 0)

    def kv_tile(jt, carry):
        m, l, acc = carry
        k0 = pl.multiple_of(jt * kt, kt)
        kpos = k0 + lax.broadcasted_iota(jnp.int32, (1, kt), 1)
        expand = jnp.where((kpos // SLC_BLOCK) == sidx, 1.0, 0.0).astype(MXU_DTYPE)
        bias = (jnp.dot(sel_b, expand, preferred_element_type=F32) - 1.0) * (-NEG_INF)
        s = _mm_nt(qbd, kv_ref[pl.ds(k0, kt), 0:w]) * ATT_SCALE + _per_head_rows(bias, tq)
        s = jnp.where(kpos <= qpos, s, NEG_INF)
        m_new = jnp.maximum(m, jnp.max(s, axis=-1, keepdims=True))
        alpha = jnp.exp(m - m_new)
        e = jnp.exp(s - m_new)
        l = alpha * l + jnp.sum(e, axis=-1, keepdims=True)
        acc = alpha * acc + _mm(e, kv_ref[pl.ds(k0, kt), w:2 * w])
        return m_new, l, acc

    rows = NSA_HEADS * tq
    n_tiles = (t0 + tq + kt - 1) // kt
    m, l, acc = lax.fori_loop(0, n_tiles, kv_tile, (jnp.full((rows, 1), SOFTMAX_M0, F32), jnp.zeros((rows, 1), F32),
                                                    jnp.zeros((rows, w), F32)))
    o_slc = acc * (1.0 / l)

    start = pl.multiple_of(jnp.maximum(t0 + tq - wk, 0), tq)
    kpos = start + lax.broadcasted_iota(jnp.int32, (1, wk), 1)
    s = _mm_nt(qbd, win_ref[pl.ds(start, wk), 0:w]) * ATT_SCALE
    s = jnp.where(kpos <= qpos, s, NEG_INF)
    s = jnp.where(kpos > qpos - WINDOW, s, NEG_INF)
    e = jnp.exp(s - jnp.max(s, axis=-1, keepdims=True))
    o_win = _mm(e, win_ref[pl.ds(start, wk), w:2 * w]) * (1.0 / jnp.sum(e, axis=-1, keepdims=True))
    o_ref[...] = _gated_sum(gates_ref[...], gx_ref[...], o_cmp, o_slc, o_win, bm, tq).astype(o_ref.dtype)


def nsa_attn_prompt(qn, qr, gates, kc, vc, rows_new, win_new, n_seq, seq):
    tq = Q_TILE
    nb = kc.shape[1]
    n_sel = -(-seq // SLC_BLOCK)
    kt = min(KV_TILE, seq)
    wk = min(WINDOW + tq, seq)
    ov = jnp.asarray(_overlap_matrix(nb, seq))
    gx = jnp.asarray(_gate_expand(), MXU_DTYPE)
    per = seq // tq

    def qrows(width):
        return pl.BlockSpec((tq, width), lambda n, t: (n * per + t, 0))

    def per_seq(shape, lane_block=0):
        return pl.BlockSpec(shape, lambda n, t: (n,) + (0,) * (len(shape) - 2) + (lane_block,))

    return pl.pallas_call(
        functools.partial(_nsa_prompt_body, seq=seq, n_sel=n_sel, kt=kt, wk=wk),
        grid=(n_seq, per),
        in_specs=[qrows(NSA_Q_W), qrows(NSA_Q_W), qrows(LANES), per_seq((1, nb, NSA_KV_W)), per_seq((1, nb, NSA_KV_W)),
                  per_seq((seq, 2 * NSA_KV_W), 1), per_seq((seq, 2 * NSA_KV_W)),
                  _resident((nb, LANES)), _resident((LANES, 3 * NSA_Q_W))],
        out_specs=qrows(NSA_Q_W),
        out_shape=jax.ShapeDtypeStruct((n_seq * seq, NSA_Q_W), MXU_DTYPE),
        compiler_params=_params(2),
        name="nsa_attn_prompt",
    )(qn, qr, gates, kc, vc, rows_new, win_new, ov, gx)


def _nsa_sample_body(pt_ref, *refs, n_pages, past_len, ts, n_sel):
    del pt_ref
    page_refs = refs[:n_pages]
    (qn_ref, qr_ref, gates_ref, kc_ref, vc_ref, rnew_ref, wold_ref, wnew_ref, ov_ref, gx_ref, ex_ref,
     o_ref, wout_ref) = refs[n_pages:]
    tq = TS_PAD
    w = NSA_KV_W
    bm = _group_masks()
    o_cmp, sel, qpos = _cmp_and_select(_blockdiag(qn_ref[0].astype(F32), bm), kc_ref[0], vc_ref[0], ov_ref[...],
                                       past_len, tq, n_sel)
    qbd = _blockdiag(qr_ref[0].astype(F32), bm).astype(MXU_DTYPE)
    pad = jnp.zeros((PAGE_SIZE - tq, w), F32)

    bias = (jnp.dot(sel.astype(MXU_DTYPE), ex_ref[...], preferred_element_type=F32) - 1.0) * (-NEG_INF)
    k_new = jnp.concatenate([rnew_ref[0][:, 0:w], pad], axis=0)
    v_new = jnp.concatenate([rnew_ref[0][:, w:2 * w], pad], axis=0)
    s = jnp.concatenate([_mm_nt(qbd, r[0][:, 0:w]) for r in page_refs] + [_mm_nt(qbd, k_new)], axis=1)
    s = s * ATT_SCALE + _per_head_rows(bias, tq)
    kpos = lax.broadcasted_iota(jnp.int32, (1, (n_pages + 1) * PAGE_SIZE), 1)
    s = jnp.where(kpos <= qpos, s, NEG_INF)
    e = jnp.exp(s - jnp.max(s, axis=-1, keepdims=True))
    acc = _mm(e[:, n_pages * PAGE_SIZE:], v_new)
    for p, r in enumerate(page_refs):
        acc = acc + _mm(e[:, p * PAGE_SIZE:(p + 1) * PAGE_SIZE], r[0][:, w:2 * w])
    o_slc = acc * (1.0 / jnp.sum(e, axis=-1, keepdims=True))

    wlen = wold_ref.shape[1]
    kw_new = jnp.concatenate([wnew_ref[0][:, 0:w], pad], axis=0)
    vw_new = jnp.concatenate([wnew_ref[0][:, w:2 * w], pad], axis=0)
    s = jnp.concatenate([_mm_nt(qbd, wold_ref[0][:, 0:w]), _mm_nt(qbd, kw_new)], axis=1) * ATT_SCALE
    kpos = (past_len - wlen) + lax.broadcasted_iota(jnp.int32, (1, wlen + PAGE_SIZE), 1)
    s = jnp.where(kpos <= qpos, s, NEG_INF)
    s = jnp.where(kpos > qpos - WINDOW, s, NEG_INF)
    e = jnp.exp(s - jnp.max(s, axis=-1, keepdims=True))
    acc = _mm(e[:, 0:wlen], wold_ref[0][:, w:2 * w]) + _mm(e[:, wlen:], vw_new)
    o_win = acc * (1.0 / jnp.sum(e, axis=-1, keepdims=True))

    o_ref[0] = _gated_sum(gates_ref[0], gx_ref[...], o_cmp, o_slc, o_win, bm, tq)
    wout_ref[0, 0:wlen - ts, :] = wold_ref[0, ts:wlen, :]
    wout_ref[0, wlen - ts:wlen, :] = wnew_ref[0, 0:ts, :]


def nsa_attn_sample(table, pages, qn, qr, gates, kc, vc, rows_new, win_old, win_new, win_base, past_len, ts):
    n_seq, n_pages = table.shape
    nb = kc.shape[1]
    wlen = win_old.shape[1]
    length = past_len + ts
    n_sel = -(-length // SLC_BLOCK)
    n_keys = (n_pages + 1) * PAGE_SIZE
    ov = jnp.asarray(_overlap_matrix(nb, length))
    gx = jnp.asarray(_gate_expand(), MXU_DTYPE)
    ex = jnp.asarray((np.arange(n_keys)[None, :] // SLC_BLOCK == np.arange(LANES)[:, None]).astype(np.float32), MXU_DTYPE)

    def page_spec(p):
        return pl.BlockSpec((1, PAGE_SIZE, 2 * NSA_KV_W), lambda n, pt: (pt[n, p], 0, 1))

    def per_seq(shape, lane_block=0, base=0):
        return pl.BlockSpec(shape, lambda n, pt: (base + n,) + (0,) * (len(shape) - 2) + (lane_block,))

    def const(shape):
        return pl.BlockSpec(shape, lambda n, pt: (0,) * len(shape), pipeline_mode=pl.Buffered(1))

    grid_spec = pltpu.PrefetchScalarGridSpec(
        num_scalar_prefetch=1, grid=(n_seq,),
        in_specs=[page_spec(p) for p in range(n_pages)] + [
            per_seq((1, TS_PAD, NSA_Q_W)), per_seq((1, TS_PAD, NSA_Q_W)), per_seq((1, TS_PAD, LANES)),
            per_seq((1, nb, NSA_KV_W)), per_seq((1, nb, NSA_KV_W)), per_seq((1, TS_PAD, 2 * NSA_KV_W), 1),
            per_seq((1, wlen, 2 * NSA_KV_W), 0, win_base), per_seq((1, TS_PAD, 2 * NSA_KV_W)),
            const((nb, LANES)), const((LANES, 3 * NSA_Q_W)), const((LANES, n_keys))],
        out_specs=[per_seq((1, TS_PAD, NSA_Q_W)), per_seq((1, wlen, 2 * NSA_KV_W))])
    return pl.pallas_call(
        functools.partial(_nsa_sample_body, n_pages=n_pages, past_len=past_len, ts=ts, n_sel=n_sel),
        grid_spec=grid_spec,
        out_shape=[jax.ShapeDtypeStruct((n_seq, TS_PAD, NSA_Q_W), F32),
                   jax.ShapeDtypeStruct((n_seq, wlen, 2 * NSA_KV_W), F32)],
        compiler_params=_params(), name="nsa_attn_sample",
    )(table, *([pages] * n_pages), qn, qr, gates, kc, vc, rows_new, win_old, win_new, ov, gx, ex)


S5_NB = 8
S5_GB = 8
S5_HALF = S5_GB * S5_STATE
S5_TIME = 256


def _s5_disc_body(lr_ref, li_ref, ldt_ref, bre_ref, bim_ref, are_ref, aim_ref, bbre_ref, bbim_ref):
    dt = jnp.exp(ldt_ref[...])
    lr = jnp.minimum(lr_ref[...], -1e-4)
    li = li_ref[...]
    mag = jnp.exp(lr * dt)
    a_re = mag * jnp.cos(li * dt)
    a_im = mag * jnp.sin(li * dt)
    den = lr * lr + li * li
    z_re = ((a_re - 1.0) * lr + a_im * li) / den
    z_im = (a_im * lr - (a_re - 1.0) * li) / den
    are_ref[...] = a_re
    aim_ref[...] = a_im
    bbre_ref[...] = z_re * bre_ref[...] - z_im * bim_ref[...]
    bbim_ref[...] = z_re * bim_ref[...] + z_im * bre_ref[...]


def s5_discretize(lam_re, lam_im, log_dt, b_re, b_im):
    rows = S5_GROUPS * S5_GROUP_CH

    def per_channel(a):
        return jnp.broadcast_to(a[:, None, :], (S5_GROUPS, S5_GROUP_CH, S5_STATE)).reshape(rows, S5_STATE)

    args = (per_channel(lam_re), per_channel(lam_im), per_channel(jnp.broadcast_to(log_dt[:, None], lam_re.shape)),
            b_re.transpose(0, 2, 1).reshape(rows, S5_STATE), b_im.transpose(0, 2, 1).reshape(rows, S5_STATE))
    out = jax.ShapeDtypeStruct((rows, S5_STATE), F32)
    a_re, a_im, bb_re, bb_im = pl.pallas_call(_s5_disc_body, out_shape=[out] * 4, name="s5_discretize")(*args)
    shape = (S5_GROUPS, S5_GROUP_CH, S5_STATE)
    return a_re.reshape(shape)[:, 0], a_im.reshape(shape)[:, 0], bb_re.reshape(shape), bb_im.reshape(shape)


def _s5_scan_body(u_ref, bb_ref, cc_ref, a_ref, h0_ref, y_ref, hout_ref, bu_scr, hs_scr, h_scr, *, tc, use_h0):
    t = pl.program_id(2)

    @pl.when(t == 0)
    def _():
        h_scr[...] = h0_ref[0, 0] if use_h0 else jnp.zeros(h_scr.shape, F32)

    bu_scr[...] = _mm(u_ref[...], bb_ref[0])
    ar = a_ref[0][:, :S5_HALF]
    ai = a_ref[0][:, S5_HALF:]

    def step(i, carry):
        hr, hi = carry
        r = pl.multiple_of(i * S5_NB, S5_NB)
        nhr = ar * hr - ai * hi + bu_scr[pl.ds(r, S5_NB), :S5_HALF]
        nhi = ar * hi + ai * hr + bu_scr[pl.ds(r, S5_NB), S5_HALF:]
        hs_scr[pl.ds(r, S5_NB), :S5_HALF] = nhr
        hs_scr[pl.ds(r, S5_NB), S5_HALF:] = nhi
        return nhr, nhi

    hr, hi = lax.fori_loop(0, tc, step, (h_scr[:, :S5_HALF], h_scr[:, S5_HALF:]), unroll=min(tc, 8))
    h_scr[:, :S5_HALF] = hr
    h_scr[:, S5_HALF:] = hi
    y_ref[...] = _mm(hs_scr[...], cc_ref[0])

    @pl.when(t == pl.num_programs(2) - 1)
    def _():
        hout_ref[0, 0] = h_scr[...]


def s5_scan(u, h0, a_re, a_im, bb_re, bb_im, c_re, c_im):
    n_real, t_len, _ = u.shape
    n = -(-n_real // S5_NB) * S5_NB
    if n != n_real:
        u = jnp.pad(u, ((0, n - n_real), (0, 0), (0, 0)))
        h0 = None if h0 is None else jnp.pad(h0, ((0, n - n_real), (0, 0), (0, 0), (0, 0)))
    nb = n // S5_NB
    ngb = S5_GROUPS // S5_GB
    tc = min(S5_TIME, t_len)
    eye = jnp.eye(S5_GB, dtype=F32)

    def blockdiag_in(bb):
        return jnp.einsum('ab,xacp->xacbp', eye, bb.reshape(ngb, S5_GB, S5_GROUP_CH, S5_STATE)).reshape(
            ngb, S5_GB * S5_GROUP_CH, S5_HALF)

    def blockdiag_out(cc):
        return jnp.einsum('ab,xbcp->xapbc', eye, cc.reshape(ngb, S5_GB, S5_GROUP_CH, S5_STATE)).reshape(
            ngb, S5_HALF, S5_GB * S5_GROUP_CH)

    bb = jnp.concatenate([blockdiag_in(bb_re), blockdiag_in(bb_im)], axis=2).astype(MXU_DTYPE)
    cc = jnp.concatenate([blockdiag_out(c_re), blockdiag_out(-c_im)], axis=1).astype(MXU_DTYPE)
    a = jnp.concatenate([a_re.reshape(ngb, S5_HALF), a_im.reshape(ngb, S5_HALF)], axis=1)
    a = jnp.broadcast_to(a[:, None, :], (ngb, S5_NB, 2 * S5_HALF))
    use_h0 = h0 is not None
    if use_h0:
        h0b = h0.reshape(nb, S5_NB, 2, ngb, S5_HALF).transpose(0, 3, 1, 2, 4).reshape(nb, ngb, S5_NB, 2 * S5_HALF)
    else:
        h0b = jnp.zeros((1, 1, S5_NB, 2 * S5_HALF), F32)
    ub = u.reshape(nb, S5_NB, t_len, D_MODEL).transpose(0, 2, 1, 3).reshape(nb * t_len * S5_NB, D_MODEL)
    per = t_len // tc
    lanes_u = S5_GB * S5_GROUP_CH

    rows_spec = pl.BlockSpec((tc * S5_NB, lanes_u), lambda b, g, t: (b * per + t, g))
    state_spec = pl.BlockSpec((1, 1, S5_NB, 2 * S5_HALF), lambda b, g, t: (b, g, 0, 0))
    h0_spec = state_spec if use_h0 else pl.BlockSpec((1, 1, S5_NB, 2 * S5_HALF), lambda b, g, t: (0, 0, 0, 0))
    y, h_last = pl.pallas_call(
        functools.partial(_s5_scan_body, tc=tc, use_h0=use_h0),
        grid=(nb, ngb, per),
        in_specs=[rows_spec, pl.BlockSpec((1, lanes_u, 2 * S5_HALF), lambda b, g, t: (g, 0, 0)),
                  pl.BlockSpec((1, 2 * S5_HALF, lanes_u), lambda b, g, t: (g, 0, 0)),
                  pl.BlockSpec((1, S5_NB, 2 * S5_HALF), lambda b, g, t: (g, 0, 0)), h0_spec],
        out_specs=[rows_spec, state_spec],
        out_shape=[jax.ShapeDtypeStruct((nb * t_len * S5_NB, D_MODEL), F32),
                   jax.ShapeDtypeStruct((nb, ngb, S5_NB, 2 * S5_HALF), F32)],
        scratch_shapes=[pltpu.VMEM((tc * S5_NB, 2 * S5_HALF), F32), pltpu.VMEM((tc * S5_NB, 2 * S5_HALF), F32),
                        pltpu.VMEM((S5_NB, 2 * S5_HALF), F32)],
        compiler_params=pltpu.CompilerParams(dimension_semantics=("parallel", "parallel", "arbitrary"),
                                             vmem_limit_bytes=VMEM_LIMIT),
        name="s5_scan",
    )(ub, bb, cc, a, h0b)
    y = y.reshape(nb, t_len, S5_NB, D_MODEL).transpose(0, 2, 1, 3).reshape(n, t_len, D_MODEL)
    h_last = h_last.reshape(nb, ngb, S5_NB, 2, S5_GB, S5_STATE).transpose(0, 2, 3, 1, 4, 5).reshape(
        n, 2, S5_GROUPS, S5_STATE)
    return y[:n_real], h_last[:n_real]


def _s5_out_body(x_ref, y_ref, u_ref, d_ref, w_ref, o_ref):
    z = jax.nn.gelu(y_ref[...] + d_ref[...] * u_ref[...])
    ab = _mm(z, w_ref[...])
    o_ref[...] = x_ref[...] + ab[:, :D_MODEL] * jax.nn.sigmoid(ab[:, D_MODEL:])


def s5_out(x, y, u, d_skip, w_glu):
    rows = x.shape[0]
    tm = _row_tile(rows)
    return pl.pallas_call(
        _s5_out_body,
        grid=(rows // tm,),
        in_specs=[_rows(tm, D_MODEL), _rows(tm, D_MODEL), _rows(tm, D_MODEL), _resident((1, D_MODEL)),
                  _resident((D_MODEL, 2 * D_MODEL))],
        out_specs=_rows(tm, D_MODEL),
        out_shape=jax.ShapeDtypeStruct((rows, D_MODEL), F32),
        compiler_params=_params(),
        name="s5_out",
    )(x, y, u, d_skip.reshape(1, -1), w_glu.astype(MXU_DTYPE))


HG_SUB = 16
HG_TIME = 256


def _mm_exact(l01, x):
    x1 = x.astype(MXU_DTYPE)
    r1 = x - x1.astype(F32)
    x2 = r1.astype(MXU_DTYPE)
    x3 = (r1 - x2.astype(F32)).astype(MXU_DTYPE)
    dot = functools.partial(jnp.dot, preferred_element_type=F32)
    return dot(l01, x1) + dot(l01, x2) + dot(l01, x3)


def _hgrn_body(q_ref, fz_ref, v_ref, g_ref, lb_ref, og_ref, s0_ref, o_ref, sout_ref, s_scr,
               *, tb, chunk, sub, valid, use_s0):
    tblk = pl.program_id(2)

    @pl.when(tblk == 0)
    def _():
        s_scr[...] = s0_ref[0, 0] if use_s0 else jnp.zeros(s_scr.shape, F32)

    lb = lb_ref[...]
    og = og_ref[...]
    eye = (lax.broadcasted_iota(jnp.int32, (HG_DK, HG_DK), 0) == lax.broadcasted_iota(jnp.int32, (HG_DK, HG_DK), 1))
    tril = jnp.where(lax.broadcasted_iota(jnp.int32, (chunk, chunk), 0)
                     >= lax.broadcasted_iota(jnp.int32, (chunk, chunk), 1), 1.0, 0.0).astype(MXU_DTYPE)
    trow = lax.broadcasted_iota(jnp.int32, (sub, 1), 0)
    nsub = chunk // sub

    for ci in range(tb // chunk):
        r0 = ci * chunk
        q = q_ref[r0:r0 + chunk, :]
        v = v_ref[r0:r0 + chunk, :]
        f = lb + (1.0 - lb) * jax.nn.sigmoid(fz_ref[r0:r0 + chunk, :])
        k = 1.0 - f
        lf = jnp.log(f)
        if valid < tb:
            live = (r0 + lax.broadcasted_iota(jnp.int32, (chunk, 1), 0)) < valid
            k = jnp.where(live, k, 0.0)
            lf = jnp.where(live, lf, 0.0)
        gcum = _mm_exact(tril, lf)
        state = s_scr[...]
        o_inter = _mm(q * jnp.exp(gcum), state)
        o_blocks = [o_inter[i * sub:(i + 1) * sub] for i in range(nsub)]

        for j in range(nsub - 1):
            lo, hi = j * sub, (j + 1) * sub
            g_ref_row = gcum[hi - 1:hi, :]
            k_t = k[lo:hi] * jnp.exp(g_ref_row - gcum[lo:hi])
            q_t = q[hi:] * jnp.exp(gcum[hi:] - g_ref_row)
            contrib = _mm(_mm_nt(q_t, k_t), v[lo:hi])
            for i in range(j + 1, nsub):
                o_blocks[i] = o_blocks[i] + contrib[(i - j - 1) * sub:(i - j) * sub]

        for i in range(nsub):
            lo, hi = i * sub, (i + 1) * sub
            q_i, k_i, v_i, g_i = q[lo:hi], k[lo:hi], v[lo:hi], gcum[lo:hi]
            acc = jnp.zeros((sub, HG_DV), F32)
            for s in range(sub):
                decay = jnp.exp(jnp.minimum(g_i - g_i[s:s + 1], 0.0))
                wgt = jnp.sum(q_i * k_i[s:s + 1] * decay, axis=1, keepdims=True)
                acc = acc + jnp.where(trow >= s, wgt, 0.0) * v_i[s:s + 1]
            o_blocks[i] = o_blocks[i] + acc

        g_last = gcum[chunk - 1:chunk, :]
        k_t = k * jnp.exp(g_last - gcum)
        decay_col = jnp.sum(jnp.where(eye, jnp.exp(g_last), 0.0), axis=1, keepdims=True)
        kv = lax.dot_general(k_t.astype(MXU_DTYPE), v.astype(MXU_DTYPE), (((0,), (0,)), ((), ())),
                             preferred_element_type=F32)
        s_scr[...] = decay_col * state + kv

        o = jnp.concatenate(o_blocks, axis=0)
        o = o * lax.rsqrt(jnp.mean(o * o, axis=-1, keepdims=True) + RMS_EPS) * og
        gate = g_ref[r0:r0 + chunk, :]
        o_ref[r0:r0 + chunk, :] = o * (gate * jax.nn.sigmoid(gate))

    @pl.when(tblk == pl.num_programs(2) - 1)
    def _():
        sout_ref[0, 0] = s_scr[...]


def hgrn_scan(pr, s0, o_gain, lb, n_seq, t_rows, valid):
    tb = min(HG_TIME, t_rows)
    chunk = min(HG_CHUNK, tb)
    sub = min(HG_SUB, chunk)
    per = t_rows // tb
    use_s0 = s0 is not None
    if not use_s0:
        s0 = jnp.zeros((1, 1, HG_DK, HG_DV), F32)

    def part(idx):
        return pl.BlockSpec((tb, HG_DK), lambda n, h, t: (n * per + t, idx * HG_HEADS + h))

    head_vec = pl.BlockSpec((1, HG_DK), lambda n, h, t: (0, h))
    state_spec = pl.BlockSpec((1, 1, HG_DK, HG_DV), lambda n, h, t: (n, h, 0, 0))
    s0_spec = state_spec if use_s0 else pl.BlockSpec((1, 1, HG_DK, HG_DV), lambda n, h, t: (0, 0, 0, 0))
    return pl.pallas_call(
        functools.partial(_hgrn_body, tb=tb, chunk=chunk, sub=sub, valid=valid, use_s0=use_s0),
        grid=(n_seq, HG_HEADS, per),
        in_specs=[part(0), part(1), part(2), part(3), head_vec, pl.BlockSpec((1, HG_DV), lambda n, h, t: (0, 0)), s0_spec],
        out_specs=[pl.BlockSpec((tb, HG_DV), lambda n, h, t: (n * per + t, h)), state_spec],
        out_shape=[jax.ShapeDtypeStruct((n_seq * t_rows, D_MODEL), F32),
                   jax.ShapeDtypeStruct((n_seq, HG_HEADS, HG_DK, HG_DV), F32)],
        scratch_shapes=[pltpu.VMEM((HG_DK, HG_DV), F32)],
        compiler_params=pltpu.CompilerParams(dimension_semantics=("parallel", "parallel", "arbitrary"),
                                             vmem_limit_bytes=VMEM_LIMIT),
        name="hgrn_scan",
    )(pr, pr, pr, pr, lb.reshape(1, -1), o_gain.reshape(1, -1), s0)


def kernel(x_prompt, x_sample, cache_nsa, state_nsa_win, state_s5, state_hgrn, page_table, p_prompt, p_sample, norm_gain, ffn_w_in, ffn_w_out, ple_w_gate, ple_w_proj, nsa_w_in, nsa_w_out, nsa_qk_gain, nsa_cmp_pe, nsa_cmp_w, s5_lam_re, s5_lam_im, s5_log_dt, s5_b_re, s5_b_im, s5_c_re, s5_c_im, s5_d, s5_w_glu, hg_w_in, hg_w_out, hg_o_gain, hg_lb_raw):
    B, T, _ = x_prompt.shape
    Bs, Ts, _ = x_sample.shape
    rp = B * T
    rs = Bs * Ts
    n_pages = page_table.shape[1]
    past_len = n_pages * PAGE_SIZE
    n_phys = cache_nsa.shape[1]
    wlen = state_nsa_win.shape[2]
    assert T % Q_TILE == 0 and T % PAGE_SIZE == 0 and T % min(KV_TILE, T) == 0 and Ts <= TS_PAD
    lb_sm = jax.nn.softmax(hg_lb_raw.astype(F32), axis=0)
    lower_bounds = jnp.cumsum(lb_sm, axis=0) - lb_sm[0]

    x = jnp.concatenate([x_prompt.reshape(rp, D_MODEL), x_sample.reshape(rs, D_MODEL)], axis=0)
    p_all = jnp.concatenate([p_prompt.reshape(DEPTH, rp, PLE_DIM), p_sample.reshape(DEPTH, rs, PLE_DIM)], axis=1)
    pos = jnp.concatenate([jnp.tile(jnp.arange(T), B), jnp.tile(past_len + jnp.arange(Ts), Bs)])
    cache_pages = cache_nsa.reshape(-1, PAGE_SIZE, 4 * NSA_KV_W)
    win_state = state_nsa_win.reshape(-1, wlen, 2 * NSA_KV_W)
    prompt_table = (jnp.arange(B, dtype=jnp.int32)[:, None] * (T // PAGE_SIZE)
                    + jnp.arange(T // PAGE_SIZE, dtype=jnp.int32)[None, :])
    out_perm = _head_perm()

    def split(a):
        return a[:rp].reshape(B, T, -1), a[rp:].reshape(Bs, Ts, -1)

    def join(a, b):
        return jnp.concatenate([a.reshape(rp, -1), b.reshape(rs, -1)], axis=0)

    def sample_pad(a):
        return jnp.pad(a[rp:].reshape(Bs, Ts, -1), ((0, 0), (0, TS_PAD - Ts), (0, 0)))

    outs_p = {0: [], 1: [], 2: [], 3: []}
    outs_s = {0: [], 1: [], 2: [], 3: []}
    for i in range(DEPTH):
        kind = LAYER_KIND[i]
        j = LAYER_SLOT[i]
        g = norm_gain[i]
        x, xn = ffn_step(x, g[0], g[1], ffn_w_in[i, 0], ffn_w_out[i, 0])
        if kind == 0:
            qn, qr, rows_new, win_new, gates = nsa_proj(xn, nsa_w_in[j], nsa_qk_gain[j], pos)
            cmp_args = (nsa_cmp_pe[j], nsa_cmp_w[j], nsa_qk_gain[j, 1])
            kc_p, vc_p = nsa_compress(rows_new.reshape(-1, PAGE_SIZE, 4 * NSA_KV_W), prompt_table, *cmp_args)
            o_p = nsa_attn_prompt(qn, qr, gates, kc_p, vc_p, rows_new, win_new, B, T)
            table = page_table.astype(jnp.int32) + j * n_phys
            kc_s, vc_s = nsa_compress(cache_pages, table, *cmp_args)
            o_s, w_s = nsa_attn_sample(table, cache_pages, sample_pad(qn), sample_pad(qr), sample_pad(gates), kc_s, vc_s,
                                       sample_pad(rows_new), win_state, sample_pad(win_new), j * Bs, past_len, Ts)
            r_p, r_s = split(rows_new)
            outs_p[0].append(r_p.reshape(B, T, 4, NSA_KV_HEADS, HEAD_DIM))
            outs_s[0].append(r_s.reshape(Bs, Ts, 4, NSA_KV_HEADS, HEAD_DIM))
            buf = min(WINDOW, T)
            outs_p[1].append(win_new[:rp].reshape(B, T, 2, NSA_KV_HEADS, HEAD_DIM)[:, T - buf:])
            outs_s[1].append(w_s.reshape(Bs, wlen, 2, NSA_KV_HEADS, HEAD_DIM))
            o_all = jnp.concatenate([o_p, o_s[:, :Ts].reshape(rs, NSA_Q_W).astype(o_p.dtype)], axis=0)
            x = resid_proj(x, o_all, nsa_w_out[j][out_perm])
        elif kind == 1:
            u_p, u_s = split(xn)
            disc = s5_discretize(s5_lam_re[j], s5_lam_im[j], s5_log_dt[j], s5_b_re[j], s5_b_im[j])
            y_p, h_p = s5_scan(u_p, None, *disc, s5_c_re[j], s5_c_im[j])
            y_s, h_s = s5_scan(u_s, state_s5[j], *disc, s5_c_re[j], s5_c_im[j])
            outs_p[2].append(h_p); outs_s[2].append(h_s)
            x = s5_out(x, join(y_p, y_s), xn, s5_d[j], s5_w_glu[j])
        else:
            pr = proj(xn, hg_w_in[j])
            pr_s = jnp.pad(pr[rp:].reshape(Bs, Ts, -1), ((0, 0), (0, TS_PAD - Ts), (0, 0))).reshape(Bs * TS_PAD, -1)
            o_p, s_p = hgrn_scan(pr, None, hg_o_gain[j], lower_bounds[i], B, T, T)
            o_s, s_s = hgrn_scan(pr_s, state_hgrn[j], hg_o_gain[j], lower_bounds[i], Bs, TS_PAD, Ts)
            outs_p[3].append(s_p); outs_s[3].append(s_s)
            o_s = o_s.reshape(Bs, TS_PAD, D_MODEL)[:, :Ts].reshape(rs, D_MODEL)
            x = resid_proj(x, jnp.concatenate([o_p, o_s], axis=0), hg_w_out[j])
        x, xn = ffn_step(x, g[2], g[3], ffn_w_in[i, 1], ffn_w_out[i, 1])
        x = ple_step(x, xn, p_all[i], ple_w_gate[i], ple_w_proj[i])

    y_p, y_s = split(x)
    return (y_p, y_s,
            jnp.stack(outs_p[0]), jnp.stack(outs_p[1]), jnp.stack(outs_p[2]), jnp.stack(outs_p[3]),
            jnp.stack(outs_s[0]), jnp.stack(outs_s[1]), jnp.stack(outs_s[2]), jnp.stack(outs_s[3]))
```

```python
import functools
import math

import numpy as np
import jax
import jax.numpy as jnp
from jax import lax
from jax.experimental import pallas as pl
from jax.experimental.pallas import tpu as pltpu

F32 = jnp.float32
MXU_DTYPE = jnp.bfloat16

D_MODEL = 1024
DEPTH = 4
PAGE_SIZE = 128
D_FF = 2816
PLE_DIM = 256
RMS_EPS = 1e-6
LAYER_KIND = (0, 1, 2, 0)
LAYER_SLOT = (0, 0, 0, 1)

NSA_HEADS = 16
NSA_KV_HEADS = 4
HEAD_DIM = 64
NSA_HPG = NSA_HEADS // NSA_KV_HEADS
NSA_KV_W = NSA_KV_HEADS * HEAD_DIM
NSA_Q_W = NSA_HEADS * HEAD_DIM
NSA_IN = NSA_Q_W + 6 * NSA_KV_W + 3 * NSA_HEADS
CMP_BLOCK = 32
CMP_STRIDE = 16
SLC_BLOCK = 64
SLC_TOPN = 16
WINDOW = 512
FORCE_BONUS = 1e4
NEG_INF = -1e30
ROPE_THETA = 500000.0
ROPE_DIMS = HEAD_DIM // 4
ATT_SCALE = HEAD_DIM ** -0.5

S5_GROUP_CH = 16
S5_GROUPS = D_MODEL // S5_GROUP_CH
S5_STATE = 64

HG_DK = 128
HG_HEADS = D_MODEL // HG_DK
HG_DV = D_MODEL // HG_HEADS
HG_CHUNK = 64

V7X_VMEM_BYTES = 64 * 1024 * 1024
VMEM_LIMIT = V7X_VMEM_BYTES - 8 * 1024 * 1024
LANES = 128
ROW_TILE = 512
FF_CHUNK = 256
Q_TILE = 128
KV_TILE = 512
TS_PAD = 8
SOFTMAX_M0 = -1e29


def _resident(shape):
    return pl.BlockSpec(shape, lambda *_: (0,) * len(shape), pipeline_mode=pl.Buffered(1))


def _row_tile(rows):
    return max(t for t in range(8, ROW_TILE + 1, 8) if rows % t == 0)


def _rows(tm, width):
    return pl.BlockSpec((tm, width), lambda i: (i, 0))


def _params(n_axes=1):
    return pltpu.CompilerParams(dimension_semantics=("parallel",) * n_axes, vmem_limit_bytes=VMEM_LIMIT)


def _rms(x, g):
    return x * lax.rsqrt(jnp.mean(x * x, axis=-1, keepdims=True) + RMS_EPS) * g


def _mm(a, b):
    return jnp.dot(a.astype(MXU_DTYPE), b.astype(MXU_DTYPE), preferred_element_type=F32)


def _mm_nt(a, b):
    return lax.dot_general(a.astype(MXU_DTYPE), b.astype(MXU_DTYPE), (((1,), (1,)), ((), ())),
                           preferred_element_type=F32)


def _mm_split(a, b):
    hi = a.astype(MXU_DTYPE)
    lo = (a - hi.astype(F32)).astype(MXU_DTYPE)
    return (jnp.dot(hi, b, preferred_element_type=F32) + jnp.dot(lo, b, preferred_element_type=F32))


def _ffn_body(x_ref, g_ref, gn_ref, win_ref, wout_ref, o_ref, on_ref):
    x = x_ref[...]
    xb = _rms(x, g_ref[...]).astype(MXU_DTYPE)
    acc = jnp.zeros(x.shape, F32)
    for c in range(D_FF // FF_CHUNK):
        lo = c * FF_CHUNK
        a = _mm(xb, win_ref[:, lo:lo + FF_CHUNK])
        b = _mm(xb, win_ref[:, D_FF + lo:D_FF + lo + FF_CHUNK])
        h = a * jax.nn.sigmoid(a) * b
        acc = acc + _mm(h, wout_ref[lo:lo + FF_CHUNK, :])
    y = x + 0.5 * acc
    o_ref[...] = y
    on_ref[...] = _rms(y, gn_ref[...])


def ffn_step(x, g, g_next, w_in, w_out):
    rows = x.shape[0]
    tm = _row_tile(rows)
    out = jax.ShapeDtypeStruct((rows, D_MODEL), F32)
    return pl.pallas_call(
        _ffn_body,
        grid=(rows // tm,),
        in_specs=[_rows(tm, D_MODEL), _resident((1, D_MODEL)), _resident((1, D_MODEL)),
                  _resident((D_MODEL, 2 * D_FF)), _resident((D_FF, D_MODEL))],
        out_specs=[_rows(tm, D_MODEL), _rows(tm, D_MODEL)],
        out_shape=[out, out],
        compiler_params=_params(),
        name="ffn_step",
    )(x, g.reshape(1, -1), g_next.reshape(1, -1), w_in.astype(MXU_DTYPE), w_out.astype(MXU_DTYPE))


def _ple_body(x_ref, xn_ref, p_ref, wg_ref, wp_ref, o_ref):
    gate = jax.nn.sigmoid(_mm(xn_ref[...], wg_ref[...]))
    o_ref[...] = x_ref[...] + gate * _mm(p_ref[...], wp_ref[...])


def ple_step(x, xn, p, w_gate, w_proj):
    rows = x.shape[0]
    tm = _row_tile(rows)
    return pl.pallas_call(
        _ple_body,
        grid=(rows // tm,),
        in_specs=[_rows(tm, D_MODEL), _rows(tm, D_MODEL), _rows(tm, PLE_DIM),
                  _resident((D_MODEL, D_MODEL)), _resident((PLE_DIM, D_MODEL))],
        out_specs=_rows(tm, D_MODEL),
        out_shape=jax.ShapeDtypeStruct((rows, D_MODEL), F32),
        compiler_params=_params(),
        name="ple_step",
    )(x, xn, p, w_gate.astype(MXU_DTYPE), w_proj.astype(MXU_DTYPE))


def _proj_body(a_ref, w_ref, o_ref):
    o_ref[...] = _mm(a_ref[...], w_ref[...])


def proj(a, w):
    rows, k = a.shape
    n = w.shape[1]
    tm = _row_tile(rows)
    return pl.pallas_call(
        _proj_body,
        grid=(rows // tm,),
        in_specs=[_rows(tm, k), _resident((k, n))],
        out_specs=_rows(tm, n),
        out_shape=jax.ShapeDtypeStruct((rows, n), F32),
        compiler_params=_params(),
        name="proj",
    )(a, w.astype(MXU_DTYPE))


def _resid_body(x_ref, a_ref, w_ref, o_ref):
    o_ref[...] = x_ref[...] + _mm(a_ref[...], w_ref[...])


def resid_proj(x, a, w):
    rows, k = a.shape
    tm = _row_tile(rows)
    return pl.pallas_call(
        _resid_body,
        grid=(rows // tm,),
        in_specs=[_rows(tm, D_MODEL), _rows(tm, k), _resident((k, D_MODEL))],
        out_specs=_rows(tm, D_MODEL),
        out_shape=jax.ShapeDtypeStruct((rows, D_MODEL), F32),
        compiler_params=_params(),
        name="resid_proj",
    )(x, a, w.astype(MXU_DTYPE))


def _head_perm():
    idx = np.arange(NSA_Q_W).reshape(NSA_KV_HEADS, NSA_HPG, HEAD_DIM)
    return idx.transpose(1, 0, 2).reshape(-1)


def _gate_expand():
    x = np.zeros((LANES, 3 * NSA_Q_W), np.float32)
    for b in range(3):
        for g in range(NSA_KV_HEADS):
            for j in range(NSA_HPG):
                h = g * NSA_HPG + j
                c0 = b * NSA_Q_W + (j * NSA_KV_HEADS + g) * HEAD_DIM
                x[b * NSA_HEADS + h, c0:c0 + HEAD_DIM] = 1.0
    return x


def _overlap_matrix(nb, length):
    n_sel = -(-length // SLC_BLOCK)
    c0 = np.arange(nb)[:, None] * CMP_STRIDE
    s0 = np.arange(LANES)[None, :] * SLC_BLOCK
    ov = np.clip(np.minimum(c0 + CMP_BLOCK, s0 + SLC_BLOCK) - np.maximum(c0, s0), 0, None) / CMP_BLOCK
    ov = np.where(np.arange(LANES)[None, :] < n_sel, ov, 0.0)
    return ov.astype(np.float32)


def _rope_tables(pos):
    half = ROPE_DIMS // 2
    inv = ROPE_THETA ** (-jnp.arange(half, dtype=F32) / half)
    ang = pos.astype(F32)[:, None] * inv[None, :]
    cos, sin = jnp.cos(ang), jnp.sin(ang)
    ones = jnp.ones((pos.shape[0], HEAD_DIM - ROPE_DIMS), F32)
    zeros = jnp.zeros((pos.shape[0], HEAD_DIM - ROPE_DIMS), F32)
    zh = jnp.zeros_like(sin)
    c = jnp.concatenate([cos, cos, ones], axis=1)
    sa = jnp.concatenate([-sin, zh, zeros], axis=1)
    sb = jnp.concatenate([zh, sin, zeros], axis=1)
    rep = LANES // HEAD_DIM
    return jnp.tile(c, (1, rep)), jnp.tile(sa, (1, rep)), jnp.tile(sb, (1, rep))


def _nsa_proj_body(xn_ref, wq_ref, wkv_ref, wgl_ref, b64_ref, gq_ref, gk_ref, c_ref, sa_ref, sb_ref,
                   qn_ref, qr_ref, rows_ref, win_ref, gates_ref):
    xb = xn_ref[...].astype(MXU_DTYPE)
    c, sa, sb = c_ref[...], sa_ref[...], sb_ref[...]

    def head_norm(v, gain):
        w = v.shape[1]
        ms = _mm_split(v * v, b64_ref[:w, :w])
        return v * lax.rsqrt(ms + RMS_EPS) * gain

    def rope(v):
        w = v.shape[1]
        rep = w // LANES
        ct, sat, sbt = (jnp.concatenate([t] * rep, axis=1) for t in (c, sa, sb))
        return v * ct + pltpu.roll(v, w - ROPE_DIMS // 2, 1) * sat + pltpu.roll(v, ROPE_DIMS // 2, 1) * sbt

    qn = head_norm(_mm(xb, wq_ref[...]), gq_ref[...])
    qn_ref[...] = qn.astype(qn_ref.dtype)
    qr_ref[...] = rope(qn).astype(qr_ref.dtype)
    kv = _mm(xb, wkv_ref[...])
    w = NSA_KV_W
    rows_ref[:, 0:2 * w] = kv[:, 0:2 * w]
    rows_ref[:, 2 * w:3 * w] = rope(head_norm(kv[:, 2 * w:3 * w], gk_ref[0:1, :]))
    rows_ref[:, 3 * w:4 * w] = kv[:, 3 * w:4 * w]
    win_ref[:, 0:w] = rope(head_norm(kv[:, 4 * w:5 * w], gk_ref[1:2, :]))
    win_ref[:, w:2 * w] = kv[:, 5 * w:6 * w]
    gates_ref[...] = jax.nn.sigmoid(_mm(xb, wgl_ref[...]))


def nsa_proj(xn, w_in, qk_gain, pos):
    rows = xn.shape[0]
    tm = _row_tile(rows)
    kvw = 6 * NSA_KV_W
    wq = w_in[:, :NSA_Q_W][:, _head_perm()].astype(MXU_DTYPE)
    wkv = w_in[:, NSA_Q_W:NSA_Q_W + kvw].astype(MXU_DTYPE)
    wgl = jnp.pad(w_in[:, NSA_Q_W + kvw:], ((0, 0), (0, LANES - 3 * NSA_HEADS))).astype(MXU_DTYPE)
    b64 = jnp.asarray(np.kron(np.eye(NSA_HEADS), np.full((HEAD_DIM, HEAD_DIM), 1.0 / HEAD_DIM)), MXU_DTYPE)
    gq = jnp.tile(qk_gain[0], NSA_HEADS).reshape(1, -1)
    gk = jnp.stack([jnp.tile(qk_gain[2], NSA_KV_HEADS), jnp.tile(qk_gain[3], NSA_KV_HEADS)])
    c, sa, sb = _rope_tables(pos)
    return pl.pallas_call(
        _nsa_proj_body,
        grid=(rows // tm,),
        in_specs=[_rows(tm, D_MODEL), _resident((D_MODEL, NSA_Q_W)), _resident((D_MODEL, kvw)),
                  _resident((D_MODEL, LANES)), _resident((NSA_Q_W, NSA_Q_W)), _resident((1, NSA_Q_W)),
                  _resident((2, NSA_KV_W)), _rows(tm, LANES), _rows(tm, LANES), _rows(tm, LANES)],
        out_specs=[_rows(tm, NSA_Q_W), _rows(tm, NSA_Q_W), _rows(tm, 4 * NSA_KV_W), _rows(tm, 2 * NSA_KV_W),
                   _rows(tm, LANES)],
        out_shape=[jax.ShapeDtypeStruct((rows, NSA_Q_W), MXU_DTYPE), jax.ShapeDtypeStruct((rows, NSA_Q_W), MXU_DTYPE),
                   jax.ShapeDtypeStruct((rows, 4 * NSA_KV_W), F32), jax.ShapeDtypeStruct((rows, 2 * NSA_KV_W), F32),
                   jax.ShapeDtypeStruct((rows, LANES), F32)],
        compiler_params=_params(),
        name="nsa_proj",
    )(xn, wq, wkv, wgl, b64, gq, gk, c, sa, sb)


def _nsa_compress_body(pt_ref, *refs, n_pages, transposed):
    del pt_ref
    page_refs = refs[:n_pages]
    wk_ref, wv_ref, pek_ref, pev_ref, b64_ref, gk_ref, kc_ref, vc_ref, stage = refs[n_pages:]
    per_page = PAGE_SIZE // CMP_STRIDE
    nb = n_pages * per_page
    tiles = 2 * NSA_KV_W // LANES

    for p, r in enumerate(page_refs):
        for j in range(tiles):
            if transposed:
                stage[j, p * PAGE_SIZE:(p + 1) * PAGE_SIZE, :] = r[0, j * LANES:(j + 1) * LANES, :].T
            else:
                stage[j, p * PAGE_SIZE:(p + 1) * PAGE_SIZE, :] = r[0, :, j * LANES:(j + 1) * LANES]

    def compress(kind, w_ref, pe_ref):
        per_kind = NSA_KV_W // LANES
        first = jnp.zeros((nb, NSA_KV_W), F32)
        second = jnp.zeros((nb, NSA_KV_W), F32)
        for l in range(CMP_STRIDE):
            x = jnp.concatenate([stage[kind * per_kind + j, pl.ds(l, nb, stride=CMP_STRIDE), :]
                                 for j in range(per_kind)], axis=1)
            first = first + _mm(x + pe_ref[l:l + 1, :], w_ref[l])
            second = second + _mm(x + pe_ref[CMP_STRIDE + l:CMP_STRIDE + l + 1, :], w_ref[CMP_STRIDE + l])
        return first + pltpu.roll(second, nb - 1, 0)

    kc = compress(0, wk_ref, pek_ref)
    ms = _mm_split(kc * kc, b64_ref[...])
    kc_ref[0] = kc * lax.rsqrt(ms + RMS_EPS) * gk_ref[...]
    vc_ref[0] = compress(1, wv_ref, pev_ref)


def nsa_compress(pages, table, cmp_pe, cmp_w, k_gain, transposed):
    n_seq, n_pages = table.shape
    nb = n_pages * (PAGE_SIZE // CMP_STRIDE)
    eye = jnp.eye(NSA_KV_HEADS, dtype=F32)
    w4 = jnp.einsum('gh,klde->klgdhe', eye, cmp_w).reshape(2, CMP_BLOCK, NSA_KV_W, NSA_KV_W).astype(MXU_DTYPE)
    pe4 = jnp.tile(cmp_pe, (1, 1, NSA_KV_HEADS))
    b64 = jnp.asarray(np.kron(np.eye(NSA_KV_HEADS), np.full((HEAD_DIM, HEAD_DIM), 1.0 / HEAD_DIM)), MXU_DTYPE)
    gk = jnp.tile(k_gain, NSA_KV_HEADS).reshape(1, -1)

    page_block = (1, 2 * NSA_KV_W, PAGE_SIZE) if transposed else (1, PAGE_SIZE, 2 * NSA_KV_W)

    def page_spec(p):
        return pl.BlockSpec(page_block, lambda n, pt: (pt[n, p], 0, 0))

    def const(shape):
        return pl.BlockSpec(shape, lambda n, pt: (0,) * len(shape), pipeline_mode=pl.Buffered(1))

    out = jax.ShapeDtypeStruct((n_seq, nb, NSA_KV_W), F32)
    out_spec = pl.BlockSpec((1, nb, NSA_KV_W), lambda n, pt: (n, 0, 0))
    grid_spec = pltpu.PrefetchScalarGridSpec(
        num_scalar_prefetch=1, grid=(n_seq,),
        in_specs=[page_spec(p) for p in range(n_pages)] + [
            const((CMP_BLOCK, NSA_KV_W, NSA_KV_W)), const((CMP_BLOCK, NSA_KV_W, NSA_KV_W)),
            const((CMP_BLOCK, NSA_KV_W)), const((CMP_BLOCK, NSA_KV_W)), const((NSA_KV_W, NSA_KV_W)),
            const((1, NSA_KV_W))],
        out_specs=[out_spec, out_spec],
        scratch_shapes=[pltpu.VMEM((2 * NSA_KV_W // LANES, n_pages * PAGE_SIZE, LANES), F32)])
    return pl.pallas_call(
        functools.partial(_nsa_compress_body, n_pages=n_pages, transposed=transposed),
        grid_spec=grid_spec, out_shape=[out, out], compiler_params=_params(), name="nsa_compress",
    )(table, *([pages] * n_pages), w4[0], w4[1], pe4[0], pe4[1], b64, gk)


def _group_masks():
    lane = lax.broadcasted_iota(jnp.int32, (1, NSA_KV_W), 1)
    return [(lane // HEAD_DIM) == g for g in range(NSA_KV_HEADS)]


def _blockdiag(q, bm):
    zero = jnp.zeros((), q.dtype)
    return jnp.concatenate([jnp.where(bm[g], q[:, NSA_KV_W * j:NSA_KV_W * (j + 1)], zero)
                            for g in range(NSA_KV_HEADS) for j in range(NSA_HPG)], axis=0)


def _extract(obd, bm, tq):
    outs = []
    for j in range(NSA_HPG):
        z = jnp.zeros((tq, NSA_KV_W), F32)
        for g in range(NSA_KV_HEADS):
            r0 = (g * NSA_HPG + j) * tq
            z = z + jnp.where(bm[g], obd[r0:r0 + tq], 0.0)
        outs.append(z)
    return jnp.concatenate(outs, axis=1)


def _per_head_rows(a, tq):
    k = a.shape[1]
    a4 = jnp.broadcast_to(a.reshape(NSA_KV_HEADS, 1, tq, k), (NSA_KV_HEADS, NSA_HPG, tq, k))
    return a4.reshape(NSA_HEADS * tq, k)


def _topk_mask(score, n_sel, tq):
    if tq == LANES:
        nsp = -(-n_sel // 8) * 8
        sub = lax.broadcasted_iota(jnp.int32, (nsp, 1), 0)
        outs = []
        for g in range(NSA_KV_HEADS):
            st = score[g * tq:(g + 1) * tq].T[:nsp]
            rank = jnp.zeros(st.shape, F32)
            for s2 in range(n_sel):
                row = st[s2:s2 + 1, :]
                rank = rank + jnp.where(row > st, 1.0, jnp.where(row == st, jnp.where(sub > s2, 1.0, 0.0), 0.0))
            sel_t = jnp.where(rank < SLC_TOPN, 1.0, 0.0)
            sel_t = jnp.concatenate([sel_t, jnp.zeros((LANES - nsp, tq), F32)], axis=0)
            outs.append(sel_t.T)
        return jnp.concatenate(outs, axis=0)
    lane = lax.broadcasted_iota(jnp.int32, (1, LANES), 1)
    rank = jnp.zeros(score.shape, F32)
    for s2 in range(n_sel):
        col = score[:, s2:s2 + 1]
        rank = rank + jnp.where(col > score, 1.0, jnp.where(col == score, jnp.where(lane > s2, 1.0, 0.0), 0.0))
    return jnp.where(rank < SLC_TOPN, 1.0, 0.0)


def _cmp_and_select(qbd_n, kc, vc, ov, pos_base, tq, n_sel):
    nb = kc.shape[0]
    r = lax.broadcasted_iota(jnp.int32, (NSA_HEADS * tq, 1), 0)
    qpos = pos_base + (r & (tq - 1))
    blk_end = lax.broadcasted_iota(jnp.int32, (1, nb), 1) * CMP_STRIDE + (CMP_BLOCK - 1)
    visible = blk_end <= qpos
    s = jnp.where(visible, _mm_nt(qbd_n, kc) * ATT_SCALE, NEG_INF)
    e = jnp.exp(s - jnp.max(s, axis=-1, keepdims=True))
    p = jnp.where(visible, e / jnp.sum(e, axis=-1, keepdims=True), 0.0)
    o = _mm(p, vc)
    psum = jnp.sum(p.reshape(NSA_KV_HEADS, NSA_HPG, tq, nb), axis=1).reshape(NSA_KV_HEADS * tq, nb)
    imp = _mm_split(psum, ov.astype(MXU_DTYPE))
    r4 = lax.broadcasted_iota(jnp.int32, (NSA_KV_HEADS * tq, 1), 0)
    qblk = (pos_base + (r4 & (tq - 1))) // SLC_BLOCK
    sidx = lax.broadcasted_iota(jnp.int32, (1, LANES), 1)
    bonus = jnp.where(sidx == 0, FORCE_BONUS, jnp.where(sidx == qblk, FORCE_BONUS,
                                                        jnp.where(sidx == qblk - 1, FORCE_BONUS, 0.0)))
    score = jnp.where(sidx <= qblk, imp + bonus, NEG_INF)
    return o, _topk_mask(score, n_sel, tq), qpos


def _gated_sum(gates, gx, o_cmp, o_slc, o_win, bm, tq):
    gf = _mm_split(gates, gx)
    return (gf[:, 0:NSA_Q_W] * _extract(o_cmp, bm, tq) + gf[:, NSA_Q_W:2 * NSA_Q_W] * _extract(o_slc, bm, tq)
            + gf[:, 2 * NSA_Q_W:] * _extract(o_win, bm, tq))


def _nsa_prompt_body(qn_ref, qr_ref, gates_ref, kc_ref, vc_ref, kv_ref, win_ref, ov_ref, gx_ref, o_ref,
                     *, seq, n_sel, kt, wk):
    tq = Q_TILE
    t0 = pl.program_id(1) * tq
    bm = _group_masks()
    w = NSA_KV_W
    o_cmp, sel, qpos = _cmp_and_select(_blockdiag(qn_ref[...], bm), kc_ref[0], vc_ref[0], ov_ref[...], t0, tq, n_sel)
    qbd = _blockdiag(qr_ref[...], bm)
    sel_b = sel.astype(MXU_DTYPE)
    sidx = lax.broadcasted_iota(jnp.int32, (LANES, 1), 0)

    def kv_tile(jt, carry):
        m, l, acc = carry
        k0 = pl.multiple_of(jt * kt, kt)
        kpos = k0 + lax.broadcasted_iota(jnp.int32, (1, kt), 1)
        expand = jnp.where((kpos // SLC_BLOCK) == sidx, 1.0, 0.0).astype(MXU_DTYPE)
        bias = (jnp.dot(sel_b, expand, preferred_element_type=F32) - 1.0) * (-NEG_INF)
        s = _mm_nt(qbd, kv_ref[pl.ds(k0, kt), 0:w]) * ATT_SCALE + _per_head_rows(bias, tq)
        s = jnp.where(kpos <= qpos, s, NEG_INF)
        m_new = jnp.maximum(m, jnp.max(s, axis=-1, keepdims=True))
        alpha = jnp.exp(m - m_new)
        e = jnp.exp(s - m_new)
        l = alpha * l + jnp.sum(e, axis=-1, keepdims=True)
        acc = alpha * acc + _mm(e, kv_ref[pl.ds(k0, kt), w:2 * w])
        return m_new, l, acc

    rows = NSA_HEADS * tq
    n_tiles = (t0 + tq + kt - 1) // kt
    m, l, acc = lax.fori_loop(0, n_tiles, kv_tile, (jnp.full((rows, 1), SOFTMAX_M0, F32), jnp.zeros((rows, 1), F32),
                                                    jnp.zeros((rows, w), F32)))
    o_slc = acc * (1.0 / l)

    start = pl.multiple_of(jnp.maximum(t0 + tq - wk, 0), tq)
    kpos = start + lax.broadcasted_iota(jnp.int32, (1, wk), 1)
    s = _mm_nt(qbd, win_ref[pl.ds(start, wk), 0:w]) * ATT_SCALE
    s = jnp.where(kpos <= qpos, s, NEG_INF)
    s = jnp.where(kpos > qpos - WINDOW, s, NEG_INF)
    e = jnp.exp(s - jnp.max(s, axis=-1, keepdims=True))
    o_win = _mm(e, win_ref[pl.ds(start, wk), w:2 * w]) * (1.0 / jnp.sum(e, axis=-1, keepdims=True))
    o_ref[...] = _gated_sum(gates_ref[...], gx_ref[...], o_cmp, o_slc, o_win, bm, tq).astype(o_ref.dtype)


def nsa_attn_prompt(qn, qr, gates, kc, vc, rows_new, win_new, n_seq, seq):
    tq = Q_TILE
    nb = kc.shape[1]
    n_sel = -(-seq // SLC_BLOCK)
    kt = min(KV_TILE, seq)
    wk = min(WINDOW + tq, seq)
    ov = jnp.asarray(_overlap_matrix(nb, seq))
    gx = jnp.asarray(_gate_expand(), MXU_DTYPE)
    per = seq // tq

    def qrows(width):
        return pl.BlockSpec((tq, width), lambda n, t: (n * per + t, 0))

    def per_seq(shape, lane_block=0):
        return pl.BlockSpec(shape, lambda n, t: (n,) + (0,) * (len(shape) - 2) + (lane_block,))

    return pl.pallas_call(
        functools.partial(_nsa_prompt_body, seq=seq, n_sel=n_sel, kt=kt, wk=wk),
        grid=(n_seq, per),
        in_specs=[qrows(NSA_Q_W), qrows(NSA_Q_W), qrows(LANES), per_seq((1, nb, NSA_KV_W)), per_seq((1, nb, NSA_KV_W)),
                  per_seq((seq, 2 * NSA_KV_W), 1), per_seq((seq, 2 * NSA_KV_W)),
                  _resident((nb, LANES)), _resident((LANES, 3 * NSA_Q_W))],
        out_specs=qrows(NSA_Q_W),
        out_shape=jax.ShapeDtypeStruct((n_seq * seq, NSA_Q_W), MXU_DTYPE),
        compiler_params=_params(2),
        name="nsa_attn_prompt",
    )(qn, qr, gates, kc, vc, rows_new, win_new, ov, gx)


def _nsa_sample_body(pt_ref, *refs, n_pages, past_len, ts, n_sel):
    del pt_ref
    page_refs = refs[:n_pages]
    (qn_ref, qr_ref, gates_ref, kc_ref, vc_ref, rnew_ref, wold_ref, wnew_ref, ov_ref, gx_ref, ex_ref,
     o_ref, wout_ref) = refs[n_pages:]
    tq = TS_PAD
    w = NSA_KV_W
    bm = _group_masks()
    o_cmp, sel, qpos = _cmp_and_select(_blockdiag(qn_ref[0].astype(F32), bm), kc_ref[0], vc_ref[0], ov_ref[...],
                                       past_len, tq, n_sel)
    qbd = _blockdiag(qr_ref[0].astype(F32), bm).astype(MXU_DTYPE)
    pad = jnp.zeros((PAGE_SIZE - tq, w), F32)

    bias = (jnp.dot(sel.astype(MXU_DTYPE), ex_ref[...], preferred_element_type=F32) - 1.0) * (-NEG_INF)
    k_new = jnp.concatenate([rnew_ref[0][:, 0:w], pad], axis=0)
    v_new = jnp.concatenate([rnew_ref[0][:, w:2 * w], pad], axis=0)
    s = jnp.concatenate([_mm(qbd, r[0, 0:w, :]) for r in page_refs] + [_mm_nt(qbd, k_new)], axis=1)
    s = s * ATT_SCALE + _per_head_rows(bias, tq)
    kpos = lax.broadcasted_iota(jnp.int32, (1, (n_pages + 1) * PAGE_SIZE), 1)
    s = jnp.where(kpos <= qpos, s, NEG_INF)
    e = jnp.exp(s - jnp.max(s, axis=-1, keepdims=True))
    acc = _mm(e[:, n_pages * PAGE_SIZE:], v_new)
    for p, r in enumerate(page_refs):
        acc = acc + _mm_nt(e[:, p * PAGE_SIZE:(p + 1) * PAGE_SIZE], r[0, w:2 * w, :])
    o_slc = acc * (1.0 / jnp.sum(e, axis=-1, keepdims=True))

    wlen = wold_ref.shape[2]
    kw_new = jnp.concatenate([wnew_ref[0][:, 0:w], pad], axis=0)
    vw_new = jnp.concatenate([wnew_ref[0][:, w:2 * w], pad], axis=0)
    s = jnp.concatenate([_mm(qbd, wold_ref[0, 0:w, :]), _mm_nt(qbd, kw_new)], axis=1) * ATT_SCALE
    kpos = (past_len - wlen) + lax.broadcasted_iota(jnp.int32, (1, wlen + PAGE_SIZE), 1)
    s = jnp.where(kpos <= qpos, s, NEG_INF)
    s = jnp.where(kpos > qpos - WINDOW, s, NEG_INF)
    e = jnp.exp(s - jnp.max(s, axis=-1, keepdims=True))
    acc = _mm_nt(e[:, 0:wlen], wold_ref[0, w:2 * w, :]) + _mm(e[:, wlen:], vw_new)
    o_win = acc * (1.0 / jnp.sum(e, axis=-1, keepdims=True))

    o_ref[0] = _gated_sum(gates_ref[0], gx_ref[...], o_cmp, o_slc, o_win, bm, tq)

    shifted = pltpu.roll(wold_ref[0], wlen - ts, 1)
    new_rows = jnp.concatenate([wnew_ref[0], jnp.zeros((LANES - tq, 2 * w), F32)], axis=0)
    new_t = jnp.concatenate([new_rows[:, j * LANES:(j + 1) * LANES].T for j in range(2 * w // LANES)], axis=0)
    new_t = pltpu.roll(new_t, LANES - ts, 1)
    lane = lax.broadcasted_iota(jnp.int32, (1, LANES), 1)
    wout_ref[0, :, 0:wlen - LANES] = shifted[:, 0:wlen - LANES]
    wout_ref[0, :, wlen - LANES:wlen] = jnp.where(lane >= LANES - ts, new_t, shifted[:, wlen - LANES:wlen])


def nsa_attn_sample(table, pages, qn, qr, gates, kc, vc, rows_new, win_old, win_new, win_base, past_len, ts):
    n_seq, n_pages = table.shape
    nb = kc.shape[1]
    wlen = win_old.shape[2]
    length = past_len + ts
    n_sel = -(-length // SLC_BLOCK)
    n_keys = (n_pages + 1) * PAGE_SIZE
    ov = jnp.asarray(_overlap_matrix(nb, length))
    gx = jnp.asarray(_gate_expand(), MXU_DTYPE)
    ex = jnp.asarray((np.arange(n_keys)[None, :] // SLC_BLOCK == np.arange(LANES)[:, None]).astype(np.float32), MXU_DTYPE)

    def page_spec(p):
        return pl.BlockSpec((1, 2 * NSA_KV_W, PAGE_SIZE), lambda n, pt: (pt[n, p], 1, 0))

    def per_seq(shape, lane_block=0, base=0):
        return pl.BlockSpec(shape, lambda n, pt: (base + n,) + (0,) * (len(shape) - 2) + (lane_block,))

    def const(shape):
        return pl.BlockSpec(shape, lambda n, pt: (0,) * len(shape), pipeline_mode=pl.Buffered(1))

    grid_spec = pltpu.PrefetchScalarGridSpec(
        num_scalar_prefetch=1, grid=(n_seq,),
        in_specs=[page_spec(p) for p in range(n_pages)] + [
            per_seq((1, TS_PAD, NSA_Q_W)), per_seq((1, TS_PAD, NSA_Q_W)), per_seq((1, TS_PAD, LANES)),
            per_seq((1, nb, NSA_KV_W)), per_seq((1, nb, NSA_KV_W)), per_seq((1, TS_PAD, 2 * NSA_KV_W), 1),
            per_seq((1, 2 * NSA_KV_W, wlen), 0, win_base), per_seq((1, TS_PAD, 2 * NSA_KV_W)),
            const((nb, LANES)), const((LANES, 3 * NSA_Q_W)), const((LANES, n_keys))],
        out_specs=[per_seq((1, TS_PAD, NSA_Q_W)), per_seq((1, 2 * NSA_KV_W, wlen))])
    return pl.pallas_call(
        functools.partial(_nsa_sample_body, n_pages=n_pages, past_len=past_len, ts=ts, n_sel=n_sel),
        grid_spec=grid_spec,
        out_shape=[jax.ShapeDtypeStruct((n_seq, TS_PAD, NSA_Q_W), F32),
                   jax.ShapeDtypeStruct((n_seq, 2 * NSA_KV_W, wlen), F32)],
        compiler_params=_params(), name="nsa_attn_sample",
    )(table, *([pages] * n_pages), qn, qr, gates, kc, vc, rows_new, win_old, win_new, ov, gx, ex)


S5_NB = 8
S5_GB = 8
S5_HALF = S5_GB * S5_STATE
S5_TIME = 256


def _s5_disc_body(lr_ref, li_ref, ldt_ref, bre_ref, bim_ref, are_ref, aim_ref, bbre_ref, bbim_ref):
    dt = jnp.exp(ldt_ref[...])
    lr = jnp.minimum(lr_ref[...], -1e-4)
    li = li_ref[...]
    mag = jnp.exp(lr * dt)
    a_re = mag * jnp.cos(li * dt)
    a_im = mag * jnp.sin(li * dt)
    den = lr * lr + li * li
    z_re = ((a_re - 1.0) * lr + a_im * li) / den
    z_im = (a_im * lr - (a_re - 1.0) * li) / den
    are_ref[...] = a_re
    aim_ref[...] = a_im
    bbre_ref[...] = z_re * bre_ref[...] - z_im * bim_ref[...]
    bbim_ref[...] = z_re * bim_ref[...] + z_im * bre_ref[...]


def s5_discretize(lam_re, lam_im, log_dt, b_re, b_im):
    rows = S5_GROUPS * S5_GROUP_CH

    def per_channel(a):
        return jnp.broadcast_to(a[:, None, :], (S5_GROUPS, S5_GROUP_CH, S5_STATE)).reshape(rows, S5_STATE)

    args = (per_channel(lam_re), per_channel(lam_im), per_channel(jnp.broadcast_to(log_dt[:, None], lam_re.shape)),
            b_re.transpose(0, 2, 1).reshape(rows, S5_STATE), b_im.transpose(0, 2, 1).reshape(rows, S5_STATE))
    out = jax.ShapeDtypeStruct((rows, S5_STATE), F32)
    a_re, a_im, bb_re, bb_im = pl.pallas_call(_s5_disc_body, out_shape=[out] * 4, name="s5_discretize")(*args)
    shape = (S5_GROUPS, S5_GROUP_CH, S5_STATE)
    return a_re.reshape(shape)[:, 0], a_im.reshape(shape)[:, 0], bb_re.reshape(shape), bb_im.reshape(shape)


def _s5_scan_body(u_ref, bb_ref, cc_ref, a_ref, h0_ref, y_ref, hout_ref, bu_scr, hs_scr, h_scr, *, tc, use_h0):
    t = pl.program_id(2)

    @pl.when(t == 0)
    def _():
        h_scr[...] = h0_ref[0, 0] if use_h0 else jnp.zeros(h_scr.shape, F32)

    bu_scr[...] = _mm(u_ref[...], bb_ref[0])
    ar = a_ref[0][:, :S5_HALF]
    ai = a_ref[0][:, S5_HALF:]

    def step(i, carry):
        hr, hi = carry
        r = pl.multiple_of(i * S5_NB, S5_NB)
        nhr = ar * hr - ai * hi + bu_scr[pl.ds(r, S5_NB), :S5_HALF]
        nhi = ar * hi + ai * hr + bu_scr[pl.ds(r, S5_NB), S5_HALF:]
        hs_scr[pl.ds(r, S5_NB), :S5_HALF] = nhr
        hs_scr[pl.ds(r, S5_NB), S5_HALF:] = nhi
        return nhr, nhi

    hr, hi = lax.fori_loop(0, tc, step, (h_scr[:, :S5_HALF], h_scr[:, S5_HALF:]), unroll=min(tc, 8))
    h_scr[:, :S5_HALF] = hr
    h_scr[:, S5_HALF:] = hi
    y_ref[...] = _mm(hs_scr[...], cc_ref[0])

    @pl.when(t == pl.num_programs(2) - 1)
    def _():
        hout_ref[0, 0] = h_scr[...]


def s5_scan(u, h0, a_re, a_im, bb_re, bb_im, c_re, c_im):
    n_real, t_len, _ = u.shape
    n = -(-n_real // S5_NB) * S5_NB
    if n != n_real:
        u = jnp.pad(u, ((0, n - n_real), (0, 0), (0, 0)))
        h0 = None if h0 is None else jnp.pad(h0, ((0, n - n_real), (0, 0), (0, 0), (0, 0)))
    nb = n // S5_NB
    ngb = S5_GROUPS // S5_GB
    tc = min(S5_TIME, t_len)
    eye = jnp.eye(S5_GB, dtype=F32)

    def blockdiag_in(bb):
        return jnp.einsum('ab,xacp->xacbp', eye, bb.reshape(ngb, S5_GB, S5_GROUP_CH, S5_STATE)).reshape(
            ngb, S5_GB * S5_GROUP_CH, S5_HALF)

    def blockdiag_out(cc):
        return jnp.einsum('ab,xbcp->xapbc', eye, cc.reshape(ngb, S5_GB, S5_GROUP_CH, S5_STATE)).reshape(
            ngb, S5_HALF, S5_GB * S5_GROUP_CH)

    bb = jnp.concatenate([blockdiag_in(bb_re), blockdiag_in(bb_im)], axis=2).astype(MXU_DTYPE)
    cc = jnp.concatenate([blockdiag_out(c_re), blockdiag_out(-c_im)], axis=1).astype(MXU_DTYPE)
    a = jnp.concatenate([a_re.reshape(ngb, S5_HALF), a_im.reshape(ngb, S5_HALF)], axis=1)
    a = jnp.broadcast_to(a[:, None, :], (ngb, S5_NB, 2 * S5_HALF))
    use_h0 = h0 is not None
    if use_h0:
        h0b = h0.reshape(nb, S5_NB, 2, ngb, S5_HALF).transpose(0, 3, 1, 2, 4).reshape(nb, ngb, S5_NB, 2 * S5_HALF)
    else:
        h0b = jnp.zeros((1, 1, S5_NB, 2 * S5_HALF), F32)
    ub = u.reshape(nb, S5_NB, t_len, D_MODEL).transpose(0, 2, 1, 3).reshape(nb * t_len * S5_NB, D_MODEL)
    per = t_len // tc
    lanes_u = S5_GB * S5_GROUP_CH

    rows_spec = pl.BlockSpec((tc * S5_NB, lanes_u), lambda b, g, t: (b * per + t, g))
    state_spec = pl.BlockSpec((1, 1, S5_NB, 2 * S5_HALF), lambda b, g, t: (b, g, 0, 0))
    h0_spec = state_spec if use_h0 else pl.BlockSpec((1, 1, S5_NB, 2 * S5_HALF), lambda b, g, t: (0, 0, 0, 0))
    y, h_last = pl.pallas_call(
        functools.partial(_s5_scan_body, tc=tc, use_h0=use_h0),
        grid=(nb, ngb, per),
        in_specs=[rows_spec, pl.BlockSpec((1, lanes_u, 2 * S5_HALF), lambda b, g, t: (g, 0, 0)),
                  pl.BlockSpec((1, 2 * S5_HALF, lanes_u), lambda b, g, t: (g, 0, 0)),
                  pl.BlockSpec((1, S5_NB, 2 * S5_HALF), lambda b, g, t: (g, 0, 0)), h0_spec],
        out_specs=[rows_spec, state_spec],
        out_shape=[jax.ShapeDtypeStruct((nb * t_len * S5_NB, D_MODEL), F32),
                   jax.ShapeDtypeStruct((nb, ngb, S5_NB, 2 * S5_HALF), F32)],
        scratch_shapes=[pltpu.VMEM((tc * S5_NB, 2 * S5_HALF), F32), pltpu.VMEM((tc * S5_NB, 2 * S5_HALF), F32),
                        pltpu.VMEM((S5_NB, 2 * S5_HALF), F32)],
        compiler_params=pltpu.CompilerParams(dimension_semantics=("parallel", "parallel", "arbitrary"),
                                             vmem_limit_bytes=VMEM_LIMIT),
        name="s5_scan",
    )(ub, bb, cc, a, h0b)
    y = y.reshape(nb, t_len, S5_NB, D_MODEL).transpose(0, 2, 1, 3).reshape(n, t_len, D_MODEL)
    h_last = h_last.reshape(nb, ngb, S5_NB, 2, S5_GB, S5_STATE).transpose(0, 2, 3, 1, 4, 5).reshape(
        n, 2, S5_GROUPS, S5_STATE)
    return y[:n_real], h_last[:n_real]


def _s5_out_body(x_ref, y_ref, u_ref, d_ref, w_ref, o_ref):
    z = jax.nn.gelu(y_ref[...] + d_ref[...] * u_ref[...])
    ab = _mm(z, w_ref[...])
    o_ref[...] = x_ref[...] + ab[:, :D_MODEL] * jax.nn.sigmoid(ab[:, D_MODEL:])


def s5_out(x, y, u, d_skip, w_glu):
    rows = x.shape[0]
    tm = _row_tile(rows)
    return pl.pallas_call(
        _s5_out_body,
        grid=(rows // tm,),
        in_specs=[_rows(tm, D_MODEL), _rows(tm, D_MODEL), _rows(tm, D_MODEL), _resident((1, D_MODEL)),
                  _resident((D_MODEL, 2 * D_MODEL))],
        out_specs=_rows(tm, D_MODEL),
        out_shape=jax.ShapeDtypeStruct((rows, D_MODEL), F32),
        compiler_params=_params(),
        name="s5_out",
    )(x, y, u, d_skip.reshape(1, -1), w_glu.astype(MXU_DTYPE))


HG_SUB = 16
HG_TIME = 256


def _mm_exact(l01, x):
    x1 = x.astype(MXU_DTYPE)
    r1 = x - x1.astype(F32)
    x2 = r1.astype(MXU_DTYPE)
    x3 = (r1 - x2.astype(F32)).astype(MXU_DTYPE)
    dot = functools.partial(jnp.dot, preferred_element_type=F32)
    return dot(l01, x1) + dot(l01, x2) + dot(l01, x3)


def _hgrn_body(q_ref, fz_ref, v_ref, g_ref, lb_ref, og_ref, s0_ref, o_ref, sout_ref, s_scr,
               *, tb, chunk, sub, valid, use_s0):
    tblk = pl.program_id(2)

    @pl.when(tblk == 0)
    def _():
        s_scr[...] = s0_ref[0, 0] if use_s0 else jnp.zeros(s_scr.shape, F32)

    lb = lb_ref[...]
    og = og_ref[...]
    eye = (lax.broadcasted_iota(jnp.int32, (HG_DK, HG_DK), 0) == lax.broadcasted_iota(jnp.int32, (HG_DK, HG_DK), 1))
    tril = jnp.where(lax.broadcasted_iota(jnp.int32, (chunk, chunk), 0)
                     >= lax.broadcasted_iota(jnp.int32, (chunk, chunk), 1), 1.0, 0.0).astype(MXU_DTYPE)
    trow = lax.broadcasted_iota(jnp.int32, (sub, 1), 0)
    nsub = chunk // sub

    for ci in range(tb // chunk):
        r0 = ci * chunk
        q = q_ref[r0:r0 + chunk, :]
        v = v_ref[r0:r0 + chunk, :]
        f = lb + (1.0 - lb) * jax.nn.sigmoid(fz_ref[r0:r0 + chunk, :])
        k = 1.0 - f
        lf = jnp.log(f)
        if valid < tb:
            live = (r0 + lax.broadcasted_iota(jnp.int32, (chunk, 1), 0)) < valid
            k = jnp.where(live, k, 0.0)
            lf = jnp.where(live, lf, 0.0)
        gcum = _mm_exact(tril, lf)
        state = s_scr[...]
        o_inter = _mm(q * jnp.exp(gcum), state)
        o_blocks = [o_inter[i * sub:(i + 1) * sub] for i in range(nsub)]

        for j in range(nsub - 1):
            lo, hi = j * sub, (j + 1) * sub
            g_ref_row = gcum[hi - 1:hi, :]
            k_t = k[lo:hi] * jnp.exp(g_ref_row - gcum[lo:hi])
            q_t = q[hi:] * jnp.exp(gcum[hi:] - g_ref_row)
            contrib = _mm(_mm_nt(q_t, k_t), v[lo:hi])
            for i in range(j + 1, nsub):
                o_blocks[i] = o_blocks[i] + contrib[(i - j - 1) * sub:(i - j) * sub]

        for i in range(nsub):
            lo, hi = i * sub, (i + 1) * sub
            q_i, k_i, v_i, g_i = q[lo:hi], k[lo:hi], v[lo:hi], gcum[lo:hi]
            acc = jnp.zeros((sub, HG_DV), F32)
            for s in range(sub):
                decay = jnp.exp(jnp.minimum(g_i - g_i[s:s + 1], 0.0))
                wgt = jnp.sum(q_i * k_i[s:s + 1] * decay, axis=1, keepdims=True)
                acc = acc + jnp.where(trow >= s, wgt, 0.0) * v_i[s:s + 1]
            o_blocks[i] = o_blocks[i] + acc

        g_last = gcum[chunk - 1:chunk, :]
        k_t = k * jnp.exp(g_last - gcum)
        decay_col = jnp.sum(jnp.where(eye, jnp.exp(g_last), 0.0), axis=1, keepdims=True)
        kv = lax.dot_general(k_t.astype(MXU_DTYPE), v.astype(MXU_DTYPE), (((0,), (0,)), ((), ())),
                             preferred_element_type=F32)
        s_scr[...] = decay_col * state + kv

        o = jnp.concatenate(o_blocks, axis=0)
        o = o * lax.rsqrt(jnp.mean(o * o, axis=-1, keepdims=True) + RMS_EPS) * og
        gate = g_ref[r0:r0 + chunk, :]
        o_ref[r0:r0 + chunk, :] = o * (gate * jax.nn.sigmoid(gate))

    @pl.when(tblk == pl.num_programs(2) - 1)
    def _():
        sout_ref[0, 0] = s_scr[...]


def hgrn_scan(pr, s0, o_gain, lb, n_seq, t_rows, valid):
    tb = min(HG_TIME, t_rows)
    chunk = min(HG_CHUNK, tb)
    sub = min(HG_SUB, chunk)
    per = t_rows // tb
    use_s0 = s0 is not None
    if not use_s0:
        s0 = jnp.zeros((1, 1, HG_DK, HG_DV), F32)

    def part(idx):
        return pl.BlockSpec((tb, HG_DK), lambda n, h, t: (n * per + t, idx * HG_HEADS + h))

    head_vec = pl.BlockSpec((1, HG_DK), lambda n, h, t: (0, h))
    state_spec = pl.BlockSpec((1, 1, HG_DK, HG_DV), lambda n, h, t: (n, h, 0, 0))
    s0_spec = state_spec if use_s0 else pl.BlockSpec((1, 1, HG_DK, HG_DV), lambda n, h, t: (0, 0, 0, 0))
    return pl.pallas_call(
        functools.partial(_hgrn_body, tb=tb, chunk=chunk, sub=sub, valid=valid, use_s0=use_s0),
        grid=(n_seq, HG_HEADS, per),
        in_specs=[part(0), part(1), part(2), part(3), head_vec, pl.BlockSpec((1, HG_DV), lambda n, h, t: (0, 0)), s0_spec],
        out_specs=[pl.BlockSpec((tb, HG_DV), lambda n, h, t: (n * per + t, h)), state_spec],
        out_shape=[jax.ShapeDtypeStruct((n_seq * t_rows, D_MODEL), F32),
                   jax.ShapeDtypeStruct((n_seq, HG_HEADS, HG_DK, HG_DV), F32)],
        scratch_shapes=[pltpu.VMEM((HG_DK, HG_DV), F32)],
        compiler_params=pltpu.CompilerParams(dimension_semantics=("parallel", "parallel", "arbitrary"),
                                             vmem_limit_bytes=VMEM_LIMIT),
        name="hgrn_scan",
    )(pr, pr, pr, pr, lb.reshape(1, -1), o_gain.reshape(1, -1), s0)


def kernel(x_prompt, x_sample, cache_nsa, state_nsa_win, state_s5, state_hgrn, page_table, p_prompt, p_sample, norm_gain, ffn_w_in, ffn_w_out, ple_w_gate, ple_w_proj, nsa_w_in, nsa_w_out, nsa_qk_gain, nsa_cmp_pe, nsa_cmp_w, s5_lam_re, s5_lam_im, s5_log_dt, s5_b_re, s5_b_im, s5_c_re, s5_c_im, s5_d, s5_w_glu, hg_w_in, hg_w_out, hg_o_gain, hg_lb_raw):
    B, T, _ = x_prompt.shape
    Bs, Ts, _ = x_sample.shape
    rp = B * T
    rs = Bs * Ts
    n_pages = page_table.shape[1]
    past_len = n_pages * PAGE_SIZE
    n_phys = cache_nsa.shape[1]
    wlen = state_nsa_win.shape[2]
    assert T % Q_TILE == 0 and T % PAGE_SIZE == 0 and T % min(KV_TILE, T) == 0 and Ts <= TS_PAD
    lb_sm = jax.nn.softmax(hg_lb_raw.astype(F32), axis=0)
    lower_bounds = jnp.cumsum(lb_sm, axis=0) - lb_sm[0]

    x = jnp.concatenate([x_prompt.reshape(rp, D_MODEL), x_sample.reshape(rs, D_MODEL)], axis=0)
    p_all = jnp.concatenate([p_prompt.reshape(DEPTH, rp, PLE_DIM), p_sample.reshape(DEPTH, rs, PLE_DIM)], axis=1)
    pos = jnp.concatenate([jnp.tile(jnp.arange(T), B), jnp.tile(past_len + jnp.arange(Ts), Bs)])
    cache_pages = cache_nsa.transpose(0, 1, 3, 4, 5, 2).reshape(-1, 4 * NSA_KV_W, PAGE_SIZE)
    win_state = state_nsa_win.transpose(0, 1, 3, 4, 5, 2).reshape(-1, 2 * NSA_KV_W, wlen)
    prompt_table = (jnp.arange(B, dtype=jnp.int32)[:, None] * (T // PAGE_SIZE)
                    + jnp.arange(T // PAGE_SIZE, dtype=jnp.int32)[None, :])
    out_perm = _head_perm()

    def split(a):
        return a[:rp].reshape(B, T, -1), a[rp:].reshape(Bs, Ts, -1)

    def join(a, b):
        return jnp.concatenate([a.reshape(rp, -1), b.reshape(rs, -1)], axis=0)

    def sample_pad(a):
        return jnp.pad(a[rp:].reshape(Bs, Ts, -1), ((0, 0), (0, TS_PAD - Ts), (0, 0)))

    outs_p = {0: [], 1: [], 2: [], 3: []}
    outs_s = {0: [], 1: [], 2: [], 3: []}
    for i in range(DEPTH):
        kind = LAYER_KIND[i]
        j = LAYER_SLOT[i]
        g = norm_gain[i]
        x, xn = ffn_step(x, g[0], g[1], ffn_w_in[i, 0], ffn_w_out[i, 0])
        if kind == 0:
            qn, qr, rows_new, win_new, gates = nsa_proj(xn, nsa_w_in[j], nsa_qk_gain[j], pos)
            cmp_args = (nsa_cmp_pe[j], nsa_cmp_w[j], nsa_qk_gain[j, 1])
            kc_p, vc_p = nsa_compress(rows_new.reshape(-1, PAGE_SIZE, 4 * NSA_KV_W), prompt_table, *cmp_args, False)
            o_p = nsa_attn_prompt(qn, qr, gates, kc_p, vc_p, rows_new, win_new, B, T)
            table = page_table.astype(jnp.int32) + j * n_phys
            kc_s, vc_s = nsa_compress(cache_pages, table, *cmp_args, True)
            o_s, w_s = nsa_attn_sample(table, cache_pages, sample_pad(qn), sample_pad(qr), sample_pad(gates), kc_s, vc_s,
                                       sample_pad(rows_new), win_state, sample_pad(win_new), j * Bs, past_len, Ts)
            r_p, r_s = split(rows_new)
            outs_p[0].append(r_p.reshape(B, T, 4, NSA_KV_HEADS, HEAD_DIM))
            outs_s[0].append(r_s.reshape(Bs, Ts, 4, NSA_KV_HEADS, HEAD_DIM))
            buf = min(WINDOW, T)
            outs_p[1].append(win_new[:rp].reshape(B, T, 2, NSA_KV_HEADS, HEAD_DIM)[:, T - buf:])
            outs_s[1].append(w_s.reshape(Bs, 2, NSA_KV_HEADS, HEAD_DIM, wlen).transpose(0, 4, 1, 2, 3))
            o_all = jnp.concatenate([o_p, o_s[:, :Ts].reshape(rs, NSA_Q_W).astype(o_p.dtype)], axis=0)
            x = resid_proj(x, o_all, nsa_w_out[j][out_perm])
        elif kind == 1:
            u_p, u_s = split(xn)
            disc = s5_discretize(s5_lam_re[j], s5_lam_im[j], s5_log_dt[j], s5_b_re[j], s5_b_im[j])
            y_p, h_p = s5_scan(u_p, None, *disc, s5_c_re[j], s5_c_im[j])
            y_s, h_s = s5_scan(u_s, state_s5[j], *disc, s5_c_re[j], s5_c_im[j])
            outs_p[2].append(h_p); outs_s[2].append(h_s)
            x = s5_out(x, join(y_p, y_s), xn, s5_d[j], s5_w_glu[j])
        else:
            pr = proj(xn, hg_w_in[j])
            pr_s = jnp.pad(pr[rp:].reshape(Bs, Ts, -1), ((0, 0), (0, TS_PAD - Ts), (0, 0))).reshape(Bs * TS_PAD, -1)
            o_p, s_p = hgrn_scan(pr, None, hg_o_gain[j], lower_bounds[i], B, T, T)
            o_s, s_s = hgrn_scan(pr_s, state_hgrn[j], hg_o_gain[j], lower_bounds[i], Bs, TS_PAD, Ts)
            outs_p[3].append(s_p); outs_s[3].append(s_s)
            o_s = o_s.reshape(Bs, TS_PAD, D_MODEL)[:, :Ts].reshape(rs, D_MODEL)
            x = resid_proj(x, jnp.concatenate([o_p, o_s], axis=0), hg_w_out[j])
        x, xn = ffn_step(x, g[2], g[3], ffn_w_in[i, 1], ffn_w_out[i, 1])
        x = ple_step(x, xn, p_all[i], ple_w_gate[i], ple_w_proj[i])

    y_p, y_s = split(x)
    return (y_p, y_s,
            jnp.stack(outs_p[0]), jnp.stack(outs_p[1]), jnp.stack(outs_p[2]), jnp.stack(outs_p[3]),
            jnp.stack(outs_s[0]), jnp.stack(outs_s[1]), jnp.stack(outs_s[2]), jnp.stack(outs_s[3]))
```

```python
import functools
import itertools

import numpy as np
import jax
import jax.numpy as jnp
from jax import lax
from jax.experimental import pallas as pl
from jax.experimental.pallas import tpu as pltpu

F32 = jnp.float32
MXU_DTYPE = jnp.bfloat16

D_MODEL = 1024
DEPTH = 4
PAGE_SIZE = 128
D_FF = 2816
PLE_DIM = 256
RMS_EPS = 1e-6
LAYER_KIND = (0, 1, 2, 0)
LAYER_SLOT = (0, 0, 0, 1)

NSA_HEADS = 16
NSA_KV_HEADS = 4
HEAD_DIM = 64
NSA_HPG = NSA_HEADS // NSA_KV_HEADS
NSA_KV_W = NSA_KV_HEADS * HEAD_DIM
NSA_Q_W = NSA_HEADS * HEAD_DIM
NSA_IN = NSA_Q_W + 6 * NSA_KV_W + 3 * NSA_HEADS
CMP_BLOCK = 32
CMP_STRIDE = 16
SLC_BLOCK = 64
SLC_TOPN = 16
WINDOW = 512
FORCE_BONUS = 1e4
NEG_INF = -1e30
ROPE_THETA = 500000.0
ROPE_DIMS = HEAD_DIM // 4
ATT_SCALE = HEAD_DIM ** -0.5

S5_GROUP_CH = 16
S5_GROUPS = D_MODEL // S5_GROUP_CH
S5_STATE = 64

HG_DK = 128
HG_HEADS = D_MODEL // HG_DK
HG_DV = D_MODEL // HG_HEADS
HG_CHUNK = 64

V7X_VMEM_BYTES = 64 * 1024 * 1024
VMEM_LIMIT = V7X_VMEM_BYTES - 8 * 1024 * 1024
LANES = 128
ROW_TILE = 512
FF_CHUNK = 256
Q_TILE = 128
KV_TILE = 512
TS_PAD = 8
SOFTMAX_M0 = -1e29


def _resident(shape):
    return pl.BlockSpec(shape, lambda *_: (0,) * len(shape), pipeline_mode=pl.Buffered(1))


def _row_tile(rows):
    return max(t for t in range(8, ROW_TILE + 1, 8) if rows % t == 0)


def _rows(tm, width):
    return pl.BlockSpec((tm, width), lambda i: (i, 0))


def _params(n_axes=1):
    return pltpu.CompilerParams(dimension_semantics=("parallel",) * n_axes, vmem_limit_bytes=VMEM_LIMIT)


def _rms(x, g):
    return x * lax.rsqrt(jnp.mean(x * x, axis=-1, keepdims=True) + RMS_EPS) * g


def _mm(a, b):
    return jnp.dot(a.astype(MXU_DTYPE), b.astype(MXU_DTYPE), preferred_element_type=F32)


def _mm_nt(a, b):
    return lax.dot_general(a.astype(MXU_DTYPE), b.astype(MXU_DTYPE), (((1,), (1,)), ((), ())),
                           preferred_element_type=F32)


def _mm_split(a, b):
    hi = a.astype(MXU_DTYPE)
    lo = (a - hi.astype(F32)).astype(MXU_DTYPE)
    return (jnp.dot(hi, b, preferred_element_type=F32) + jnp.dot(lo, b, preferred_element_type=F32))


def _ffn_body(x_ref, g_ref, gn_ref, win_ref, wout_ref, o_ref, on_ref, h_ref):
    x = x_ref[...]
    xb = _rms(x, g_ref[...]).astype(MXU_DTYPE)
    for c in range(D_FF // FF_CHUNK):
        lo = c * FF_CHUNK
        a = _mm(xb, win_ref[:, lo:lo + FF_CHUNK])
        b = _mm(xb, win_ref[:, D_FF + lo:D_FF + lo + FF_CHUNK])
        h_ref[:, lo:lo + FF_CHUNK] = (a * jax.nn.sigmoid(a) * b).astype(MXU_DTYPE)
    y = x + 0.5 * _mm(h_ref[...], wout_ref[...])
    o_ref[...] = y
    on_ref[...] = _rms(y, gn_ref[...])


def ffn_step(x, g, g_next, w_in, w_out):
    rows = x.shape[0]
    tm = _row_tile(rows)
    out = jax.ShapeDtypeStruct((rows, D_MODEL), F32)
    return pl.pallas_call(
        _ffn_body,
        grid=(rows // tm,),
        in_specs=[_rows(tm, D_MODEL), _resident((1, D_MODEL)), _resident((1, D_MODEL)),
                  _resident((D_MODEL, 2 * D_FF)), _resident((D_FF, D_MODEL))],
        out_specs=[_rows(tm, D_MODEL), _rows(tm, D_MODEL)],
        out_shape=[out, out],
        scratch_shapes=[pltpu.VMEM((tm, D_FF), MXU_DTYPE)],
        compiler_params=_params(),
        name="ffn_step",
    )(x, g.reshape(1, -1), g_next.reshape(1, -1), w_in.astype(MXU_DTYPE), w_out.astype(MXU_DTYPE))


def _ffn_ple_body(x_ref, g_ref, gn_ref, win_ref, wout_ref, p_ref, wg_ref, wp_ref, o_ref, h_ref):
    x = x_ref[...]
    xb = _rms(x, g_ref[...]).astype(MXU_DTYPE)
    for c in range(D_FF // FF_CHUNK):
        lo = c * FF_CHUNK
        a = _mm(xb, win_ref[:, lo:lo + FF_CHUNK])
        b = _mm(xb, win_ref[:, D_FF + lo:D_FF + lo + FF_CHUNK])
        h_ref[:, lo:lo + FF_CHUNK] = (a * jax.nn.sigmoid(a) * b).astype(MXU_DTYPE)
    y = x + 0.5 * _mm(h_ref[...], wout_ref[...])
    gate = jax.nn.sigmoid(_mm(_rms(y, gn_ref[...]), wg_ref[...]))
    o_ref[...] = y + gate * _mm(p_ref[...], wp_ref[...])


def ffn_ple_step(x, g, g_ple, w_in, w_out, p, w_gate, w_proj):
    rows = x.shape[0]
    tm = _row_tile(rows)
    return pl.pallas_call(
        _ffn_ple_body,
        grid=(rows // tm,),
        in_specs=[_rows(tm, D_MODEL), _resident((1, D_MODEL)), _resident((1, D_MODEL)),
                  _resident((D_MODEL, 2 * D_FF)), _resident((D_FF, D_MODEL)), _rows(tm, PLE_DIM),
                  _resident((D_MODEL, D_MODEL)), _resident((PLE_DIM, D_MODEL))],
        out_specs=_rows(tm, D_MODEL),
        out_shape=jax.ShapeDtypeStruct((rows, D_MODEL), F32),
        scratch_shapes=[pltpu.VMEM((tm, D_FF), MXU_DTYPE)],
        compiler_params=_params(),
        name="ffn_ple_step",
    )(x, g.reshape(1, -1), g_ple.reshape(1, -1), w_in.astype(MXU_DTYPE), w_out.astype(MXU_DTYPE), p,
      w_gate.astype(MXU_DTYPE), w_proj.astype(MXU_DTYPE))


def _proj_body(a_ref, w_ref, o_ref):
    o_ref[...] = _mm(a_ref[...], w_ref[...])


def proj(a, w):
    rows, k = a.shape
    n = w.shape[1]
    tm = _row_tile(rows)
    return pl.pallas_call(
        _proj_body,
        grid=(rows // tm,),
        in_specs=[_rows(tm, k), _resident((k, n))],
        out_specs=_rows(tm, n),
        out_shape=jax.ShapeDtypeStruct((rows, n), F32),
        compiler_params=_params(),
        name="proj",
    )(a, w.astype(MXU_DTYPE))


def _resid_body(x_ref, a_ref, w_ref, o_ref):
    o_ref[...] = x_ref[...] + _mm(a_ref[...], w_ref[...])


def resid_proj(x, a, w):
    rows, k = a.shape
    tm = _row_tile(rows)
    return pl.pallas_call(
        _resid_body,
        grid=(rows // tm,),
        in_specs=[_rows(tm, D_MODEL), _rows(tm, k), _resident((k, D_MODEL))],
        out_specs=_rows(tm, D_MODEL),
        out_shape=jax.ShapeDtypeStruct((rows, D_MODEL), F32),
        compiler_params=_params(),
        name="resid_proj",
    )(x, a, w.astype(MXU_DTYPE))


def _head_perm():
    idx = np.arange(NSA_Q_W).reshape(NSA_KV_HEADS, NSA_HPG, HEAD_DIM)
    return idx.transpose(1, 0, 2).reshape(-1)


def _gate_expand():
    x = np.zeros((LANES, 3 * NSA_Q_W), np.float32)
    for b in range(3):
        for g in range(NSA_KV_HEADS):
            for j in range(NSA_HPG):
                h = g * NSA_HPG + j
                c0 = b * NSA_Q_W + (j * NSA_KV_HEADS + g) * HEAD_DIM
                x[b * NSA_HEADS + h, c0:c0 + HEAD_DIM] = 1.0
    return x


def _overlap_matrix(nb, length):
    n_sel = -(-length // SLC_BLOCK)
    c0 = np.arange(nb)[:, None] * CMP_STRIDE
    s0 = np.arange(LANES)[None, :] * SLC_BLOCK
    ov = np.clip(np.minimum(c0 + CMP_BLOCK, s0 + SLC_BLOCK) - np.maximum(c0, s0), 0, None) / CMP_BLOCK
    ov = np.where(np.arange(LANES)[None, :] < n_sel, ov, 0.0)
    return ov.astype(np.float32)


def _rope_tables(pos):
    half = ROPE_DIMS // 2
    inv = ROPE_THETA ** (-jnp.arange(half, dtype=F32) / half)
    ang = pos.astype(F32)[:, None] * inv[None, :]
    cos, sin = jnp.cos(ang), jnp.sin(ang)
    ones = jnp.ones((pos.shape[0], HEAD_DIM - ROPE_DIMS), F32)
    zeros = jnp.zeros((pos.shape[0], HEAD_DIM - ROPE_DIMS), F32)
    zh = jnp.zeros_like(sin)
    c = jnp.concatenate([cos, cos, ones], axis=1)
    sa = jnp.concatenate([-sin, zh, zeros], axis=1)
    sb = jnp.concatenate([zh, sin, zeros], axis=1)
    rep = LANES // HEAD_DIM
    return jnp.tile(c, (1, rep)), jnp.tile(sa, (1, rep)), jnp.tile(sb, (1, rep))


def _nsa_proj_body(xn_ref, wq_ref, wkv_ref, wgl_ref, b64_ref, gq_ref, gk_ref, c_ref, sa_ref, sb_ref,
                   qn_ref, qr_ref, rows_ref, win_ref, gates_ref):
    xb = xn_ref[...].astype(MXU_DTYPE)
    c, sa, sb = c_ref[...], sa_ref[...], sb_ref[...]

    def head_norm(v, gain):
        w = v.shape[1]
        ms = _mm_split(v * v, b64_ref[:w, :w])
        return v * lax.rsqrt(ms + RMS_EPS) * gain

    def rope(v):
        w = v.shape[1]
        rep = w // LANES
        ct, sat, sbt = (jnp.concatenate([t] * rep, axis=1) for t in (c, sa, sb))
        return v * ct + pltpu.roll(v, w - ROPE_DIMS // 2, 1) * sat + pltpu.roll(v, ROPE_DIMS // 2, 1) * sbt

    qn = head_norm(_mm(xb, wq_ref[...]), gq_ref[...])
    qn_ref[...] = qn.astype(qn_ref.dtype)
    qr_ref[...] = rope(qn).astype(qr_ref.dtype)
    kv = _mm(xb, wkv_ref[...])
    w = NSA_KV_W
    rows_ref[:, 0:2 * w] = kv[:, 0:2 * w]
    rows_ref[:, 2 * w:3 * w] = rope(head_norm(kv[:, 2 * w:3 * w], gk_ref[0:1, :]))
    rows_ref[:, 3 * w:4 * w] = kv[:, 3 * w:4 * w]
    win_ref[:, 0:w] = rope(head_norm(kv[:, 4 * w:5 * w], gk_ref[1:2, :]))
    win_ref[:, w:2 * w] = kv[:, 5 * w:6 * w]
    gates_ref[...] = jax.nn.sigmoid(_mm(xb, wgl_ref[...]))


def nsa_proj(xn, w_in, qk_gain, pos):
    rows = xn.shape[0]
    tm = _row_tile(rows)
    kvw = 6 * NSA_KV_W
    wq = w_in[:, :NSA_Q_W][:, _head_perm()].astype(MXU_DTYPE)
    wkv = w_in[:, NSA_Q_W:NSA_Q_W + kvw].astype(MXU_DTYPE)
    wgl = jnp.pad(w_in[:, NSA_Q_W + kvw:], ((0, 0), (0, LANES - 3 * NSA_HEADS))).astype(MXU_DTYPE)
    b64 = jnp.asarray(np.kron(np.eye(NSA_HEADS), np.full((HEAD_DIM, HEAD_DIM), 1.0 / HEAD_DIM)), MXU_DTYPE)
    gq = jnp.tile(qk_gain[0], NSA_HEADS).reshape(1, -1)
    gk = jnp.stack([jnp.tile(qk_gain[2], NSA_KV_HEADS), jnp.tile(qk_gain[3], NSA_KV_HEADS)])
    c, sa, sb = _rope_tables(pos)
    return pl.pallas_call(
        _nsa_proj_body,
        grid=(rows // tm,),
        in_specs=[_rows(tm, D_MODEL), _resident((D_MODEL, NSA_Q_W)), _resident((D_MODEL, kvw)),
                  _resident((D_MODEL, LANES)), _resident((NSA_Q_W, NSA_Q_W)), _resident((1, NSA_Q_W)),
                  _resident((2, NSA_KV_W)), _rows(tm, LANES), _rows(tm, LANES), _rows(tm, LANES)],
        out_specs=[_rows(tm, NSA_Q_W), _rows(tm, NSA_Q_W), _rows(tm, 4 * NSA_KV_W), _rows(tm, 2 * NSA_KV_W),
                   _rows(tm, LANES)],
        out_shape=[jax.ShapeDtypeStruct((rows, NSA_Q_W), MXU_DTYPE), jax.ShapeDtypeStruct((rows, NSA_Q_W), MXU_DTYPE),
                   jax.ShapeDtypeStruct((rows, 4 * NSA_KV_W), F32), jax.ShapeDtypeStruct((rows, 2 * NSA_KV_W), F32),
                   jax.ShapeDtypeStruct((rows, LANES), F32)],
        compiler_params=_params(),
        name="nsa_proj",
    )(xn, wq, wkv, wgl, b64, gq, gk, c, sa, sb)


def _nsa_compress_body(pt_ref, *refs, n_pages, transposed):
    del pt_ref
    page_refs = refs[:n_pages]
    wk_ref, wv_ref, pek_ref, pev_ref, b64_ref, gk_ref, kc_ref, vc_ref, stage = refs[n_pages:]
    per_page = PAGE_SIZE // CMP_STRIDE
    nb = n_pages * per_page
    tiles = 2 * NSA_KV_W // LANES

    for p, r in enumerate(page_refs):
        for j in range(tiles):
            if transposed:
                stage[j, p * PAGE_SIZE:(p + 1) * PAGE_SIZE, :] = r[0, j * LANES:(j + 1) * LANES, :].T
            else:
                stage[j, p * PAGE_SIZE:(p + 1) * PAGE_SIZE, :] = r[0, :, j * LANES:(j + 1) * LANES]

    def compress(kind, w_ref, pe_ref):
        per_kind = NSA_KV_W // LANES
        first = jnp.zeros((nb, NSA_KV_W), F32)
        second = jnp.zeros((nb, NSA_KV_W), F32)
        for l in range(CMP_STRIDE):
            x = jnp.concatenate([stage[kind * per_kind + j, pl.ds(l, nb, stride=CMP_STRIDE), :]
                                 for j in range(per_kind)], axis=1)
            first = first + _mm(x + pe_ref[l:l + 1, :], w_ref[l])
            second = second + _mm(x + pe_ref[CMP_STRIDE + l:CMP_STRIDE + l + 1, :], w_ref[CMP_STRIDE + l])
        return first + pltpu.roll(second, nb - 1, 0)

    kc = compress(0, wk_ref, pek_ref)
    ms = _mm_split(kc * kc, b64_ref[...])
    kc_ref[0] = kc * lax.rsqrt(ms + RMS_EPS) * gk_ref[...]
    vc_ref[0] = compress(1, wv_ref, pev_ref)


def nsa_compress(pages, table, cmp_pe, cmp_w, k_gain, transposed):
    n_seq, n_pages = table.shape
    nb = n_pages * (PAGE_SIZE // CMP_STRIDE)
    eye = jnp.eye(NSA_KV_HEADS, dtype=F32)
    w4 = jnp.einsum('gh,klde->klgdhe', eye, cmp_w).reshape(2, CMP_BLOCK, NSA_KV_W, NSA_KV_W).astype(MXU_DTYPE)
    pe4 = jnp.tile(cmp_pe, (1, 1, NSA_KV_HEADS))
    b64 = jnp.asarray(np.kron(np.eye(NSA_KV_HEADS), np.full((HEAD_DIM, HEAD_DIM), 1.0 / HEAD_DIM)), MXU_DTYPE)
    gk = jnp.tile(k_gain, NSA_KV_HEADS).reshape(1, -1)

    page_block = (1, 2 * NSA_KV_W, PAGE_SIZE) if transposed else (1, PAGE_SIZE, 2 * NSA_KV_W)

    def page_spec(p):
        return pl.BlockSpec(page_block, lambda n, pt: (pt[n, p], 0, 0))

    def const(shape):
        return pl.BlockSpec(shape, lambda n, pt: (0,) * len(shape), pipeline_mode=pl.Buffered(1))

    out = jax.ShapeDtypeStruct((n_seq, nb, NSA_KV_W), F32)
    out_spec = pl.BlockSpec((1, nb, NSA_KV_W), lambda n, pt: (n, 0, 0))
    grid_spec = pltpu.PrefetchScalarGridSpec(
        num_scalar_prefetch=1, grid=(n_seq,),
        in_specs=[page_spec(p) for p in range(n_pages)] + [
            const((CMP_BLOCK, NSA_KV_W, NSA_KV_W)), const((CMP_BLOCK, NSA_KV_W, NSA_KV_W)),
            const((CMP_BLOCK, NSA_KV_W)), const((CMP_BLOCK, NSA_KV_W)), const((NSA_KV_W, NSA_KV_W)),
            const((1, NSA_KV_W))],
        out_specs=[out_spec, out_spec],
        scratch_shapes=[pltpu.VMEM((2 * NSA_KV_W // LANES, n_pages * PAGE_SIZE, LANES), F32)])
    return pl.pallas_call(
        functools.partial(_nsa_compress_body, n_pages=n_pages, transposed=transposed),
        grid_spec=grid_spec, out_shape=[out, out], compiler_params=_params(), name="nsa_compress",
    )(table, *([pages] * n_pages), w4[0], w4[1], pe4[0], pe4[1], b64, gk)


def _group_masks():
    lane = lax.broadcasted_iota(jnp.int32, (1, NSA_KV_W), 1)
    return [(lane // HEAD_DIM) == g for g in range(NSA_KV_HEADS)]


def _blockdiag(q, bm):
    zero = jnp.zeros((), q.dtype)
    return jnp.concatenate([jnp.where(bm[g], q[:, NSA_KV_W * j:NSA_KV_W * (j + 1)], zero)
                            for g in range(NSA_KV_HEADS) for j in range(NSA_HPG)], axis=0)


def _extract(obd, bm, tq):
    outs = []
    for j in range(NSA_HPG):
        z = jnp.zeros((tq, NSA_KV_W), F32)
        for g in range(NSA_KV_HEADS):
            r0 = (g * NSA_HPG + j) * tq
            z = z + jnp.where(bm[g], obd[r0:r0 + tq], 0.0)
        outs.append(z)
    return jnp.concatenate(outs, axis=1)


def _per_head_rows(a, tq):
    k = a.shape[1]
    a4 = jnp.broadcast_to(a.reshape(NSA_KV_HEADS, 1, tq, k), (NSA_KV_HEADS, NSA_HPG, tq, k))
    return a4.reshape(NSA_HEADS * tq, k)


def _topk_mask(score, n_sel, tq):
    if tq == LANES:
        nsp = -(-n_sel // 8) * 8
        sub = lax.broadcasted_iota(jnp.int32, (nsp, 1), 0)
        outs = []
        for g in range(NSA_KV_HEADS):
            st = score[g * tq:(g + 1) * tq].T[:nsp]
            rank = jnp.zeros(st.shape, F32)
            for s2 in range(n_sel):
                row = st[s2:s2 + 1, :]
                rank = rank + jnp.where(row > st, 1.0, jnp.where(row == st, jnp.where(sub > s2, 1.0, 0.0), 0.0))
            sel_t = jnp.where(rank < SLC_TOPN, 1.0, 0.0)
            sel_t = jnp.concatenate([sel_t, jnp.zeros((LANES - nsp, tq), F32)], axis=0)
            outs.append(sel_t.T)
        return jnp.concatenate(outs, axis=0)
    lane = lax.broadcasted_iota(jnp.int32, (1, LANES), 1)
    rank = jnp.zeros(score.shape, F32)
    for s2 in range(n_sel):
        col = score[:, s2:s2 + 1]
        rank = rank + jnp.where(col > score, 1.0, jnp.where(col == score, jnp.where(lane > s2, 1.0, 0.0), 0.0))
    return jnp.where(rank < SLC_TOPN, 1.0, 0.0)


def _cmp_and_select(qbd_n, kc, vc, ov, pos_base, tq, n_sel):
    nb = kc.shape[0]
    r = lax.broadcasted_iota(jnp.int32, (NSA_HEADS * tq, 1), 0)
    qpos = pos_base + (r & (tq - 1))
    blk_end = lax.broadcasted_iota(jnp.int32, (1, nb), 1) * CMP_STRIDE + (CMP_BLOCK - 1)
    visible = blk_end <= qpos
    s = jnp.where(visible, _mm_nt(qbd_n, kc) * ATT_SCALE, NEG_INF)
    e = jnp.exp(s - jnp.max(s, axis=-1, keepdims=True))
    p = jnp.where(visible, e / jnp.sum(e, axis=-1, keepdims=True), 0.0)
    o = _mm(p, vc)
    psum = jnp.sum(p.reshape(NSA_KV_HEADS, NSA_HPG, tq, nb), axis=1).reshape(NSA_KV_HEADS * tq, nb)
    imp = _mm_split(psum, ov.astype(MXU_DTYPE))
    r4 = lax.broadcasted_iota(jnp.int32, (NSA_KV_HEADS * tq, 1), 0)
    qblk = (pos_base + (r4 & (tq - 1))) // SLC_BLOCK
    sidx = lax.broadcasted_iota(jnp.int32, (1, LANES), 1)
    bonus = jnp.where(sidx == 0, FORCE_BONUS, jnp.where(sidx == qblk, FORCE_BONUS,
                                                        jnp.where(sidx == qblk - 1, FORCE_BONUS, 0.0)))
    score = jnp.where(sidx <= qblk, imp + bonus, NEG_INF)
    return o, _topk_mask(score, n_sel, tq), qpos


def _gated_sum(gates, gx, o_cmp, o_slc, o_win, bm, tq):
    gf = _mm_split(gates, gx)
    return (gf[:, 0:NSA_Q_W] * _extract(o_cmp, bm, tq) + gf[:, NSA_Q_W:2 * NSA_Q_W] * _extract(o_slc, bm, tq)
            + gf[:, 2 * NSA_Q_W:] * _extract(o_win, bm, tq))


def _nsa_prompt_body(qn_ref, qr_ref, gates_ref, kc_ref, vc_ref, kv_ref, win_ref, ov_ref, gx_ref, o_ref,
                     *, seq, n_sel, kt, wk):
    tq = Q_TILE
    t0 = pl.program_id(1) * tq
    bm = _group_masks()
    w = NSA_KV_W
    o_cmp, sel, _ = _cmp_and_select(_blockdiag(qn_ref[...], bm), kc_ref[0], vc_ref[0], ov_ref[...], t0, tq, n_sel)
    qbd = _blockdiag(qr_ref[...], bm) * ATT_SCALE
    sel_b = sel.astype(MXU_DTYPE)
    sidx = lax.broadcasted_iota(jnp.int32, (LANES, 1), 0)
    qpos_t = t0 + lax.broadcasted_iota(jnp.int32, (tq, 1), 0)
    qpos_gt = t0 + (lax.broadcasted_iota(jnp.int32, (NSA_KV_HEADS * tq, 1), 0) & (tq - 1))

    def kv_tile(jt, carry):
        m, l, acc = carry
        k0 = pl.multiple_of(jt * kt, kt)
        kpos = k0 + lax.broadcasted_iota(jnp.int32, (1, kt), 1)
        expand = jnp.where((kpos // SLC_BLOCK) == sidx, 1.0, 0.0).astype(MXU_DTYPE)
        bias = (jnp.dot(sel_b, expand, preferred_element_type=F32) - 1.0) * (-NEG_INF)
        bias = jnp.where(kpos <= qpos_gt, bias, NEG_INF)
        s = _mm_nt(qbd, kv_ref[pl.ds(k0, kt), 0:w]) + _per_head_rows(bias, tq)
        m_new = jnp.maximum(m, jnp.max(s, axis=-1, keepdims=True))
        alpha = jnp.exp(m - m_new)
        e = jnp.exp(s - m_new)
        l = alpha * l + jnp.sum(e, axis=-1, keepdims=True)
        acc = alpha * acc + _mm(e, kv_ref[pl.ds(k0, kt), w:2 * w])
        return m_new, l, acc

    rows = NSA_HEADS * tq
    n_tiles = (t0 + tq + kt - 1) // kt
    m, l, acc = lax.fori_loop(0, n_tiles, kv_tile, (jnp.full((rows, 1), SOFTMAX_M0, F32), jnp.zeros((rows, 1), F32),
                                                    jnp.zeros((rows, w), F32)))
    o_slc = acc * (1.0 / l)

    start = pl.multiple_of(jnp.maximum(t0 + tq - wk, 0), tq)
    kpos = start + lax.broadcasted_iota(jnp.int32, (1, wk), 1)
    bias = jnp.where(kpos <= qpos_t, jnp.where(kpos > qpos_t - WINDOW, 0.0, NEG_INF), NEG_INF)
    bias = jnp.broadcast_to(bias.reshape(1, tq, wk), (NSA_HEADS, tq, wk)).reshape(rows, wk)
    s = _mm_nt(qbd, win_ref[pl.ds(start, wk), 0:w]) + bias
    e = jnp.exp(s - jnp.max(s, axis=-1, keepdims=True))
    o_win = _mm(e, win_ref[pl.ds(start, wk), w:2 * w]) * (1.0 / jnp.sum(e, axis=-1, keepdims=True))
    o_ref[...] = _gated_sum(gates_ref[...], gx_ref[...], o_cmp, o_slc, o_win, bm, tq).astype(o_ref.dtype)


def nsa_attn_prompt(qn, qr, gates, kc, vc, rows_new, win_new, n_seq, seq):
    tq = Q_TILE
    nb = kc.shape[1]
    n_sel = -(-seq // SLC_BLOCK)
    kt = min(KV_TILE, seq)
    wk = min(WINDOW + tq, seq)
    ov = jnp.asarray(_overlap_matrix(nb, seq))
    gx = jnp.asarray(_gate_expand(), MXU_DTYPE)
    per = seq // tq

    def qrows(width):
        return pl.BlockSpec((tq, width), lambda n, t: (n * per + t, 0))

    def per_seq(shape, lane_block=0):
        return pl.BlockSpec(shape, lambda n, t: (n,) + (0,) * (len(shape) - 2) + (lane_block,))

    return pl.pallas_call(
        functools.partial(_nsa_prompt_body, seq=seq, n_sel=n_sel, kt=kt, wk=wk),
        grid=(n_seq, per),
        in_specs=[qrows(NSA_Q_W), qrows(NSA_Q_W), qrows(LANES), per_seq((1, nb, NSA_KV_W)), per_seq((1, nb, NSA_KV_W)),
                  per_seq((seq, 2 * NSA_KV_W), 1), per_seq((seq, 2 * NSA_KV_W)),
                  _resident((nb, LANES)), _resident((LANES, 3 * NSA_Q_W))],
        out_specs=qrows(NSA_Q_W),
        out_shape=jax.ShapeDtypeStruct((n_seq * seq, NSA_Q_W), MXU_DTYPE),
        compiler_params=_params(2),
        name="nsa_attn_prompt",
    )(qn, qr, gates, kc, vc, rows_new, win_new, ov, gx)


def _nsa_sample_body(pt_ref, *refs, n_pages, past_len, ts, n_sel):
    del pt_ref
    page_refs = refs[:n_pages]
    (qn_ref, qr_ref, gates_ref, kc_ref, vc_ref, rnew_ref, wold_ref, wnew_ref, ov_ref, gx_ref, ex_ref,
     o_ref, wout_ref) = refs[n_pages:]
    tq = TS_PAD
    w = NSA_KV_W
    bm = _group_masks()
    o_cmp, sel, qpos = _cmp_and_select(_blockdiag(qn_ref[0].astype(F32), bm), kc_ref[0], vc_ref[0], ov_ref[...],
                                       past_len, tq, n_sel)
    qbd = _blockdiag(qr_ref[0].astype(F32), bm).astype(MXU_DTYPE)
    pad = jnp.zeros((PAGE_SIZE - tq, w), F32)

    bias = (jnp.dot(sel.astype(MXU_DTYPE), ex_ref[...], preferred_element_type=F32) - 1.0) * (-NEG_INF)
    k_new = jnp.concatenate([rnew_ref[0][:, 0:w], pad], axis=0)
    v_new = jnp.concatenate([rnew_ref[0][:, w:2 * w], pad], axis=0)
    s = jnp.concatenate([_mm(qbd, r[0, 0:w, :]) for r in page_refs] + [_mm_nt(qbd, k_new)], axis=1)
    s = s * ATT_SCALE + _per_head_rows(bias, tq)
    kpos = lax.broadcasted_iota(jnp.int32, (1, (n_pages + 1) * PAGE_SIZE), 1)
    s = jnp.where(kpos <= qpos, s, NEG_INF)
    e = jnp.exp(s - jnp.max(s, axis=-1, keepdims=True))
    acc = _mm(e[:, n_pages * PAGE_SIZE:], v_new)
    for p, r in enumerate(page_refs):
        acc = acc + _mm_nt(e[:, p * PAGE_SIZE:(p + 1) * PAGE_SIZE], r[0, w:2 * w, :])
    o_slc = acc * (1.0 / jnp.sum(e, axis=-1, keepdims=True))

    wlen = wold_ref.shape[2]
    kw_new = jnp.concatenate([wnew_ref[0][:, 0:w], pad], axis=0)
    vw_new = jnp.concatenate([wnew_ref[0][:, w:2 * w], pad], axis=0)
    s = jnp.concatenate([_mm(qbd, wold_ref[0, 0:w, :]), _mm_nt(qbd, kw_new)], axis=1) * ATT_SCALE
    kpos = (past_len - wlen) + lax.broadcasted_iota(jnp.int32, (1, wlen + PAGE_SIZE), 1)
    s = jnp.where(kpos <= qpos, s, NEG_INF)
    s = jnp.where(kpos > qpos - WINDOW, s, NEG_INF)
    e = jnp.exp(s - jnp.max(s, axis=-1, keepdims=True))
    acc = _mm_nt(e[:, 0:wlen], wold_ref[0, w:2 * w, :]) + _mm(e[:, wlen:], vw_new)
    o_win = acc * (1.0 / jnp.sum(e, axis=-1, keepdims=True))

    o_ref[0] = _gated_sum(gates_ref[0], gx_ref[...], o_cmp, o_slc, o_win, bm, tq)

    shifted = pltpu.roll(wold_ref[0], wlen - ts, 1)
    new_rows = jnp.concatenate([wnew_ref[0], jnp.zeros((LANES - tq, 2 * w), F32)], axis=0)
    new_t = jnp.concatenate([new_rows[:, j * LANES:(j + 1) * LANES].T for j in range(2 * w // LANES)], axis=0)
    new_t = pltpu.roll(new_t, LANES - ts, 1)
    lane = lax.broadcasted_iota(jnp.int32, (1, LANES), 1)
    wout_ref[0, :, 0:wlen - LANES] = shifted[:, 0:wlen - LANES]
    wout_ref[0, :, wlen - LANES:wlen] = jnp.where(lane >= LANES - ts, new_t, shifted[:, wlen - LANES:wlen])


def nsa_attn_sample(table, pages, qn, qr, gates, kc, vc, rows_new, win_old, win_new, win_base, past_len, ts):
    n_seq, n_pages = table.shape
    nb = kc.shape[1]
    wlen = win_old.shape[2]
    length = past_len + ts
    n_sel = -(-length // SLC_BLOCK)
    n_keys = (n_pages + 1) * PAGE_SIZE
    ov = jnp.asarray(_overlap_matrix(nb, length))
    gx = jnp.asarray(_gate_expand(), MXU_DTYPE)
    ex = jnp.asarray((np.arange(n_keys)[None, :] // SLC_BLOCK == np.arange(LANES)[:, None]).astype(np.float32), MXU_DTYPE)

    def page_spec(p):
        return pl.BlockSpec((1, 2 * NSA_KV_W, PAGE_SIZE), lambda n, pt: (pt[n, p], 1, 0))

    def per_seq(shape, lane_block=0, base=0):
        return pl.BlockSpec(shape, lambda n, pt: (base + n,) + (0,) * (len(shape) - 2) + (lane_block,))

    def const(shape):
        return pl.BlockSpec(shape, lambda n, pt: (0,) * len(shape), pipeline_mode=pl.Buffered(1))

    grid_spec = pltpu.PrefetchScalarGridSpec(
        num_scalar_prefetch=1, grid=(n_seq,),
        in_specs=[page_spec(p) for p in range(n_pages)] + [
            per_seq((1, TS_PAD, NSA_Q_W)), per_seq((1, TS_PAD, NSA_Q_W)), per_seq((1, TS_PAD, LANES)),
            per_seq((1, nb, NSA_KV_W)), per_seq((1, nb, NSA_KV_W)), per_seq((1, TS_PAD, 2 * NSA_KV_W), 1),
            per_seq((1, 2 * NSA_KV_W, wlen), 0, win_base), per_seq((1, TS_PAD, 2 * NSA_KV_W)),
            const((nb, LANES)), const((LANES, 3 * NSA_Q_W)), const((LANES, n_keys))],
        out_specs=[per_seq((1, TS_PAD, NSA_Q_W)), per_seq((1, 2 * NSA_KV_W, wlen))])
    return pl.pallas_call(
        functools.partial(_nsa_sample_body, n_pages=n_pages, past_len=past_len, ts=ts, n_sel=n_sel),
        grid_spec=grid_spec,
        out_shape=[jax.ShapeDtypeStruct((n_seq, TS_PAD, NSA_Q_W), F32),
                   jax.ShapeDtypeStruct((n_seq, 2 * NSA_KV_W, wlen), F32)],
        compiler_params=_params(), name="nsa_attn_sample",
    )(table, *([pages] * n_pages), qn, qr, gates, kc, vc, rows_new, win_old, win_new, ov, gx, ex)


S5_NB = 8
S5_GB = 8
S5_HALF = S5_GB * S5_STATE
S5_TIME = 256


def _s5_disc_body(lr_ref, li_ref, ldt_ref, bre_ref, bim_ref, are_ref, aim_ref, bbre_ref, bbim_ref):
    dt = jnp.exp(ldt_ref[...])
    lr = jnp.minimum(lr_ref[...], -1e-4)
    li = li_ref[...]
    mag = jnp.exp(lr * dt)
    a_re = mag * jnp.cos(li * dt)
    a_im = mag * jnp.sin(li * dt)
    den = lr * lr + li * li
    z_re = ((a_re - 1.0) * lr + a_im * li) / den
    z_im = (a_im * lr - (a_re - 1.0) * li) / den
    are_ref[...] = a_re
    aim_ref[...] = a_im
    bbre_ref[...] = z_re * bre_ref[...] - z_im * bim_ref[...]
    bbim_ref[...] = z_re * bim_ref[...] + z_im * bre_ref[...]


def s5_discretize(lam_re, lam_im, log_dt, b_re, b_im):
    rows = S5_GROUPS * S5_GROUP_CH

    def per_channel(a):
        return jnp.broadcast_to(a[:, None, :], (S5_GROUPS, S5_GROUP_CH, S5_STATE)).reshape(rows, S5_STATE)

    args = (per_channel(lam_re), per_channel(lam_im), per_channel(jnp.broadcast_to(log_dt[:, None], lam_re.shape)),
            b_re.transpose(0, 2, 1).reshape(rows, S5_STATE), b_im.transpose(0, 2, 1).reshape(rows, S5_STATE))
    out = jax.ShapeDtypeStruct((rows, S5_STATE), F32)
    a_re, a_im, bb_re, bb_im = pl.pallas_call(_s5_disc_body, out_shape=[out] * 4, name="s5_discretize")(*args)
    shape = (S5_GROUPS, S5_GROUP_CH, S5_STATE)
    return a_re.reshape(shape)[:, 0], a_im.reshape(shape)[:, 0], bb_re.reshape(shape), bb_im.reshape(shape)


def _s5_scan_body(u_ref, bb_ref, cc_ref, a_ref, h0_ref, y_ref, hout_ref, bu_scr, hs_scr, h_scr, *, tc, use_h0):
    t = pl.program_id(2)

    @pl.when(t == 0)
    def _():
        h_scr[...] = h0_ref[0, 0] if use_h0 else jnp.zeros(h_scr.shape, F32)

    bu_scr[...] = _mm(u_ref[...], bb_ref[0])
    ar = a_ref[0][:, :S5_HALF]
    ai = a_ref[0][:, S5_HALF:]

    def step(i, carry):
        hr, hi = carry
        r = pl.multiple_of(i * S5_NB, S5_NB)
        nhr = ar * hr - ai * hi + bu_scr[pl.ds(r, S5_NB), :S5_HALF]
        nhi = ar * hi + ai * hr + bu_scr[pl.ds(r, S5_NB), S5_HALF:]
        hs_scr[pl.ds(r, S5_NB), :S5_HALF] = nhr
        hs_scr[pl.ds(r, S5_NB), S5_HALF:] = nhi
        return nhr, nhi

    hr, hi = lax.fori_loop(0, tc, step, (h_scr[:, :S5_HALF], h_scr[:, S5_HALF:]), unroll=min(tc, 8))
    h_scr[:, :S5_HALF] = hr
    h_scr[:, S5_HALF:] = hi
    y_ref[...] = _mm(hs_scr[...], cc_ref[0])

    @pl.when(t == pl.num_programs(2) - 1)
    def _():
        hout_ref[0, 0] = h_scr[...]


def s5_scan(u, h0, a_re, a_im, bb_re, bb_im, c_re, c_im):
    n_real, t_len, _ = u.shape
    n = -(-n_real // S5_NB) * S5_NB
    if n != n_real:
        u = jnp.pad(u, ((0, n - n_real), (0, 0), (0, 0)))
        h0 = None if h0 is None else jnp.pad(h0, ((0, n - n_real), (0, 0), (0, 0), (0, 0)))
    nb = n // S5_NB
    ngb = S5_GROUPS // S5_GB
    tc = min(S5_TIME, t_len)
    eye = jnp.eye(S5_GB, dtype=F32)

    def blockdiag_in(bb):
        return jnp.einsum('ab,xacp->xacbp', eye, bb.reshape(ngb, S5_GB, S5_GROUP_CH, S5_STATE)).reshape(
            ngb, S5_GB * S5_GROUP_CH, S5_HALF)

    def blockdiag_out(cc):
        return jnp.einsum('ab,xbcp->xapbc', eye, cc.reshape(ngb, S5_GB, S5_GROUP_CH, S5_STATE)).reshape(
            ngb, S5_HALF, S5_GB * S5_GROUP_CH)

    bb = jnp.concatenate([blockdiag_in(bb_re), blockdiag_in(bb_im)], axis=2).astype(MXU_DTYPE)
    cc = jnp.concatenate([blockdiag_out(c_re), blockdiag_out(-c_im)], axis=1).astype(MXU_DTYPE)
    a = jnp.concatenate([a_re.reshape(ngb, S5_HALF), a_im.reshape(ngb, S5_HALF)], axis=1)
    a = jnp.broadcast_to(a[:, None, :], (ngb, S5_NB, 2 * S5_HALF))
    use_h0 = h0 is not None
    if use_h0:
        h0b = h0.reshape(nb, S5_NB, 2, ngb, S5_HALF).transpose(0, 3, 1, 2, 4).reshape(nb, ngb, S5_NB, 2 * S5_HALF)
    else:
        h0b = jnp.zeros((1, 1, S5_NB, 2 * S5_HALF), F32)
    ub = u.reshape(nb, S5_NB, t_len, D_MODEL).transpose(0, 2, 1, 3).reshape(nb * t_len * S5_NB, D_MODEL)
    per = t_len // tc
    lanes_u = S5_GB * S5_GROUP_CH

    rows_spec = pl.BlockSpec((tc * S5_NB, lanes_u), lambda b, g, t: (b * per + t, g))
    state_spec = pl.BlockSpec((1, 1, S5_NB, 2 * S5_HALF), lambda b, g, t: (b, g, 0, 0))
    h0_spec = state_spec if use_h0 else pl.BlockSpec((1, 1, S5_NB, 2 * S5_HALF), lambda b, g, t: (0, 0, 0, 0))
    y, h_last = pl.pallas_call(
        functools.partial(_s5_scan_body, tc=tc, use_h0=use_h0),
        grid=(nb, ngb, per),
        in_specs=[rows_spec, pl.BlockSpec((1, lanes_u, 2 * S5_HALF), lambda b, g, t: (g, 0, 0)),
                  pl.BlockSpec((1, 2 * S5_HALF, lanes_u), lambda b, g, t: (g, 0, 0)),
                  pl.BlockSpec((1, S5_NB, 2 * S5_HALF), lambda b, g, t: (g, 0, 0)), h0_spec],
        out_specs=[rows_spec, state_spec],
        out_shape=[jax.ShapeDtypeStruct((nb * t_len * S5_NB, D_MODEL), F32),
                   jax.ShapeDtypeStruct((nb, ngb, S5_NB, 2 * S5_HALF), F32)],
        scratch_shapes=[pltpu.VMEM((tc * S5_NB, 2 * S5_HALF), F32), pltpu.VMEM((tc * S5_NB, 2 * S5_HALF), F32),
                        pltpu.VMEM((S5_NB, 2 * S5_HALF), F32)],
        compiler_params=pltpu.CompilerParams(dimension_semantics=("parallel", "parallel", "arbitrary"),
                                             vmem_limit_bytes=VMEM_LIMIT),
        name="s5_scan",
    )(ub, bb, cc, a, h0b)
    y = y.reshape(nb, t_len, S5_NB, D_MODEL).transpose(0, 2, 1, 3).reshape(n, t_len, D_MODEL)
    h_last = h_last.reshape(nb, ngb, S5_NB, 2, S5_GB, S5_STATE).transpose(0, 2, 3, 1, 4, 5).reshape(
        n, 2, S5_GROUPS, S5_STATE)
    return y[:n_real], h_last[:n_real]


def _s5_out_body(x_ref, y_ref, u_ref, d_ref, w_ref, o_ref):
    z = jax.nn.gelu(y_ref[...] + d_ref[...] * u_ref[...])
    ab = _mm(z, w_ref[...])
    o_ref[...] = x_ref[...] + ab[:, :D_MODEL] * jax.nn.sigmoid(ab[:, D_MODEL:])


def s5_out(x, y, u, d_skip, w_glu):
    rows = x.shape[0]
    tm = _row_tile(rows)
    return pl.pallas_call(
        _s5_out_body,
        grid=(rows // tm,),
        in_specs=[_rows(tm, D_MODEL), _rows(tm, D_MODEL), _rows(tm, D_MODEL), _resident((1, D_MODEL)),
                  _resident((D_MODEL, 2 * D_MODEL))],
        out_specs=_rows(tm, D_MODEL),
        out_shape=jax.ShapeDtypeStruct((rows, D_MODEL), F32),
        compiler_params=_params(),
        name="s5_out",
    )(x, y, u, d_skip.reshape(1, -1), w_glu.astype(MXU_DTYPE))


HG_SUB = 16
HG_TIME = 256


def _mm_exact(l01, x):
    x1 = x.astype(MXU_DTYPE)
    r1 = x - x1.astype(F32)
    x2 = r1.astype(MXU_DTYPE)
    x3 = (r1 - x2.astype(F32)).astype(MXU_DTYPE)
    dot = functools.partial(jnp.dot, preferred_element_type=F32)
    return dot(l01, x1) + dot(l01, x2) + dot(l01, x3)


def _hgrn_body(q_ref, fz_ref, v_ref, g_ref, lb_ref, og_ref, s0_ref, o_ref, sout_ref, s_scr,
               *, tb, chunk, sub, valid, use_s0, hps):
    tblk = pl.program_id(2)

    @pl.when(tblk == 0)
    def _():
        s_scr[...] = s0_ref[0] if use_s0 else jnp.zeros(s_scr.shape, F32)

    og = og_ref[...]
    eye = (lax.broadcasted_iota(jnp.int32, (HG_DK, HG_DK), 0) == lax.broadcasted_iota(jnp.int32, (HG_DK, HG_DK), 1))
    tril = jnp.where(lax.broadcasted_iota(jnp.int32, (chunk, chunk), 0)
                     >= lax.broadcasted_iota(jnp.int32, (chunk, chunk), 1), 1.0, 0.0).astype(MXU_DTYPE)
    trow = lax.broadcasted_iota(jnp.int32, (sub, 1), 0)
    nsub = chunk // sub

    for hh, ci in itertools.product(range(hps), range(tb // chunk)):
        r0 = ci * chunk
        cols = slice(hh * HG_DK, (hh + 1) * HG_DK)
        lb = lb_ref[:, cols]
        q = q_ref[r0:r0 + chunk, cols]
        v = v_ref[r0:r0 + chunk, cols]
        f = lb + (1.0 - lb) * jax.nn.sigmoid(fz_ref[r0:r0 + chunk, cols])
        k = 1.0 - f
        lf = jnp.log(f)
        if valid < tb:
            live = (r0 + lax.broadcasted_iota(jnp.int32, (chunk, 1), 0)) < valid
            k = jnp.where(live, k, 0.0)
            lf = jnp.where(live, lf, 0.0)
        gcum = _mm_exact(tril, lf)
        state = s_scr[hh]
        o_inter = _mm(q * jnp.exp(gcum), state)
        o_blocks = [o_inter[i * sub:(i + 1) * sub] for i in range(nsub)]

        for j in range(nsub - 1):
            lo, hi = j * sub, (j + 1) * sub
            g_ref_row = gcum[hi - 1:hi, :]
            k_t = k[lo:hi] * jnp.exp(g_ref_row - gcum[lo:hi])
            q_t = q[hi:] * jnp.exp(gcum[hi:] - g_ref_row)
            contrib = _mm(_mm_nt(q_t, k_t), v[lo:hi])
            for i in range(j + 1, nsub):
                o_blocks[i] = o_blocks[i] + contrib[(i - j - 1) * sub:(i - j) * sub]

        for i in range(nsub):
            lo, hi = i * sub, (i + 1) * sub
            q_i, k_i, v_i, g_i = q[lo:hi], k[lo:hi], v[lo:hi], gcum[lo:hi]
            acc = jnp.zeros((sub, HG_DV), F32)
            for s in range(sub):
                decay = jnp.exp(jnp.minimum(g_i - g_i[s:s + 1], 0.0))
                wgt = jnp.sum(q_i * k_i[s:s + 1] * decay, axis=1, keepdims=True)
                acc = acc + jnp.where(trow >= s, wgt, 0.0) * v_i[s:s + 1]
            o_blocks[i] = o_blocks[i] + acc

        g_last = gcum[chunk - 1:chunk, :]
        k_t = k * jnp.exp(g_last - gcum)
        decay_col = jnp.sum(jnp.where(eye, jnp.exp(g_last), 0.0), axis=1, keepdims=True)
        kv = lax.dot_general(k_t.astype(MXU_DTYPE), v.astype(MXU_DTYPE), (((0,), (0,)), ((), ())),
                             preferred_element_type=F32)
        s_scr[hh] = decay_col * state + kv

        o = jnp.concatenate(o_blocks, axis=0)
        o = o * lax.rsqrt(jnp.mean(o * o, axis=-1, keepdims=True) + RMS_EPS) * og
        gate = g_ref[r0:r0 + chunk, cols]
        o_ref[r0:r0 + chunk, cols] = o * (gate * jax.nn.sigmoid(gate))

    @pl.when(tblk == pl.num_programs(2) - 1)
    def _():
        sout_ref[0] = s_scr[...]


def hgrn_scan(pr, s0, o_gain, lb, n_seq, t_rows, valid):
    tb = min(HG_TIME, t_rows)
    chunk = min(HG_CHUNK, tb)
    sub = min(HG_SUB, chunk)
    per = t_rows // tb
    hps = HG_HEADS if tb < HG_TIME else 1
    hblocks = HG_HEADS // hps
    use_s0 = s0 is not None
    if not use_s0:
        s0 = jnp.zeros((1, hps, HG_DK, HG_DV), F32)

    def part(idx):
        return pl.BlockSpec((tb, hps * HG_DK), lambda n, h, t: (n * per + t, idx * hblocks + h))

    head_vec = pl.BlockSpec((1, hps * HG_DK), lambda n, h, t: (0, h))
    state_spec = pl.BlockSpec((1, hps, HG_DK, HG_DV), lambda n, h, t: (n, h, 0, 0))
    s0_spec = state_spec if use_s0 else pl.BlockSpec((1, hps, HG_DK, HG_DV), lambda n, h, t: (0, 0, 0, 0))
    return pl.pallas_call(
        functools.partial(_hgrn_body, tb=tb, chunk=chunk, sub=sub, valid=valid, use_s0=use_s0, hps=hps),
        grid=(n_seq, hblocks, per),
        in_specs=[part(0), part(1), part(2), part(3), head_vec, pl.BlockSpec((1, HG_DV), lambda n, h, t: (0, 0)), s0_spec],
        out_specs=[pl.BlockSpec((tb, hps * HG_DV), lambda n, h, t: (n * per + t, h)), state_spec],
        out_shape=[jax.ShapeDtypeStruct((n_seq * t_rows, D_MODEL), F32),
                   jax.ShapeDtypeStruct((n_seq, HG_HEADS, HG_DK, HG_DV), F32)],
        scratch_shapes=[pltpu.VMEM((hps, HG_DK, HG_DV), F32)],
        compiler_params=pltpu.CompilerParams(dimension_semantics=("parallel", "parallel", "arbitrary"),
                                             vmem_limit_bytes=VMEM_LIMIT),
        name="hgrn_scan",
    )(pr, pr, pr, pr, lb.reshape(1, -1), o_gain.reshape(1, -1), s0)


def kernel(x_prompt, x_sample, cache_nsa, state_nsa_win, state_s5, state_hgrn, page_table, p_prompt, p_sample, norm_gain, ffn_w_in, ffn_w_out, ple_w_gate, ple_w_proj, nsa_w_in, nsa_w_out, nsa_qk_gain, nsa_cmp_pe, nsa_cmp_w, s5_lam_re, s5_lam_im, s5_log_dt, s5_b_re, s5_b_im, s5_c_re, s5_c_im, s5_d, s5_w_glu, hg_w_in, hg_w_out, hg_o_gain, hg_lb_raw):
    B, T, _ = x_prompt.shape
    Bs, Ts, _ = x_sample.shape
    rp = B * T
    rs = Bs * Ts
    n_pages = page_table.shape[1]
    past_len = n_pages * PAGE_SIZE
    n_phys = cache_nsa.shape[1]
    wlen = state_nsa_win.shape[2]
    assert T % Q_TILE == 0 and T % PAGE_SIZE == 0 and T % min(KV_TILE, T) == 0 and Ts <= TS_PAD
    lb_sm = jax.nn.softmax(hg_lb_raw.astype(F32), axis=0)
    lower_bounds = jnp.cumsum(lb_sm, axis=0) - lb_sm[0]

    x = jnp.concatenate([x_prompt.reshape(rp, D_MODEL), x_sample.reshape(rs, D_MODEL)], axis=0)
    p_all = jnp.concatenate([p_prompt.reshape(DEPTH, rp, PLE_DIM), p_sample.reshape(DEPTH, rs, PLE_DIM)], axis=1)
    pos = jnp.concatenate([jnp.tile(jnp.arange(T), B), jnp.tile(past_len + jnp.arange(Ts), Bs)])
    cache_pages = cache_nsa.transpose(0, 1, 3, 4, 5, 2).reshape(-1, 4 * NSA_KV_W, PAGE_SIZE)
    win_state = state_nsa_win.transpose(0, 1, 3, 4, 5, 2).reshape(-1, 2 * NSA_KV_W, wlen)
    prompt_table = (jnp.arange(B, dtype=jnp.int32)[:, None] * (T // PAGE_SIZE)
                    + jnp.arange(T // PAGE_SIZE, dtype=jnp.int32)[None, :])
    out_perm = _head_perm()

    def split(a):
        return a[:rp].reshape(B, T, -1), a[rp:].reshape(Bs, Ts, -1)

    def join(a, b):
        return jnp.concatenate([a.reshape(rp, -1), b.reshape(rs, -1)], axis=0)

    def sample_pad(a):
        return jnp.pad(a[rp:].reshape(Bs, Ts, -1), ((0, 0), (0, TS_PAD - Ts), (0, 0)))

    outs_p = {0: [], 1: [], 2: [], 3: []}
    outs_s = {0: [], 1: [], 2: [], 3: []}
    for i in range(DEPTH):
        kind = LAYER_KIND[i]
        j = LAYER_SLOT[i]
        g = norm_gain[i]
        x, xn = ffn_step(x, g[0], g[1], ffn_w_in[i, 0], ffn_w_out[i, 0])
        if kind == 0:
            qn, qr, rows_new, win_new, gates = nsa_proj(xn, nsa_w_in[j], nsa_qk_gain[j], pos)
            cmp_args = (nsa_cmp_pe[j], nsa_cmp_w[j], nsa_qk_gain[j, 1])
            kc_p, vc_p = nsa_compress(rows_new.reshape(-1, PAGE_SIZE, 4 * NSA_KV_W), prompt_table, *cmp_args, False)
            o_p = nsa_attn_prompt(qn, qr, gates, kc_p, vc_p, rows_new, win_new, B, T)
            table = page_table.astype(jnp.int32) + j * n_phys
            kc_s, vc_s = nsa_compress(cache_pages, table, *cmp_args, True)
            o_s, w_s = nsa_attn_sample(table, cache_pages, sample_pad(qn), sample_pad(qr), sample_pad(gates), kc_s, vc_s,
                                       sample_pad(rows_new), win_state, sample_pad(win_new), j * Bs, past_len, Ts)
            r_p, r_s = split(rows_new)
            outs_p[0].append(r_p.reshape(B, T, 4, NSA_KV_HEADS, HEAD_DIM))
            outs_s[0].append(r_s.reshape(Bs, Ts, 4, NSA_KV_HEADS, HEAD_DIM))
            buf = min(WINDOW, T)
            outs_p[1].append(win_new[:rp].reshape(B, T, 2, NSA_KV_HEADS, HEAD_DIM)[:, T - buf:])
            outs_s[1].append(w_s.reshape(Bs, 2, NSA_KV_HEADS, HEAD_DIM, wlen).transpose(0, 4, 1, 2, 3))
            o_all = jnp.concatenate([o_p, o_s[:, :Ts].reshape(rs, NSA_Q_W).astype(o_p.dtype)], axis=0)
            x = resid_proj(x, o_all, nsa_w_out[j][out_perm])
        elif kind == 1:
            u_p, u_s = split(xn)
            disc = s5_discretize(s5_lam_re[j], s5_lam_im[j], s5_log_dt[j], s5_b_re[j], s5_b_im[j])
            y_p, h_p = s5_scan(u_p, None, *disc, s5_c_re[j], s5_c_im[j])
            y_s, h_s = s5_scan(u_s, state_s5[j], *disc, s5_c_re[j], s5_c_im[j])
            outs_p[2].append(h_p); outs_s[2].append(h_s)
            x = s5_out(x, join(y_p, y_s), xn, s5_d[j], s5_w_glu[j])
        else:
            pr = proj(xn, hg_w_in[j])
            pr_s = jnp.pad(pr[rp:].reshape(Bs, Ts, -1), ((0, 0), (0, TS_PAD - Ts), (0, 0))).reshape(Bs * TS_PAD, -1)
            o_p, s_p = hgrn_scan(pr, None, hg_o_gain[j], lower_bounds[i], B, T, T)
            o_s, s_s = hgrn_scan(pr_s, state_hgrn[j], hg_o_gain[j], lower_bounds[i], Bs, TS_PAD, Ts)
            outs_p[3].append(s_p); outs_s[3].append(s_s)
            o_s = o_s.reshape(Bs, TS_PAD, D_MODEL)[:, :Ts].reshape(rs, D_MODEL)
            x = resid_proj(x, jnp.concatenate([o_p, o_s], axis=0), hg_w_out[j])
        x = ffn_ple_step(x, g[2], g[3], ffn_w_in[i, 1], ffn_w_out[i, 1], p_all[i], ple_w_gate[i], ple_w_proj[i])

    y_p, y_s = split(x)
    return (y_p, y_s,
            jnp.stack(outs_p[0]), jnp.stack(outs_p[1]), jnp.stack(outs_p[2]), jnp.stack(outs_p[3]),
            jnp.stack(outs_s[0]), jnp.stack(outs_s[1]), jnp.stack(outs_s[2]), jnp.stack(outs_s[3]))
```

```python
import functools
import itertools

import numpy as np
import jax
import jax.numpy as jnp
from jax import lax
from jax.experimental import pallas as pl
from jax.experimental.pallas import tpu as pltpu

F32 = jnp.float32
MXU_DTYPE = jnp.bfloat16

D_MODEL = 1024
DEPTH = 4
PAGE_SIZE = 128
D_FF = 2816
PLE_DIM = 256
RMS_EPS = 1e-6
LAYER_KIND = (0, 1, 2, 0)
LAYER_SLOT = (0, 0, 0, 1)

NSA_HEADS = 16
NSA_KV_HEADS = 4
HEAD_DIM = 64
NSA_HPG = NSA_HEADS // NSA_KV_HEADS
NSA_KV_W = NSA_KV_HEADS * HEAD_DIM
NSA_Q_W = NSA_HEADS * HEAD_DIM
NSA_IN = NSA_Q_W + 6 * NSA_KV_W + 3 * NSA_HEADS
CMP_BLOCK = 32
CMP_STRIDE = 16
SLC_BLOCK = 64
SLC_TOPN = 16
WINDOW = 512
FORCE_BONUS = 1e4
NEG_INF = -1e30
ROPE_THETA = 500000.0
ROPE_DIMS = HEAD_DIM // 4
ATT_SCALE = HEAD_DIM ** -0.5

S5_GROUP_CH = 16
S5_GROUPS = D_MODEL // S5_GROUP_CH
S5_STATE = 64

HG_DK = 128
HG_HEADS = D_MODEL // HG_DK
HG_DV = D_MODEL // HG_HEADS
HG_CHUNK = 64

V7X_VMEM_BYTES = 64 * 1024 * 1024
VMEM_LIMIT = V7X_VMEM_BYTES - 8 * 1024 * 1024
LANES = 128
ROW_TILE = 512
FF_CHUNK = 256
Q_TILE = 128
KV_TILE = 512
TS_PAD = 8
SOFTMAX_M0 = -1e29


def _resident(shape):
    return pl.BlockSpec(shape, lambda *_: (0,) * len(shape), pipeline_mode=pl.Buffered(1))


def _row_tile(rows):
    return max(t for t in range(8, ROW_TILE + 1, 8) if rows % t == 0)


def _rows(tm, width):
    return pl.BlockSpec((tm, width), lambda i: (i, 0))


def _params(n_axes=1):
    return pltpu.CompilerParams(dimension_semantics=("parallel",) * n_axes, vmem_limit_bytes=VMEM_LIMIT)


def _rms(x, g):
    return x * lax.rsqrt(jnp.mean(x * x, axis=-1, keepdims=True) + RMS_EPS) * g


def _mm(a, b):
    return jnp.dot(a.astype(MXU_DTYPE), b.astype(MXU_DTYPE), preferred_element_type=F32)


def _mm_nt(a, b):
    return lax.dot_general(a.astype(MXU_DTYPE), b.astype(MXU_DTYPE), (((1,), (1,)), ((), ())),
                           preferred_element_type=F32)


def _mm_split(a, b):
    hi = a.astype(MXU_DTYPE)
    lo = (a - hi.astype(F32)).astype(MXU_DTYPE)
    return (jnp.dot(hi, b, preferred_element_type=F32) + jnp.dot(lo, b, preferred_element_type=F32))


def _ffn_body(x_ref, g_ref, gn_ref, win_ref, wout_ref, o_ref, on_ref, h_ref):
    x = x_ref[...]
    xb = _rms(x, g_ref[...]).astype(MXU_DTYPE)
    for c in range(D_FF // FF_CHUNK):
        lo = c * FF_CHUNK
        a = _mm(xb, win_ref[:, lo:lo + FF_CHUNK])
        b = _mm(xb, win_ref[:, D_FF + lo:D_FF + lo + FF_CHUNK])
        h_ref[:, lo:lo + FF_CHUNK] = (a * jax.nn.sigmoid(a) * b).astype(MXU_DTYPE)
    y = x + 0.5 * _mm(h_ref[...], wout_ref[...])
    o_ref[...] = y
    on_ref[...] = _rms(y, gn_ref[...])


def ffn_step(x, g, g_next, w_in, w_out):
    rows = x.shape[0]
    tm = _row_tile(rows)
    out = jax.ShapeDtypeStruct((rows, D_MODEL), F32)
    return pl.pallas_call(
        _ffn_body,
        grid=(rows // tm,),
        in_specs=[_rows(tm, D_MODEL), _resident((1, D_MODEL)), _resident((1, D_MODEL)),
                  _resident((D_MODEL, 2 * D_FF)), _resident((D_FF, D_MODEL))],
        out_specs=[_rows(tm, D_MODEL), _rows(tm, D_MODEL)],
        out_shape=[out, out],
        scratch_shapes=[pltpu.VMEM((tm, D_FF), MXU_DTYPE)],
        compiler_params=_params(),
        name="ffn_step",
    )(x, g.reshape(1, -1), g_next.reshape(1, -1), w_in.astype(MXU_DTYPE), w_out.astype(MXU_DTYPE))


def _ffn_ple_body(x_ref, g_ref, gn_ref, win_ref, wout_ref, p_ref, wg_ref, wp_ref, o_ref, h_ref):
    x = x_ref[...]
    xb = _rms(x, g_ref[...]).astype(MXU_DTYPE)
    for c in range(D_FF // FF_CHUNK):
        lo = c * FF_CHUNK
        a = _mm(xb, win_ref[:, lo:lo + FF_CHUNK])
        b = _mm(xb, win_ref[:, D_FF + lo:D_FF + lo + FF_CHUNK])
        h_ref[:, lo:lo + FF_CHUNK] = (a * jax.nn.sigmoid(a) * b).astype(MXU_DTYPE)
    y = x + 0.5 * _mm(h_ref[...], wout_ref[...])
    gate = jax.nn.sigmoid(_mm(_rms(y, gn_ref[...]), wg_ref[...]))
    o_ref[...] = y + gate * _mm(p_ref[...], wp_ref[...])


def ffn_ple_step(x, g, g_ple, w_in, w_out, p, w_gate, w_proj):
    rows = x.shape[0]
    tm = _row_tile(rows)
    return pl.pallas_call(
        _ffn_ple_body,
        grid=(rows // tm,),
        in_specs=[_rows(tm, D_MODEL), _resident((1, D_MODEL)), _resident((1, D_MODEL)),
                  _resident((D_MODEL, 2 * D_FF)), _resident((D_FF, D_MODEL)), _rows(tm, PLE_DIM),
                  _resident((D_MODEL, D_MODEL)), _resident((PLE_DIM, D_MODEL))],
        out_specs=_rows(tm, D_MODEL),
        out_shape=jax.ShapeDtypeStruct((rows, D_MODEL), F32),
        scratch_shapes=[pltpu.VMEM((tm, D_FF), MXU_DTYPE)],
        compiler_params=_params(),
        name="ffn_ple_step",
    )(x, g.reshape(1, -1), g_ple.reshape(1, -1), w_in.astype(MXU_DTYPE), w_out.astype(MXU_DTYPE), p,
      w_gate.astype(MXU_DTYPE), w_proj.astype(MXU_DTYPE))


def _proj_body(a_ref, w_ref, o_ref):
    o_ref[...] = _mm(a_ref[...], w_ref[...])


def proj(a, w):
    rows, k = a.shape
    n = w.shape[1]
    tm = _row_tile(rows)
    return pl.pallas_call(
        _proj_body,
        grid=(rows // tm,),
        in_specs=[_rows(tm, k), _resident((k, n))],
        out_specs=_rows(tm, n),
        out_shape=jax.ShapeDtypeStruct((rows, n), F32),
        compiler_params=_params(),
        name="proj",
    )(a, w.astype(MXU_DTYPE))


def _resid_body(x_ref, a_ref, w_ref, o_ref):
    o_ref[...] = x_ref[...] + _mm(a_ref[...], w_ref[...])


def resid_proj(x, a, w):
    rows, k = a.shape
    tm = _row_tile(rows)
    return pl.pallas_call(
        _resid_body,
        grid=(rows // tm,),
        in_specs=[_rows(tm, D_MODEL), _rows(tm, k), _resident((k, D_MODEL))],
        out_specs=_rows(tm, D_MODEL),
        out_shape=jax.ShapeDtypeStruct((rows, D_MODEL), F32),
        compiler_params=_params(),
        name="resid_proj",
    )(x, a, w.astype(MXU_DTYPE))


def _head_perm():
    idx = np.arange(NSA_Q_W).reshape(NSA_KV_HEADS, NSA_HPG, HEAD_DIM)
    return idx.transpose(1, 0, 2).reshape(-1)


def _gate_expand():
    x = np.zeros((LANES, 3 * NSA_Q_W), np.float32)
    for b in range(3):
        for g in range(NSA_KV_HEADS):
            for j in range(NSA_HPG):
                h = g * NSA_HPG + j
                c0 = b * NSA_Q_W + (j * NSA_KV_HEADS + g) * HEAD_DIM
                x[b * NSA_HEADS + h, c0:c0 + HEAD_DIM] = 1.0
    return x


def _overlap_matrix(nb, length):
    n_sel = -(-length // SLC_BLOCK)
    c0 = np.arange(nb)[:, None] * CMP_STRIDE
    s0 = np.arange(LANES)[None, :] * SLC_BLOCK
    ov = np.clip(np.minimum(c0 + CMP_BLOCK, s0 + SLC_BLOCK) - np.maximum(c0, s0), 0, None) / CMP_BLOCK
    ov = np.where(np.arange(LANES)[None, :] < n_sel, ov, 0.0)
    return ov.astype(np.float32)


def _rope_tables(pos):
    half = ROPE_DIMS // 2
    inv = ROPE_THETA ** (-jnp.arange(half, dtype=F32) / half)
    ang = pos.astype(F32)[:, None] * inv[None, :]
    cos, sin = jnp.cos(ang), jnp.sin(ang)
    ones = jnp.ones((pos.shape[0], HEAD_DIM - ROPE_DIMS), F32)
    zeros = jnp.zeros((pos.shape[0], HEAD_DIM - ROPE_DIMS), F32)
    zh = jnp.zeros_like(sin)
    c = jnp.concatenate([cos, cos, ones], axis=1)
    sa = jnp.concatenate([-sin, zh, zeros], axis=1)
    sb = jnp.concatenate([zh, sin, zeros], axis=1)
    rep = LANES // HEAD_DIM
    return jnp.tile(c, (1, rep)), jnp.tile(sa, (1, rep)), jnp.tile(sb, (1, rep))


def _nsa_proj_body(xn_ref, wq_ref, wkv_ref, wgl_ref, b64_ref, gq_ref, gk_ref, c_ref, sa_ref, sb_ref,
                   qn_ref, qr_ref, rows_ref, win_ref, gates_ref):
    xb = xn_ref[...].astype(MXU_DTYPE)
    c, sa, sb = c_ref[...], sa_ref[...], sb_ref[...]

    def head_norm(v, gain):
        w = v.shape[1]
        ms = _mm_split(v * v, b64_ref[:w, :w])
        return v * lax.rsqrt(ms + RMS_EPS) * gain

    def rope(v):
        w = v.shape[1]
        rep = w // LANES
        ct, sat, sbt = (jnp.concatenate([t] * rep, axis=1) for t in (c, sa, sb))
        return v * ct + pltpu.roll(v, w - ROPE_DIMS // 2, 1) * sat + pltpu.roll(v, ROPE_DIMS // 2, 1) * sbt

    qn = head_norm(_mm(xb, wq_ref[...]), gq_ref[...])
    qn_ref[...] = qn.astype(qn_ref.dtype)
    qr_ref[...] = rope(qn).astype(qr_ref.dtype)
    kv = _mm(xb, wkv_ref[...])
    w = NSA_KV_W
    rows_ref[:, 0:2 * w] = kv[:, 0:2 * w]
    rows_ref[:, 2 * w:3 * w] = rope(head_norm(kv[:, 2 * w:3 * w], gk_ref[0:1, :]))
    rows_ref[:, 3 * w:4 * w] = kv[:, 3 * w:4 * w]
    win_ref[:, 0:w] = rope(head_norm(kv[:, 4 * w:5 * w], gk_ref[1:2, :]))
    win_ref[:, w:2 * w] = kv[:, 5 * w:6 * w]
    gates_ref[...] = jax.nn.sigmoid(_mm(xb, wgl_ref[...]))


def nsa_proj(xn, w_in, qk_gain, pos):
    rows = xn.shape[0]
    tm = _row_tile(rows)
    kvw = 6 * NSA_KV_W
    wq = w_in[:, :NSA_Q_W][:, _head_perm()].astype(MXU_DTYPE)
    wkv = w_in[:, NSA_Q_W:NSA_Q_W + kvw].astype(MXU_DTYPE)
    wgl = jnp.pad(w_in[:, NSA_Q_W + kvw:], ((0, 0), (0, LANES - 3 * NSA_HEADS))).astype(MXU_DTYPE)
    b64 = jnp.asarray(np.kron(np.eye(NSA_HEADS), np.full((HEAD_DIM, HEAD_DIM), 1.0 / HEAD_DIM)), MXU_DTYPE)
    gq = jnp.tile(qk_gain[0], NSA_HEADS).reshape(1, -1)
    gk = jnp.stack([jnp.tile(qk_gain[2], NSA_KV_HEADS), jnp.tile(qk_gain[3], NSA_KV_HEADS)])
    c, sa, sb = _rope_tables(pos)
    return pl.pallas_call(
        _nsa_proj_body,
        grid=(rows // tm,),
        in_specs=[_rows(tm, D_MODEL), _resident((D_MODEL, NSA_Q_W)), _resident((D_MODEL, kvw)),
                  _resident((D_MODEL, LANES)), _resident((NSA_Q_W, NSA_Q_W)), _resident((1, NSA_Q_W)),
                  _resident((2, NSA_KV_W)), _rows(tm, LANES), _rows(tm, LANES), _rows(tm, LANES)],
        out_specs=[_rows(tm, NSA_Q_W), _rows(tm, NSA_Q_W), _rows(tm, 4 * NSA_KV_W), _rows(tm, 2 * NSA_KV_W),
                   _rows(tm, LANES)],
        out_shape=[jax.ShapeDtypeStruct((rows, NSA_Q_W), MXU_DTYPE), jax.ShapeDtypeStruct((rows, NSA_Q_W), MXU_DTYPE),
                   jax.ShapeDtypeStruct((rows, 4 * NSA_KV_W), F32), jax.ShapeDtypeStruct((rows, 2 * NSA_KV_W), F32),
                   jax.ShapeDtypeStruct((rows, LANES), F32)],
        compiler_params=_params(),
        name="nsa_proj",
    )(xn, wq, wkv, wgl, b64, gq, gk, c, sa, sb)


def _nsa_compress_body(pt_ref, *refs, n_pages, seqs, transposed):
    del pt_ref
    page_refs = refs[:seqs * n_pages]
    wk_ref, wv_ref, pek_ref, pev_ref, b64_ref, gk_ref, kc_ref, vc_ref, stage = refs[seqs * n_pages:]
    per_page = PAGE_SIZE // CMP_STRIDE
    nb_seq = n_pages * per_page
    nb = seqs * nb_seq
    tiles = 2 * NSA_KV_W // LANES

    for p, r in enumerate(page_refs):
        for j in range(tiles):
            if transposed:
                stage[j, p * PAGE_SIZE:(p + 1) * PAGE_SIZE, :] = r[0, j * LANES:(j + 1) * LANES, :].T
            else:
                stage[j, p * PAGE_SIZE:(p + 1) * PAGE_SIZE, :] = r[0, :, j * LANES:(j + 1) * LANES]

    def compress(kind, w_ref, pe_ref):
        per_kind = NSA_KV_W // LANES
        first = jnp.zeros((nb, NSA_KV_W), F32)
        second = jnp.zeros((nb, NSA_KV_W), F32)
        for l in range(CMP_STRIDE):
            x = jnp.concatenate([stage[kind * per_kind + j, pl.ds(l, nb, stride=CMP_STRIDE), :]
                                 for j in range(per_kind)], axis=1)
            first = first + _mm(x + pe_ref[l:l + 1, :], w_ref[l])
            second = second + _mm(x + pe_ref[CMP_STRIDE + l:CMP_STRIDE + l + 1, :], w_ref[CMP_STRIDE + l])
        return first + pltpu.roll(second, nb - 1, 0)

    kc = compress(0, wk_ref, pek_ref)
    ms = _mm_split(kc * kc, b64_ref[...])
    kc = kc * lax.rsqrt(ms + RMS_EPS) * gk_ref[...]
    vc = compress(1, wv_ref, pev_ref)
    for q in range(seqs):
        kc_ref[q] = kc[q * nb_seq:(q + 1) * nb_seq]
        vc_ref[q] = vc[q * nb_seq:(q + 1) * nb_seq]


def nsa_compress(pages, table, cmp_pe, cmp_w, k_gain, transposed):
    n_seq, n_pages = table.shape
    nb = n_pages * (PAGE_SIZE // CMP_STRIDE)
    eye = jnp.eye(NSA_KV_HEADS, dtype=F32)
    w4 = jnp.einsum('gh,klde->klgdhe', eye, cmp_w).reshape(2, CMP_BLOCK, NSA_KV_W, NSA_KV_W).astype(MXU_DTYPE)
    pe4 = jnp.tile(cmp_pe, (1, 1, NSA_KV_HEADS))
    b64 = jnp.asarray(np.kron(np.eye(NSA_KV_HEADS), np.full((HEAD_DIM, HEAD_DIM), 1.0 / HEAD_DIM)), MXU_DTYPE)
    gk = jnp.tile(k_gain, NSA_KV_HEADS).reshape(1, -1)

    page_block = (1, 2 * NSA_KV_W, PAGE_SIZE) if transposed else (1, PAGE_SIZE, 2 * NSA_KV_W)

    seqs = 2 if n_seq % 2 == 0 else 1

    def page_spec(q, p):
        return pl.BlockSpec(page_block, lambda n, pt: (pt[n * seqs + q, p], 0, 0))

    def const(shape):
        return pl.BlockSpec(shape, lambda n, pt: (0,) * len(shape), pipeline_mode=pl.Buffered(1))

    out = jax.ShapeDtypeStruct((n_seq, nb, NSA_KV_W), F32)
    out_spec = pl.BlockSpec((seqs, nb, NSA_KV_W), lambda n, pt: (n, 0, 0))
    grid_spec = pltpu.PrefetchScalarGridSpec(
        num_scalar_prefetch=1, grid=(n_seq // seqs,),
        in_specs=[page_spec(q, p) for q in range(seqs) for p in range(n_pages)] + [
            const((CMP_BLOCK, NSA_KV_W, NSA_KV_W)), const((CMP_BLOCK, NSA_KV_W, NSA_KV_W)),
            const((CMP_BLOCK, NSA_KV_W)), const((CMP_BLOCK, NSA_KV_W)), const((NSA_KV_W, NSA_KV_W)),
            const((1, NSA_KV_W))],
        out_specs=[out_spec, out_spec],
        scratch_shapes=[pltpu.VMEM((2 * NSA_KV_W // LANES, seqs * n_pages * PAGE_SIZE, LANES), F32)])
    return pl.pallas_call(
        functools.partial(_nsa_compress_body, n_pages=n_pages, seqs=seqs, transposed=transposed),
        grid_spec=grid_spec, out_shape=[out, out], compiler_params=_params(), name="nsa_compress",
    )(table, *([pages] * (seqs * n_pages)), w4[0], w4[1], pe4[0], pe4[1], b64, gk)


def _group_masks():
    lane = lax.broadcasted_iota(jnp.int32, (1, NSA_KV_W), 1)
    return [(lane // HEAD_DIM) == g for g in range(NSA_KV_HEADS)]


def _blockdiag(q, bm):
    zero = jnp.zeros((), q.dtype)
    return jnp.concatenate([jnp.where(bm[g], q[:, NSA_KV_W * j:NSA_KV_W * (j + 1)], zero)
                            for g in range(NSA_KV_HEADS) for j in range(NSA_HPG)], axis=0)


def _extract(obd, bm, tq):
    outs = []
    for j in range(NSA_HPG):
        z = jnp.zeros((tq, NSA_KV_W), F32)
        for g in range(NSA_KV_HEADS):
            r0 = (g * NSA_HPG + j) * tq
            z = z + jnp.where(bm[g], obd[r0:r0 + tq], 0.0)
        outs.append(z)
    return jnp.concatenate(outs, axis=1)


def _per_head_rows(a, tq):
    k = a.shape[1]
    a4 = jnp.broadcast_to(a.reshape(NSA_KV_HEADS, 1, tq, k), (NSA_KV_HEADS, NSA_HPG, tq, k))
    return a4.reshape(NSA_HEADS * tq, k)


def _topk_mask(score, n_sel, tq):
    if tq == LANES:
        nsp = -(-n_sel // 8) * 8
        sub = lax.broadcasted_iota(jnp.int32, (nsp, 1), 0)
        outs = []
        for g in range(NSA_KV_HEADS):
            st = score[g * tq:(g + 1) * tq].T[:nsp]
            rank = jnp.zeros(st.shape, F32)
            for s2 in range(n_sel):
                row = st[s2:s2 + 1, :]
                rank = rank + jnp.where(row > st, 1.0, jnp.where(row == st, jnp.where(sub > s2, 1.0, 0.0), 0.0))
            sel_t = jnp.where(rank < SLC_TOPN, 1.0, 0.0)
            sel_t = jnp.concatenate([sel_t, jnp.zeros((LANES - nsp, tq), F32)], axis=0)
            outs.append(sel_t.T)
        return jnp.concatenate(outs, axis=0)
    lane = lax.broadcasted_iota(jnp.int32, (1, LANES), 1)
    rank = jnp.zeros(score.shape, F32)
    for s2 in range(n_sel):
        col = score[:, s2:s2 + 1]
        rank = rank + jnp.where(col > score, 1.0, jnp.where(col == score, jnp.where(lane > s2, 1.0, 0.0), 0.0))
    return jnp.where(rank < SLC_TOPN, 1.0, 0.0)


def _cmp_and_select(qbd_n, kc, vc, ov, pos_base, tq, n_sel):
    nb = kc.shape[0]
    r = lax.broadcasted_iota(jnp.int32, (NSA_HEADS * tq, 1), 0)
    qpos = pos_base + (r & (tq - 1))
    blk_end = lax.broadcasted_iota(jnp.int32, (1, nb), 1) * CMP_STRIDE + (CMP_BLOCK - 1)
    visible = blk_end <= qpos
    s = jnp.where(visible, _mm_nt(qbd_n, kc) * ATT_SCALE, NEG_INF)
    e = jnp.exp(s - jnp.max(s, axis=-1, keepdims=True))
    p = jnp.where(visible, e / jnp.sum(e, axis=-1, keepdims=True), 0.0)
    o = _mm(p, vc)
    psum = jnp.sum(p.reshape(NSA_KV_HEADS, NSA_HPG, tq, nb), axis=1).reshape(NSA_KV_HEADS * tq, nb)
    imp = _mm_split(psum, ov.astype(MXU_DTYPE))
    r4 = lax.broadcasted_iota(jnp.int32, (NSA_KV_HEADS * tq, 1), 0)
    qblk = (pos_base + (r4 & (tq - 1))) // SLC_BLOCK
    sidx = lax.broadcasted_iota(jnp.int32, (1, LANES), 1)
    bonus = jnp.where(sidx == 0, FORCE_BONUS, jnp.where(sidx == qblk, FORCE_BONUS,
                                                        jnp.where(sidx == qblk - 1, FORCE_BONUS, 0.0)))
    score = jnp.where(sidx <= qblk, imp + bonus, NEG_INF)
    return o, _topk_mask(score, n_sel, tq), qpos


def _gated_sum(gates, gx, o_cmp, o_slc, o_win, bm, tq):
    gf = _mm_split(gates, gx)
    return (gf[:, 0:NSA_Q_W] * _extract(o_cmp, bm, tq) + gf[:, NSA_Q_W:2 * NSA_Q_W] * _extract(o_slc, bm, tq)
            + gf[:, 2 * NSA_Q_W:] * _extract(o_win, bm, tq))


def _nsa_prompt_body(qn_ref, qr_ref, gates_ref, kc_ref, vc_ref, kv_ref, win_ref, ov_ref, gx_ref, o_ref,
                     *, seq, n_sel, kt, wk):
    tq = Q_TILE
    t0 = pl.program_id(1) * tq
    bm = _group_masks()
    w = NSA_KV_W
    o_cmp, sel, _ = _cmp_and_select(_blockdiag(qn_ref[...], bm), kc_ref[0], vc_ref[0], ov_ref[...], t0, tq, n_sel)
    qbd = _blockdiag(qr_ref[...], bm) * ATT_SCALE
    sel_b = sel.astype(MXU_DTYPE)
    sidx = lax.broadcasted_iota(jnp.int32, (LANES, 1), 0)
    qpos_t = t0 + lax.broadcasted_iota(jnp.int32, (tq, 1), 0)
    qpos_gt = t0 + (lax.broadcasted_iota(jnp.int32, (NSA_KV_HEADS * tq, 1), 0) & (tq - 1))

    def kv_tile(jt, carry):
        m, l, acc = carry
        k0 = pl.multiple_of(jt * kt, kt)
        kpos = k0 + lax.broadcasted_iota(jnp.int32, (1, kt), 1)
        expand = jnp.where((kpos // SLC_BLOCK) == sidx, 1.0, 0.0).astype(MXU_DTYPE)
        bias = (jnp.dot(sel_b, expand, preferred_element_type=F32) - 1.0) * (-NEG_INF)
        bias = jnp.where(kpos <= qpos_gt, bias, NEG_INF)
        s = _mm_nt(qbd, kv_ref[pl.ds(k0, kt), 0:w]) + _per_head_rows(bias, tq)
        m_new = jnp.maximum(m, jnp.max(s, axis=-1, keepdims=True))
        alpha = jnp.exp(m - m_new)
        e = jnp.exp(s - m_new)
        l = alpha * l + jnp.sum(e, axis=-1, keepdims=True)
        acc = alpha * acc + _mm(e, kv_ref[pl.ds(k0, kt), w:2 * w])
        return m_new, l, acc

    rows = NSA_HEADS * tq
    n_tiles = (t0 + tq + kt - 1) // kt
    m, l, acc = lax.fori_loop(0, n_tiles, kv_tile, (jnp.full((rows, 1), SOFTMAX_M0, F32), jnp.zeros((rows, 1), F32),
                                                    jnp.zeros((rows, w), F32)))
    o_slc = acc * (1.0 / l)

    start = pl.multiple_of(jnp.maximum(t0 + tq - wk, 0), tq)
    kpos = start + lax.broadcasted_iota(jnp.int32, (1, wk), 1)
    bias = jnp.where(kpos <= qpos_t, jnp.where(kpos > qpos_t - WINDOW, 0.0, NEG_INF), NEG_INF)
    bias = jnp.broadcast_to(bias.reshape(1, tq, wk), (NSA_HEADS, tq, wk)).reshape(rows, wk)
    s = _mm_nt(qbd, win_ref[pl.ds(start, wk), 0:w]) + bias
    e = jnp.exp(s - jnp.max(s, axis=-1, keepdims=True))
    o_win = _mm(e, win_ref[pl.ds(start, wk), w:2 * w]) * (1.0 / jnp.sum(e, axis=-1, keepdims=True))
    o_ref[...] = _gated_sum(gates_ref[...], gx_ref[...], o_cmp, o_slc, o_win, bm, tq).astype(o_ref.dtype)


def nsa_attn_prompt(qn, qr, gates, kc, vc, rows_new, win_new, n_seq, seq):
    tq = Q_TILE
    nb = kc.shape[1]
    n_sel = -(-seq // SLC_BLOCK)
    kt = min(KV_TILE, seq)
    wk = min(WINDOW + tq, seq)
    ov = jnp.asarray(_overlap_matrix(nb, seq))
    gx = jnp.asarray(_gate_expand(), MXU_DTYPE)
    per = seq // tq

    def qrows(width):
        return pl.BlockSpec((tq, width), lambda n, t: (n * per + t, 0))

    def per_seq(shape, lane_block=0):
        return pl.BlockSpec(shape, lambda n, t: (n,) + (0,) * (len(shape) - 2) + (lane_block,))

    return pl.pallas_call(
        functools.partial(_nsa_prompt_body, seq=seq, n_sel=n_sel, kt=kt, wk=wk),
        grid=(n_seq, per),
        in_specs=[qrows(NSA_Q_W), qrows(NSA_Q_W), qrows(LANES), per_seq((1, nb, NSA_KV_W)), per_seq((1, nb, NSA_KV_W)),
                  per_seq((seq, 2 * NSA_KV_W), 1), per_seq((seq, 2 * NSA_KV_W)),
                  _resident((nb, LANES)), _resident((LANES, 3 * NSA_Q_W))],
        out_specs=qrows(NSA_Q_W),
        out_shape=jax.ShapeDtypeStruct((n_seq * seq, NSA_Q_W), MXU_DTYPE),
        compiler_params=_params(2),
        name="nsa_attn_prompt",
    )(qn, qr, gates, kc, vc, rows_new, win_new, ov, gx)


def _nsa_sample_body(pt_ref, *refs, n_pages, past_len, ts, n_sel, has_prev):
    del pt_ref
    page_refs = refs[:n_pages]
    (qn_ref, qr_ref, gates_ref, kc_ref, vc_ref, rnew_ref, wold_ref, wnew_ref, ov_ref, gx_ref, ex_ref,
     o_ref, wout_ref) = refs[n_pages:n_pages + 11] + refs[n_pages + 11 + int(has_prev):]
    tq = TS_PAD
    w = NSA_KV_W
    bm = _group_masks()
    o_cmp, sel, qpos = _cmp_and_select(_blockdiag(qn_ref[0].astype(F32), bm), kc_ref[0], vc_ref[0], ov_ref[...],
                                       past_len, tq, n_sel)
    qbd = _blockdiag(qr_ref[0].astype(F32), bm).astype(MXU_DTYPE)
    pad = jnp.zeros((PAGE_SIZE - tq, w), F32)

    bias = (jnp.dot(sel.astype(MXU_DTYPE), ex_ref[...], preferred_element_type=F32) - 1.0) * (-NEG_INF)
    k_new = jnp.concatenate([rnew_ref[0][:, 0:w], pad], axis=0)
    v_new = jnp.concatenate([rnew_ref[0][:, w:2 * w], pad], axis=0)
    s = jnp.concatenate([_mm(qbd, r[0, 0:w, :]) for r in page_refs] + [_mm_nt(qbd, k_new)], axis=1)
    s = s * ATT_SCALE + _per_head_rows(bias, tq)
    kpos = lax.broadcasted_iota(jnp.int32, (1, (n_pages + 1) * PAGE_SIZE), 1)
    s = jnp.where(kpos <= qpos, s, NEG_INF)
    e = jnp.exp(s - jnp.max(s, axis=-1, keepdims=True))
    acc = _mm(e[:, n_pages * PAGE_SIZE:], v_new)
    for p, r in enumerate(page_refs):
        acc = acc + _mm_nt(e[:, p * PAGE_SIZE:(p + 1) * PAGE_SIZE], r[0, w:2 * w, :])
    o_slc = acc * (1.0 / jnp.sum(e, axis=-1, keepdims=True))

    wlen = wold_ref.shape[2]
    kw_new = jnp.concatenate([wnew_ref[0][:, 0:w], pad], axis=0)
    vw_new = jnp.concatenate([wnew_ref[0][:, w:2 * w], pad], axis=0)
    s = jnp.concatenate([_mm(qbd, wold_ref[0, 0:w, :]), _mm_nt(qbd, kw_new)], axis=1) * ATT_SCALE
    kpos = (past_len - wlen) + lax.broadcasted_iota(jnp.int32, (1, wlen + PAGE_SIZE), 1)
    s = jnp.where(kpos <= qpos, s, NEG_INF)
    s = jnp.where(kpos > qpos - WINDOW, s, NEG_INF)
    e = jnp.exp(s - jnp.max(s, axis=-1, keepdims=True))
    acc = _mm_nt(e[:, 0:wlen], wold_ref[0, w:2 * w, :]) + _mm(e[:, wlen:], vw_new)
    o_win = acc * (1.0 / jnp.sum(e, axis=-1, keepdims=True))

    o_ref[0] = _gated_sum(gates_ref[0], gx_ref[...], o_cmp, o_slc, o_win, bm, tq)

    shifted = pltpu.roll(wold_ref[0], wlen - ts, 1)
    new_rows = jnp.concatenate([wnew_ref[0], jnp.zeros((LANES - tq, 2 * w), F32)], axis=0)
    new_t = jnp.concatenate([new_rows[:, j * LANES:(j + 1) * LANES].T for j in range(2 * w // LANES)], axis=0)
    new_t = pltpu.roll(new_t, LANES - ts, 1)
    lane = lax.broadcasted_iota(jnp.int32, (1, LANES), 1)
    wout_ref[0, :, 0:wlen - LANES] = shifted[:, 0:wlen - LANES]
    wout_ref[0, :, wlen - LANES:wlen] = jnp.where(lane >= LANES - ts, new_t, shifted[:, wlen - LANES:wlen])


def nsa_attn_sample(table, pages, qn, qr, gates, kc, vc, rows_new, win_old, win_new, win_base, past_len, ts,
                    win_prev):
    n_seq, n_pages = table.shape
    nb = kc.shape[1]
    wlen = win_old.shape[2]
    length = past_len + ts
    n_sel = -(-length // SLC_BLOCK)
    n_keys = (n_pages + 1) * PAGE_SIZE
    ov = jnp.asarray(_overlap_matrix(nb, length))
    gx = jnp.asarray(_gate_expand(), MXU_DTYPE)
    ex = jnp.asarray((np.arange(n_keys)[None, :] // SLC_BLOCK == np.arange(LANES)[:, None]).astype(np.float32), MXU_DTYPE)

    def page_spec(p):
        return pl.BlockSpec((1, 2 * NSA_KV_W, PAGE_SIZE), lambda n, pt: (pt[n, p], 1, 0))

    def per_seq(shape, lane_block=0, base=0):
        return pl.BlockSpec(shape, lambda n, pt: (base + n,) + (0,) * (len(shape) - 2) + (lane_block,))

    def const(shape):
        return pl.BlockSpec(shape, lambda n, pt: (0,) * len(shape), pipeline_mode=pl.Buffered(1))

    in_specs = [page_spec(p) for p in range(n_pages)] + [
        per_seq((1, TS_PAD, NSA_Q_W)), per_seq((1, TS_PAD, NSA_Q_W)), per_seq((1, TS_PAD, LANES)),
        per_seq((1, nb, NSA_KV_W)), per_seq((1, nb, NSA_KV_W)), per_seq((1, TS_PAD, 2 * NSA_KV_W), 1),
        per_seq((1, 2 * NSA_KV_W, wlen), 0, win_base), per_seq((1, TS_PAD, 2 * NSA_KV_W)),
        const((nb, LANES)), const((LANES, 3 * NSA_Q_W)), const((LANES, n_keys))]
    operands = [table, *([pages] * n_pages), qn, qr, gates, kc, vc, rows_new, win_old, win_new, ov, gx, ex]
    aliases = {}
    if win_prev is not None:
        in_specs.append(pl.BlockSpec(memory_space=pl.ANY))
        aliases = {len(operands): 1}
        operands.append(win_prev)
    grid_spec = pltpu.PrefetchScalarGridSpec(
        num_scalar_prefetch=1, grid=(n_seq,), in_specs=in_specs,
        out_specs=[per_seq((1, TS_PAD, NSA_Q_W)), per_seq((1, 2 * NSA_KV_W, wlen), 0, win_base)])
    return pl.pallas_call(
        functools.partial(_nsa_sample_body, n_pages=n_pages, past_len=past_len, ts=ts, n_sel=n_sel,
                          has_prev=win_prev is not None),
        grid_spec=grid_spec,
        out_shape=[jax.ShapeDtypeStruct((n_seq, TS_PAD, NSA_Q_W), F32),
                   jax.ShapeDtypeStruct(win_old.shape, F32)],
        input_output_aliases=aliases,
        compiler_params=_params(), name="nsa_attn_sample",
    )(*operands)


S5_NB = 8
S5_GB = 8
S5_HALF = S5_GB * S5_STATE
S5_TIME = 256


def _s5_disc_body(lr_ref, li_ref, ldt_ref, bre_ref, bim_ref, are_ref, aim_ref, bbre_ref, bbim_ref):
    dt = jnp.exp(ldt_ref[...])
    lr = jnp.minimum(lr_ref[...], -1e-4)
    li = li_ref[...]
    mag = jnp.exp(lr * dt)
    a_re = mag * jnp.cos(li * dt)
    a_im = mag * jnp.sin(li * dt)
    den = lr * lr + li * li
    z_re = ((a_re - 1.0) * lr + a_im * li) / den
    z_im = (a_im * lr - (a_re - 1.0) * li) / den
    are_ref[...] = a_re
    aim_ref[...] = a_im
    bbre_ref[...] = z_re * bre_ref[...] - z_im * bim_ref[...]
    bbim_ref[...] = z_re * bim_ref[...] + z_im * bre_ref[...]


def s5_discretize(lam_re, lam_im, log_dt, b_re, b_im):
    rows = S5_GROUPS * S5_GROUP_CH

    def per_channel(a):
        return jnp.broadcast_to(a[:, None, :], (S5_GROUPS, S5_GROUP_CH, S5_STATE)).reshape(rows, S5_STATE)

    args = (per_channel(lam_re), per_channel(lam_im), per_channel(jnp.broadcast_to(log_dt[:, None], lam_re.shape)),
            b_re.transpose(0, 2, 1).reshape(rows, S5_STATE), b_im.transpose(0, 2, 1).reshape(rows, S5_STATE))
    out = jax.ShapeDtypeStruct((rows, S5_STATE), F32)
    a_re, a_im, bb_re, bb_im = pl.pallas_call(_s5_disc_body, out_shape=[out] * 4, name="s5_discretize")(*args)
    shape = (S5_GROUPS, S5_GROUP_CH, S5_STATE)
    return a_re.reshape(shape)[:, 0], a_im.reshape(shape)[:, 0], bb_re.reshape(shape), bb_im.reshape(shape)


def _s5_scan_body(u_ref, bb_ref, cc_ref, a_ref, h0_ref, y_ref, hout_ref, bu_scr, hs_scr, h_scr, *, tc, use_h0):
    t = pl.program_id(2)

    @pl.when(t == 0)
    def _():
        h_scr[...] = h0_ref[0, 0] if use_h0 else jnp.zeros(h_scr.shape, F32)

    bu_scr[...] = _mm(u_ref[...], bb_ref[0])
    ar = a_ref[0][:, :S5_HALF]
    ai = a_ref[0][:, S5_HALF:]

    def step(i, carry):
        hr, hi = carry
        r = pl.multiple_of(i * S5_NB, S5_NB)
        nhr = ar * hr - ai * hi + bu_scr[pl.ds(r, S5_NB), :S5_HALF]
        nhi = ar * hi + ai * hr + bu_scr[pl.ds(r, S5_NB), S5_HALF:]
        hs_scr[pl.ds(r, S5_NB), :S5_HALF] = nhr
        hs_scr[pl.ds(r, S5_NB), S5_HALF:] = nhi
        return nhr, nhi

    hr, hi = lax.fori_loop(0, tc, step, (h_scr[:, :S5_HALF], h_scr[:, S5_HALF:]), unroll=min(tc, 8))
    h_scr[:, :S5_HALF] = hr
    h_scr[:, S5_HALF:] = hi
    y_ref[...] = _mm(hs_scr[...], cc_ref[0])

    @pl.when(t == pl.num_programs(2) - 1)
    def _():
        hout_ref[0, 0] = h_scr[...]


def s5_scan(u, h0, a_re, a_im, bb_re, bb_im, c_re, c_im):
    n_real, t_len, _ = u.shape
    n = -(-n_real // S5_NB) * S5_NB
    if n != n_real:
        u = jnp.pad(u, ((0, n - n_real), (0, 0), (0, 0)))
        h0 = None if h0 is None else jnp.pad(h0, ((0, n - n_real), (0, 0), (0, 0), (0, 0)))
    nb = n // S5_NB
    ngb = S5_GROUPS // S5_GB
    tc = min(S5_TIME, t_len)
    eye = jnp.eye(S5_GB, dtype=F32)

    def blockdiag_in(bb):
        return jnp.einsum('ab,xacp->xacbp', eye, bb.reshape(ngb, S5_GB, S5_GROUP_CH, S5_STATE)).reshape(
            ngb, S5_GB * S5_GROUP_CH, S5_HALF)

    def blockdiag_out(cc):
        return jnp.einsum('ab,xbcp->xapbc', eye, cc.reshape(ngb, S5_GB, S5_GROUP_CH, S5_STATE)).reshape(
            ngb, S5_HALF, S5_GB * S5_GROUP_CH)

    bb = jnp.concatenate([blockdiag_in(bb_re), blockdiag_in(bb_im)], axis=2).astype(MXU_DTYPE)
    cc = jnp.concatenate([blockdiag_out(c_re), blockdiag_out(-c_im)], axis=1).astype(MXU_DTYPE)
    a = jnp.concatenate([a_re.reshape(ngb, S5_HALF), a_im.reshape(ngb, S5_HALF)], axis=1)
    a = jnp.broadcast_to(a[:, None, :], (ngb, S5_NB, 2 * S5_HALF))
    use_h0 = h0 is not None
    if use_h0:
        h0b = h0.reshape(nb, S5_NB, 2, ngb, S5_HALF).transpose(0, 3, 1, 2, 4).reshape(nb, ngb, S5_NB, 2 * S5_HALF)
    else:
        h0b = jnp.zeros((1, 1, S5_NB, 2 * S5_HALF), F32)
    ub = u.reshape(nb, S5_NB, t_len, D_MODEL).transpose(0, 2, 1, 3).reshape(nb * t_len * S5_NB, D_MODEL)
    per = t_len // tc
    lanes_u = S5_GB * S5_GROUP_CH

    rows_spec = pl.BlockSpec((tc * S5_NB, lanes_u), lambda b, g, t: (b * per + t, g))
    state_spec = pl.BlockSpec((1, 1, S5_NB, 2 * S5_HALF), lambda b, g, t: (b, g, 0, 0))
    h0_spec = state_spec if use_h0 else pl.BlockSpec((1, 1, S5_NB, 2 * S5_HALF), lambda b, g, t: (0, 0, 0, 0))
    y, h_last = pl.pallas_call(
        functools.partial(_s5_scan_body, tc=tc, use_h0=use_h0),
        grid=(nb, ngb, per),
        in_specs=[rows_spec, pl.BlockSpec((1, lanes_u, 2 * S5_HALF), lambda b, g, t: (g, 0, 0)),
                  pl.BlockSpec((1, 2 * S5_HALF, lanes_u), lambda b, g, t: (g, 0, 0)),
                  pl.BlockSpec((1, S5_NB, 2 * S5_HALF), lambda b, g, t: (g, 0, 0)), h0_spec],
        out_specs=[rows_spec, state_spec],
        out_shape=[jax.ShapeDtypeStruct((nb * t_len * S5_NB, D_MODEL), F32),
                   jax.ShapeDtypeStruct((nb, ngb, S5_NB, 2 * S5_HALF), F32)],
        scratch_shapes=[pltpu.VMEM((tc * S5_NB, 2 * S5_HALF), F32), pltpu.VMEM((tc * S5_NB, 2 * S5_HALF), F32),
                        pltpu.VMEM((S5_NB, 2 * S5_HALF), F32)],
        compiler_params=pltpu.CompilerParams(dimension_semantics=("parallel", "parallel", "arbitrary"),
                                             vmem_limit_bytes=VMEM_LIMIT),
        name="s5_scan",
    )(ub, bb, cc, a, h0b)
    y = y.reshape(nb, t_len, S5_NB, D_MODEL).transpose(0, 2, 1, 3).reshape(n, t_len, D_MODEL)
    h_last = h_last.reshape(nb, ngb, S5_NB, 2, S5_GB, S5_STATE).transpose(0, 2, 3, 1, 4, 5).reshape(
        n, 2, S5_GROUPS, S5_STATE)
    return y[:n_real], h_last[:n_real]


def _s5_out_body(x_ref, y_ref, u_ref, d_ref, w_ref, o_ref):
    z = jax.nn.gelu(y_ref[...] + d_ref[...] * u_ref[...])
    ab = _mm(z, w_ref[...])
    o_ref[...] = x_ref[...] + ab[:, :D_MODEL] * jax.nn.sigmoid(ab[:, D_MODEL:])


def s5_out(x, y, u, d_skip, w_glu):
    rows = x.shape[0]
    tm = _row_tile(rows)
    return pl.pallas_call(
        _s5_out_body,
        grid=(rows // tm,),
        in_specs=[_rows(tm, D_MODEL), _rows(tm, D_MODEL), _rows(tm, D_MODEL), _resident((1, D_MODEL)),
                  _resident((D_MODEL, 2 * D_MODEL))],
        out_specs=_rows(tm, D_MODEL),
        out_shape=jax.ShapeDtypeStruct((rows, D_MODEL), F32),
        compiler_params=_params(),
        name="s5_out",
    )(x, y, u, d_skip.reshape(1, -1), w_glu.astype(MXU_DTYPE))


HG_SUB = 16
HG_TIME = 256
HG_HEADS_PER_STEP = 2


def _mm_exact(l01, x):
    x1 = x.astype(MXU_DTYPE)
    r1 = x - x1.astype(F32)
    x2 = r1.astype(MXU_DTYPE)
    x3 = (r1 - x2.astype(F32)).astype(MXU_DTYPE)
    dot = functools.partial(jnp.dot, preferred_element_type=F32)
    return dot(l01, x1) + dot(l01, x2) + dot(l01, x3)


def _hgrn_body(q_ref, fz_ref, v_ref, g_ref, lb_ref, og_ref, s0_ref, o_ref, sout_ref, s_scr,
               *, tb, chunk, sub, valid, use_s0, hps):
    tblk = pl.program_id(2)

    @pl.when(tblk == 0)
    def _():
        s_scr[...] = s0_ref[0] if use_s0 else jnp.zeros(s_scr.shape, F32)

    og = og_ref[...]
    eye = (lax.broadcasted_iota(jnp.int32, (HG_DK, HG_DK), 0) == lax.broadcasted_iota(jnp.int32, (HG_DK, HG_DK), 1))
    tril = jnp.where(lax.broadcasted_iota(jnp.int32, (chunk, chunk), 0)
                     >= lax.broadcasted_iota(jnp.int32, (chunk, chunk), 1), 1.0, 0.0).astype(MXU_DTYPE)
    trow = lax.broadcasted_iota(jnp.int32, (sub, 1), 0)
    nsub = chunk // sub

    for hh, ci in itertools.product(range(hps), range(tb // chunk)):
        r0 = ci * chunk
        cols = slice(hh * HG_DK, (hh + 1) * HG_DK)
        lb = lb_ref[:, cols]
        q = q_ref[r0:r0 + chunk, cols]
        v = v_ref[r0:r0 + chunk, cols]
        f = lb + (1.0 - lb) * jax.nn.sigmoid(fz_ref[r0:r0 + chunk, cols])
        k = 1.0 - f
        lf = jnp.log(f)
        if valid < tb:
            live = (r0 + lax.broadcasted_iota(jnp.int32, (chunk, 1), 0)) < valid
            k = jnp.where(live, k, 0.0)
            lf = jnp.where(live, lf, 0.0)
        gcum = _mm_exact(tril, lf)
        state = s_scr[hh]
        o_inter = _mm(q * jnp.exp(gcum), state)
        o_blocks = [o_inter[i * sub:(i + 1) * sub] for i in range(nsub)]

        for j in range(nsub - 1):
            lo, hi = j * sub, (j + 1) * sub
            g_ref_row = gcum[hi - 1:hi, :]
            k_t = k[lo:hi] * jnp.exp(g_ref_row - gcum[lo:hi])
            q_t = q[hi:] * jnp.exp(gcum[hi:] - g_ref_row)
            contrib = _mm(_mm_nt(q_t, k_t), v[lo:hi])
            for i in range(j + 1, nsub):
                o_blocks[i] = o_blocks[i] + contrib[(i - j - 1) * sub:(i - j) * sub]

        for i in range(nsub):
            lo, hi = i * sub, (i + 1) * sub
            q_i, k_i, v_i, g_i = q[lo:hi], k[lo:hi], v[lo:hi], gcum[lo:hi]
            acc = jnp.zeros((sub, HG_DV), F32)
            for s in range(sub):
                decay = jnp.exp(jnp.minimum(g_i - g_i[s:s + 1], 0.0))
                wgt = jnp.sum(q_i * k_i[s:s + 1] * decay, axis=1, keepdims=True)
                acc = acc + jnp.where(trow >= s, wgt, 0.0) * v_i[s:s + 1]
            o_blocks[i] = o_blocks[i] + acc

        g_last = gcum[chunk - 1:chunk, :]
        k_t = k * jnp.exp(g_last - gcum)
        decay_col = jnp.sum(jnp.where(eye, jnp.exp(g_last), 0.0), axis=1, keepdims=True)
        kv = lax.dot_general(k_t.astype(MXU_DTYPE), v.astype(MXU_DTYPE), (((0,), (0,)), ((), ())),
                             preferred_element_type=F32)
        s_scr[hh] = decay_col * state + kv

        o = jnp.concatenate(o_blocks, axis=0)
        o = o * lax.rsqrt(jnp.mean(o * o, axis=-1, keepdims=True) + RMS_EPS) * og
        gate = g_ref[r0:r0 + chunk, cols]
        o_ref[r0:r0 + chunk, cols] = o * (gate * jax.nn.sigmoid(gate))

    @pl.when(tblk == pl.num_programs(2) - 1)
    def _():
        sout_ref[0] = s_scr[...]


def hgrn_scan(pr, s0, o_gain, lb, n_seq, t_rows, valid):
    tb = min(HG_TIME, t_rows)
    chunk = min(HG_CHUNK, tb)
    sub = min(HG_SUB, chunk)
    per = t_rows // tb
    hps = HG_HEADS if tb < HG_TIME else HG_HEADS_PER_STEP
    hblocks = HG_HEADS // hps
    use_s0 = s0 is not None
    if not use_s0:
        s0 = jnp.zeros((1, hps, HG_DK, HG_DV), F32)

    def part(idx):
        return pl.BlockSpec((tb, hps * HG_DK), lambda n, h, t: (n * per + t, idx * hblocks + h))

    head_vec = pl.BlockSpec((1, hps * HG_DK), lambda n, h, t: (0, h))
    state_spec = pl.BlockSpec((1, hps, HG_DK, HG_DV), lambda n, h, t: (n, h, 0, 0))
    s0_spec = state_spec if use_s0 else pl.BlockSpec((1, hps, HG_DK, HG_DV), lambda n, h, t: (0, 0, 0, 0))
    return pl.pallas_call(
        functools.partial(_hgrn_body, tb=tb, chunk=chunk, sub=sub, valid=valid, use_s0=use_s0, hps=hps),
        grid=(n_seq, hblocks, per),
        in_specs=[part(0), part(1), part(2), part(3), head_vec, pl.BlockSpec((1, HG_DV), lambda n, h, t: (0, 0)), s0_spec],
        out_specs=[pl.BlockSpec((tb, hps * HG_DV), lambda n, h, t: (n * per + t, h)), state_spec],
        out_shape=[jax.ShapeDtypeStruct((n_seq * t_rows, D_MODEL), F32),
                   jax.ShapeDtypeStruct((n_seq, HG_HEADS, HG_DK, HG_DV), F32)],
        scratch_shapes=[pltpu.VMEM((hps, HG_DK, HG_DV), F32)],
        compiler_params=pltpu.CompilerParams(dimension_semantics=("parallel", "parallel", "arbitrary"),
                                             vmem_limit_bytes=VMEM_LIMIT),
        name="hgrn_scan",
    )(pr, pr, pr, pr, lb.reshape(1, -1), o_gain.reshape(1, -1), s0)


def kernel(x_prompt, x_sample, cache_nsa, state_nsa_win, state_s5, state_hgrn, page_table, p_prompt, p_sample, norm_gain, ffn_w_in, ffn_w_out, ple_w_gate, ple_w_proj, nsa_w_in, nsa_w_out, nsa_qk_gain, nsa_cmp_pe, nsa_cmp_w, s5_lam_re, s5_lam_im, s5_log_dt, s5_b_re, s5_b_im, s5_c_re, s5_c_im, s5_d, s5_w_glu, hg_w_in, hg_w_out, hg_o_gain, hg_lb_raw):
    B, T, _ = x_prompt.shape
    Bs, Ts, _ = x_sample.shape
    rp = B * T
    rs = Bs * Ts
    n_pages = page_table.shape[1]
    past_len = n_pages * PAGE_SIZE
    n_phys = cache_nsa.shape[1]
    wlen = state_nsa_win.shape[2]
    assert T % Q_TILE == 0 and T % PAGE_SIZE == 0 and T % min(KV_TILE, T) == 0 and Ts <= TS_PAD
    lb_sm = jax.nn.softmax(hg_lb_raw.astype(F32), axis=0)
    lower_bounds = jnp.cumsum(lb_sm, axis=0) - lb_sm[0]

    x = jnp.concatenate([x_prompt.reshape(rp, D_MODEL), x_sample.reshape(rs, D_MODEL)], axis=0)
    p_all = jnp.concatenate([p_prompt.reshape(DEPTH, rp, PLE_DIM), p_sample.reshape(DEPTH, rs, PLE_DIM)], axis=1)
    pos = jnp.concatenate([jnp.tile(jnp.arange(T), B), jnp.tile(past_len + jnp.arange(Ts), Bs)])
    cache_pages = cache_nsa.transpose(0, 1, 3, 4, 5, 2).reshape(-1, 4 * NSA_KV_W, PAGE_SIZE)
    win_state = state_nsa_win.transpose(0, 1, 3, 4, 5, 2).reshape(-1, 2 * NSA_KV_W, wlen)
    prompt_table = (jnp.arange(B, dtype=jnp.int32)[:, None] * (T // PAGE_SIZE)
                    + jnp.arange(T // PAGE_SIZE, dtype=jnp.int32)[None, :])
    out_perm = _head_perm()

    def split(a):
        return a[:rp].reshape(B, T, -1), a[rp:].reshape(Bs, Ts, -1)

    def join(a, b):
        return jnp.concatenate([a.reshape(rp, -1), b.reshape(rs, -1)], axis=0)

    def sample_pad(a):
        return jnp.pad(a[rp:].reshape(Bs, Ts, -1), ((0, 0), (0, TS_PAD - Ts), (0, 0)))

    outs_p = {0: [], 1: [], 2: [], 3: []}
    outs_s = {0: [], 1: [], 2: [], 3: []}
    win_all = None
    for i in range(DEPTH):
        kind = LAYER_KIND[i]
        j = LAYER_SLOT[i]
        g = norm_gain[i]
        x, xn = ffn_step(x, g[0], g[1], ffn_w_in[i, 0], ffn_w_out[i, 0])
        if kind == 0:
            qn, qr, rows_new, win_new, gates = nsa_proj(xn, nsa_w_in[j], nsa_qk_gain[j], pos)
            cmp_args = (nsa_cmp_pe[j], nsa_cmp_w[j], nsa_qk_gain[j, 1])
            kc_p, vc_p = nsa_compress(rows_new.reshape(-1, PAGE_SIZE, 4 * NSA_KV_W), prompt_table, *cmp_args, False)
            o_p = nsa_attn_prompt(qn, qr, gates, kc_p, vc_p, rows_new, win_new, B, T)
            table = page_table.astype(jnp.int32) + j * n_phys
            kc_s, vc_s = nsa_compress(cache_pages, table, *cmp_args, True)
            o_s, win_all = nsa_attn_sample(table, cache_pages, sample_pad(qn), sample_pad(qr), sample_pad(gates), kc_s,
                                           vc_s, sample_pad(rows_new), win_state, sample_pad(win_new), j * Bs, past_len,
                                           Ts, win_all)
            r_p, r_s = split(rows_new)
            outs_p[0].append(r_p.reshape(B, T, 4, NSA_KV_HEADS, HEAD_DIM))
            outs_s[0].append(r_s.reshape(Bs, Ts, 4, NSA_KV_HEADS, HEAD_DIM))
            buf = min(WINDOW, T)
            outs_p[1].append(win_new[:rp].reshape(B, T, 2, NSA_KV_HEADS, HEAD_DIM)[:, T - buf:])
            o_all = jnp.concatenate([o_p, o_s[:, :Ts].reshape(rs, NSA_Q_W).astype(o_p.dtype)], axis=0)
            x = resid_proj(x, o_all, nsa_w_out[j][out_perm])
        elif kind == 1:
            u_p, u_s = split(xn)
            disc = s5_discretize(s5_lam_re[j], s5_lam_im[j], s5_log_dt[j], s5_b_re[j], s5_b_im[j])
            y_p, h_p = s5_scan(u_p, None, *disc, s5_c_re[j], s5_c_im[j])
            y_s, h_s = s5_scan(u_s, state_s5[j], *disc, s5_c_re[j], s5_c_im[j])
            outs_p[2].append(h_p); outs_s[2].append(h_s)
            x = s5_out(x, join(y_p, y_s), xn, s5_d[j], s5_w_glu[j])
        else:
            pr = proj(xn, hg_w_in[j])
            pr_s = jnp.pad(pr[rp:].reshape(Bs, Ts, -1), ((0, 0), (0, TS_PAD - Ts), (0, 0))).reshape(Bs * TS_PAD, -1)
            o_p, s_p = hgrn_scan(pr, None, hg_o_gain[j], lower_bounds[i], B, T, T)
            o_s, s_s = hgrn_scan(pr_s, state_hgrn[j], hg_o_gain[j], lower_bounds[i], Bs, TS_PAD, Ts)
            outs_p[3].append(s_p); outs_s[3].append(s_s)
            o_s = o_s.reshape(Bs, TS_PAD, D_MODEL)[:, :Ts].reshape(rs, D_MODEL)
            x = resid_proj(x, jnp.concatenate([o_p, o_s], axis=0), hg_w_out[j])
        x = ffn_ple_step(x, g[2], g[3], ffn_w_in[i, 1], ffn_w_out[i, 1], p_all[i], ple_w_gate[i], ple_w_proj[i])

    y_p, y_s = split(x)
    win_out = win_all.reshape(-1, Bs, 2, NSA_KV_HEADS, HEAD_DIM, wlen).transpose(0, 1, 5, 2, 3, 4)
    return (y_p, y_s,
            jnp.stack(outs_p[0]), jnp.stack(outs_p[1]), jnp.stack(outs_p[2]), jnp.stack(outs_p[3]),
            jnp.stack(outs_s[0]), win_out, jnp.stack(outs_s[2]), jnp.stack(outs_s[3]))
```

```python
import functools
import itertools

import numpy as np
import jax
import jax.numpy as jnp
from jax import lax
from jax.experimental import pallas as pl
from jax.experimental.pallas import tpu as pltpu

F32 = jnp.float32
MXU_DTYPE = jnp.bfloat16

D_MODEL = 1024
DEPTH = 4
PAGE_SIZE = 128
D_FF = 2816
PLE_DIM = 256
RMS_EPS = 1e-6
LAYER_KIND = (0, 1, 2, 0)
LAYER_SLOT = (0, 0, 0, 1)

NSA_HEADS = 16
NSA_KV_HEADS = 4
HEAD_DIM = 64
NSA_HPG = NSA_HEADS // NSA_KV_HEADS
NSA_KV_W = NSA_KV_HEADS * HEAD_DIM
NSA_Q_W = NSA_HEADS * HEAD_DIM
NSA_IN = NSA_Q_W + 6 * NSA_KV_W + 3 * NSA_HEADS
CMP_BLOCK = 32
CMP_STRIDE = 16
SLC_BLOCK = 64
SLC_TOPN = 16
WINDOW = 512
FORCE_BONUS = 1e4
NEG_INF = -1e30
ROPE_THETA = 500000.0
ROPE_DIMS = HEAD_DIM // 4
ATT_SCALE = HEAD_DIM ** -0.5

S5_GROUP_CH = 16
S5_GROUPS = D_MODEL // S5_GROUP_CH
S5_STATE = 64

HG_DK = 128
HG_HEADS = D_MODEL // HG_DK
HG_DV = D_MODEL // HG_HEADS
HG_CHUNK = 64

V7X_VMEM_BYTES = 64 * 1024 * 1024
VMEM_LIMIT = V7X_VMEM_BYTES - 8 * 1024 * 1024
LANES = 128
ROW_TILE = 512
FF_CHUNK = 256
Q_TILE = 128
KV_TILE = 512
TS_PAD = 8
CMP_PITCH = 20
SOFTMAX_M0 = -1e29


def _resident(shape):
    return pl.BlockSpec(shape, lambda *_: (0,) * len(shape), pipeline_mode=pl.Buffered(1))


def _row_tile(rows):
    return max(t for t in range(8, ROW_TILE + 1, 8) if rows % t == 0)


def _rows(tm, width):
    return pl.BlockSpec((tm, width), lambda i: (i, 0))


def _params(n_axes=1):
    return pltpu.CompilerParams(dimension_semantics=("parallel",) * n_axes, vmem_limit_bytes=VMEM_LIMIT)


def _rms(x, g):
    return x * lax.rsqrt(jnp.mean(x * x, axis=-1, keepdims=True) + RMS_EPS) * g


def _mm(a, b):
    return jnp.dot(a.astype(MXU_DTYPE), b.astype(MXU_DTYPE), preferred_element_type=F32)


def _mm_nt(a, b):
    return lax.dot_general(a.astype(MXU_DTYPE), b.astype(MXU_DTYPE), (((1,), (1,)), ((), ())),
                           preferred_element_type=F32)


def _mm_split(a, b):
    hi = a.astype(MXU_DTYPE)
    lo = (a - hi.astype(F32)).astype(MXU_DTYPE)
    return (jnp.dot(hi, b, preferred_element_type=F32) + jnp.dot(lo, b, preferred_element_type=F32))


def _ffn_body(x_ref, g_ref, gn_ref, win_ref, wout_ref, o_ref, on_ref, h_ref):
    x = x_ref[...]
    xb = _rms(x, g_ref[...]).astype(MXU_DTYPE)
    for c in range(D_FF // FF_CHUNK):
        lo = c * FF_CHUNK
        a = _mm(xb, win_ref[:, lo:lo + FF_CHUNK])
        b = _mm(xb, win_ref[:, D_FF + lo:D_FF + lo + FF_CHUNK])
        h_ref[:, lo:lo + FF_CHUNK] = (a * jax.nn.sigmoid(a) * b).astype(MXU_DTYPE)
    y = x + 0.5 * _mm(h_ref[...], wout_ref[...])
    o_ref[...] = y
    on_ref[...] = _rms(y, gn_ref[...])


def ffn_step(x, g, g_next, w_in, w_out):
    rows = x.shape[0]
    tm = _row_tile(rows)
    out = jax.ShapeDtypeStruct((rows, D_MODEL), F32)
    return pl.pallas_call(
        _ffn_body,
        grid=(rows // tm,),
        in_specs=[_rows(tm, D_MODEL), _resident((1, D_MODEL)), _resident((1, D_MODEL)),
                  _resident((D_MODEL, 2 * D_FF)), _resident((D_FF, D_MODEL))],
        out_specs=[_rows(tm, D_MODEL), _rows(tm, D_MODEL)],
        out_shape=[out, out],
        scratch_shapes=[pltpu.VMEM((tm, D_FF), MXU_DTYPE)],
        compiler_params=_params(),
        name="ffn_step",
    )(x, g.reshape(1, -1), g_next.reshape(1, -1), w_in.astype(MXU_DTYPE), w_out.astype(MXU_DTYPE))


def _ffn_ple_body(x_ref, g_ref, gn_ref, win_ref, wout_ref, p_ref, wg_ref, wp_ref, o_ref, h_ref):
    x = x_ref[...]
    xb = _rms(x, g_ref[...]).astype(MXU_DTYPE)
    for c in range(D_FF // FF_CHUNK):
        lo = c * FF_CHUNK
        a = _mm(xb, win_ref[:, lo:lo + FF_CHUNK])
        b = _mm(xb, win_ref[:, D_FF + lo:D_FF + lo + FF_CHUNK])
        h_ref[:, lo:lo + FF_CHUNK] = (a * jax.nn.sigmoid(a) * b).astype(MXU_DTYPE)
    y = x + 0.5 * _mm(h_ref[...], wout_ref[...])
    gate = jax.nn.sigmoid(_mm(_rms(y, gn_ref[...]), wg_ref[...]))
    o_ref[...] = y + gate * _mm(p_ref[...], wp_ref[...])


def ffn_ple_step(x, g, g_ple, w_in, w_out, p, w_gate, w_proj):
    rows = x.shape[0]
    tm = _row_tile(rows)
    return pl.pallas_call(
        _ffn_ple_body,
        grid=(rows // tm,),
        in_specs=[_rows(tm, D_MODEL), _resident((1, D_MODEL)), _resident((1, D_MODEL)),
                  _resident((D_MODEL, 2 * D_FF)), _resident((D_FF, D_MODEL)), _rows(tm, PLE_DIM),
                  _resident((D_MODEL, D_MODEL)), _resident((PLE_DIM, D_MODEL))],
        out_specs=_rows(tm, D_MODEL),
        out_shape=jax.ShapeDtypeStruct((rows, D_MODEL), F32),
        scratch_shapes=[pltpu.VMEM((tm, D_FF), MXU_DTYPE)],
        compiler_params=_params(),
        name="ffn_ple_step",
    )(x, g.reshape(1, -1), g_ple.reshape(1, -1), w_in.astype(MXU_DTYPE), w_out.astype(MXU_DTYPE), p,
      w_gate.astype(MXU_DTYPE), w_proj.astype(MXU_DTYPE))


def _proj_body(a_ref, w_ref, o_ref):
    o_ref[...] = _mm(a_ref[...], w_ref[...])


def proj(a, w):
    rows, k = a.shape
    n = w.shape[1]
    tm = _row_tile(rows)
    return pl.pallas_call(
        _proj_body,
        grid=(rows // tm,),
        in_specs=[_rows(tm, k), _resident((k, n))],
        out_specs=_rows(tm, n),
        out_shape=jax.ShapeDtypeStruct((rows, n), F32),
        compiler_params=_params(),
        name="proj",
    )(a, w.astype(MXU_DTYPE))


def _resid_body(x_ref, a_ref, w_ref, o_ref):
    o_ref[...] = x_ref[...] + _mm(a_ref[...], w_ref[...])


def resid_proj(x, a, w):
    rows, k = a.shape
    tm = _row_tile(rows)
    return pl.pallas_call(
        _resid_body,
        grid=(rows // tm,),
        in_specs=[_rows(tm, D_MODEL), _rows(tm, k), _resident((k, D_MODEL))],
        out_specs=_rows(tm, D_MODEL),
        out_shape=jax.ShapeDtypeStruct((rows, D_MODEL), F32),
        compiler_params=_params(),
        name="resid_proj",
    )(x, a, w.astype(MXU_DTYPE))


def _head_perm():
    idx = np.arange(NSA_Q_W).reshape(NSA_KV_HEADS, NSA_HPG, HEAD_DIM)
    return idx.transpose(1, 0, 2).reshape(-1)


def _gate_expand():
    x = np.zeros((LANES, 3 * NSA_Q_W), np.float32)
    for b in range(3):
        for g in range(NSA_KV_HEADS):
            for j in range(NSA_HPG):
                h = g * NSA_HPG + j
                c0 = b * NSA_Q_W + (j * NSA_KV_HEADS + g) * HEAD_DIM
                x[b * NSA_HEADS + h, c0:c0 + HEAD_DIM] = 1.0
    return x


def _overlap_matrix(nb, length):
    n_sel = -(-length // SLC_BLOCK)
    c0 = np.arange(nb)[:, None] * CMP_STRIDE
    s0 = np.arange(LANES)[None, :] * SLC_BLOCK
    ov = np.clip(np.minimum(c0 + CMP_BLOCK, s0 + SLC_BLOCK) - np.maximum(c0, s0), 0, None) / CMP_BLOCK
    ov = np.where(np.arange(LANES)[None, :] < n_sel, ov, 0.0)
    return ov.astype(np.float32)


def _rope_tables(pos):
    half = ROPE_DIMS // 2
    inv = ROPE_THETA ** (-jnp.arange(half, dtype=F32) / half)
    ang = pos.astype(F32)[:, None] * inv[None, :]
    cos, sin = jnp.cos(ang), jnp.sin(ang)
    ones = jnp.ones((pos.shape[0], HEAD_DIM - ROPE_DIMS), F32)
    zeros = jnp.zeros((pos.shape[0], HEAD_DIM - ROPE_DIMS), F32)
    zh = jnp.zeros_like(sin)
    c = jnp.concatenate([cos, cos, ones], axis=1)
    sa = jnp.concatenate([-sin, zh, zeros], axis=1)
    sb = jnp.concatenate([zh, sin, zeros], axis=1)
    rep = LANES // HEAD_DIM
    return jnp.tile(c, (1, rep)), jnp.tile(sa, (1, rep)), jnp.tile(sb, (1, rep))


def _nsa_proj_body(xn_ref, wq_ref, wkv_ref, wgl_ref, b64_ref, gq_ref, gk_ref, c_ref, sa_ref, sb_ref,
                   qn_ref, qr_ref, rows_ref, win_ref, gates_ref):
    xb = xn_ref[...].astype(MXU_DTYPE)
    c, sa, sb = c_ref[...], sa_ref[...], sb_ref[...]

    def head_norm(v, gain):
        w = v.shape[1]
        ms = _mm_split(v * v, b64_ref[:w, :w])
        return v * lax.rsqrt(ms + RMS_EPS) * gain

    def rope(v):
        w = v.shape[1]
        rep = w // LANES
        ct, sat, sbt = (jnp.concatenate([t] * rep, axis=1) for t in (c, sa, sb))
        return v * ct + pltpu.roll(v, w - ROPE_DIMS // 2, 1) * sat + pltpu.roll(v, ROPE_DIMS // 2, 1) * sbt

    qn = head_norm(_mm(xb, wq_ref[...]), gq_ref[...])
    qn_ref[...] = qn.astype(qn_ref.dtype)
    qr_ref[...] = rope(qn).astype(qr_ref.dtype)
    kv = _mm(xb, wkv_ref[...])
    w = NSA_KV_W
    rows_ref[:, 0:2 * w] = kv[:, 0:2 * w]
    rows_ref[:, 2 * w:3 * w] = rope(head_norm(kv[:, 2 * w:3 * w], gk_ref[0:1, :]))
    rows_ref[:, 3 * w:4 * w] = kv[:, 3 * w:4 * w]
    win_ref[:, 0:w] = rope(head_norm(kv[:, 4 * w:5 * w], gk_ref[1:2, :]))
    win_ref[:, w:2 * w] = kv[:, 5 * w:6 * w]
    gates_ref[...] = jax.nn.sigmoid(_mm(xb, wgl_ref[...]))


def nsa_proj(xn, w_in, qk_gain, pos):
    rows = xn.shape[0]
    tm = _row_tile(rows)
    kvw = 6 * NSA_KV_W
    wq = w_in[:, :NSA_Q_W][:, _head_perm()].astype(MXU_DTYPE)
    wkv = w_in[:, NSA_Q_W:NSA_Q_W + kvw].astype(MXU_DTYPE)
    wgl = jnp.pad(w_in[:, NSA_Q_W + kvw:], ((0, 0), (0, LANES - 3 * NSA_HEADS))).astype(MXU_DTYPE)
    b64 = jnp.asarray(np.kron(np.eye(NSA_HEADS), np.full((HEAD_DIM, HEAD_DIM), 1.0 / HEAD_DIM)), MXU_DTYPE)
    gq = jnp.tile(qk_gain[0], NSA_HEADS).reshape(1, -1)
    gk = jnp.stack([jnp.tile(qk_gain[2], NSA_KV_HEADS), jnp.tile(qk_gain[3], NSA_KV_HEADS)])
    c, sa, sb = _rope_tables(pos)
    return pl.pallas_call(
        _nsa_proj_body,
        grid=(rows // tm,),
        in_specs=[_rows(tm, D_MODEL), _resident((D_MODEL, NSA_Q_W)), _resident((D_MODEL, kvw)),
                  _resident((D_MODEL, LANES)), _resident((NSA_Q_W, NSA_Q_W)), _resident((1, NSA_Q_W)),
                  _resident((2, NSA_KV_W)), _rows(tm, LANES), _rows(tm, LANES), _rows(tm, LANES)],
        out_specs=[_rows(tm, NSA_Q_W), _rows(tm, NSA_Q_W), _rows(tm, 4 * NSA_KV_W), _rows(tm, 2 * NSA_KV_W),
                   _rows(tm, LANES)],
        out_shape=[jax.ShapeDtypeStruct((rows, NSA_Q_W), MXU_DTYPE), jax.ShapeDtypeStruct((rows, NSA_Q_W), MXU_DTYPE),
                   jax.ShapeDtypeStruct((rows, 4 * NSA_KV_W), F32), jax.ShapeDtypeStruct((rows, 2 * NSA_KV_W), F32),
                   jax.ShapeDtypeStruct((rows, LANES), F32)],
        compiler_params=_params(),
        name="nsa_proj",
    )(xn, wq, wkv, wgl, b64, gq, gk, c, sa, sb)


def _nsa_compress_body(pt_ref, *refs, n_pages, seqs, transposed):
    del pt_ref
    page_refs = refs[:seqs * n_pages]
    wk_ref, wv_ref, pek_ref, pev_ref, b64_ref, gk_ref, kc_ref, vc_ref, stage = refs[seqs * n_pages:]
    per_page = PAGE_SIZE // CMP_STRIDE
    nb_seq = n_pages * per_page
    nb = seqs * nb_seq
    tiles = 2 * NSA_KV_W // LANES

    for p, r in enumerate(page_refs):
        for j in range(tiles):
            if transposed:
                tile = r[0, j * LANES:(j + 1) * LANES, :].T
            else:
                tile = r[0, :, j * LANES:(j + 1) * LANES]
            for c in range(per_page):
                row = (p * per_page + c) * CMP_PITCH
                stage[j, row:row + CMP_STRIDE, :] = tile[c * CMP_STRIDE:(c + 1) * CMP_STRIDE]

    def compress(kind, w_ref, pe_ref):
        per_kind = NSA_KV_W // LANES
        first = jnp.zeros((nb, NSA_KV_W), F32)
        second = jnp.zeros((nb, NSA_KV_W), F32)
        for l in range(CMP_STRIDE):
            x = jnp.concatenate([stage[kind * per_kind + j, pl.ds(l, nb, stride=CMP_PITCH), :]
                                 for j in range(per_kind)], axis=1)
            first = first + _mm(x + pe_ref[l:l + 1, :], w_ref[l])
            second = second + _mm(x + pe_ref[CMP_STRIDE + l:CMP_STRIDE + l + 1, :], w_ref[CMP_STRIDE + l])
        return first + pltpu.roll(second, nb - 1, 0)

    kc = compress(0, wk_ref, pek_ref)
    ms = _mm_split(kc * kc, b64_ref[...])
    kc = kc * lax.rsqrt(ms + RMS_EPS) * gk_ref[...]
    vc = compress(1, wv_ref, pev_ref)
    for q in range(seqs):
        kc_ref[q] = kc[q * nb_seq:(q + 1) * nb_seq]
        vc_ref[q] = vc[q * nb_seq:(q + 1) * nb_seq]


def nsa_compress(pages, table, cmp_pe, cmp_w, k_gain, transposed):
    n_seq, n_pages = table.shape
    nb = n_pages * (PAGE_SIZE // CMP_STRIDE)
    eye = jnp.eye(NSA_KV_HEADS, dtype=F32)
    w4 = jnp.einsum('gh,klde->klgdhe', eye, cmp_w).reshape(2, CMP_BLOCK, NSA_KV_W, NSA_KV_W).astype(MXU_DTYPE)
    pe4 = jnp.tile(cmp_pe, (1, 1, NSA_KV_HEADS))
    b64 = jnp.asarray(np.kron(np.eye(NSA_KV_HEADS), np.full((HEAD_DIM, HEAD_DIM), 1.0 / HEAD_DIM)), MXU_DTYPE)
    gk = jnp.tile(k_gain, NSA_KV_HEADS).reshape(1, -1)

    page_block = (1, 2 * NSA_KV_W, PAGE_SIZE) if transposed else (1, PAGE_SIZE, 2 * NSA_KV_W)

    seqs = 2 if n_seq % 2 == 0 else 1

    def page_spec(q, p):
        return pl.BlockSpec(page_block, lambda n, pt: (pt[n * seqs + q, p], 0, 0))

    def const(shape):
        return pl.BlockSpec(shape, lambda n, pt: (0,) * len(shape), pipeline_mode=pl.Buffered(1))

    out = jax.ShapeDtypeStruct((n_seq, nb, NSA_KV_W), F32)
    out_spec = pl.BlockSpec((seqs, nb, NSA_KV_W), lambda n, pt: (n, 0, 0))
    grid_spec = pltpu.PrefetchScalarGridSpec(
        num_scalar_prefetch=1, grid=(n_seq // seqs,),
        in_specs=[page_spec(q, p) for q in range(seqs) for p in range(n_pages)] + [
            const((CMP_BLOCK, NSA_KV_W, NSA_KV_W)), const((CMP_BLOCK, NSA_KV_W, NSA_KV_W)),
            const((CMP_BLOCK, NSA_KV_W)), const((CMP_BLOCK, NSA_KV_W)), const((NSA_KV_W, NSA_KV_W)),
            const((1, NSA_KV_W))],
        out_specs=[out_spec, out_spec],
        scratch_shapes=[pltpu.VMEM((2 * NSA_KV_W // LANES, seqs * nb * CMP_PITCH, LANES), F32)])
    return pl.pallas_call(
        functools.partial(_nsa_compress_body, n_pages=n_pages, seqs=seqs, transposed=transposed),
        grid_spec=grid_spec, out_shape=[out, out], compiler_params=_params(), name="nsa_compress",
    )(table, *([pages] * (seqs * n_pages)), w4[0], w4[1], pe4[0], pe4[1], b64, gk)


def _group_masks():
    lane = lax.broadcasted_iota(jnp.int32, (1, NSA_KV_W), 1)
    return [(lane // HEAD_DIM) == g for g in range(NSA_KV_HEADS)]


def _blockdiag(q, bm):
    zero = jnp.zeros((), q.dtype)
    return jnp.concatenate([jnp.where(bm[g], q[:, NSA_KV_W * j:NSA_KV_W * (j + 1)], zero)
                            for g in range(NSA_KV_HEADS) for j in range(NSA_HPG)], axis=0)


def _extract(obd, bm, tq):
    outs = []
    for j in range(NSA_HPG):
        z = jnp.zeros((tq, NSA_KV_W), F32)
        for g in range(NSA_KV_HEADS):
            r0 = (g * NSA_HPG + j) * tq
            z = z + jnp.where(bm[g], obd[r0:r0 + tq], 0.0)
        outs.append(z)
    return jnp.concatenate(outs, axis=1)


def _per_head_rows(a, tq):
    k = a.shape[1]
    a4 = jnp.broadcast_to(a.reshape(NSA_KV_HEADS, 1, tq, k), (NSA_KV_HEADS, NSA_HPG, tq, k))
    return a4.reshape(NSA_HEADS * tq, k)


def _topk_mask(score, n_sel, tq):
    if tq == LANES:
        nsp = -(-n_sel // 8) * 8
        sub = lax.broadcasted_iota(jnp.int32, (nsp, 1), 0)
        outs = []
        for g in range(NSA_KV_HEADS):
            st = score[g * tq:(g + 1) * tq].T[:nsp]
            rank = jnp.zeros(st.shape, F32)
            for s2 in range(n_sel):
                row = st[s2:s2 + 1, :]
                rank = rank + jnp.where(row > st, 1.0, jnp.where(row == st, jnp.where(sub > s2, 1.0, 0.0), 0.0))
            sel_t = jnp.where(rank < SLC_TOPN, 1.0, 0.0)
            sel_t = jnp.concatenate([sel_t, jnp.zeros((LANES - nsp, tq), F32)], axis=0)
            outs.append(sel_t.T)
        return jnp.concatenate(outs, axis=0)
    lane = lax.broadcasted_iota(jnp.int32, (1, LANES), 1)
    rank = jnp.zeros(score.shape, F32)
    for s2 in range(n_sel):
        col = score[:, s2:s2 + 1]
        rank = rank + jnp.where(col > score, 1.0, jnp.where(col == score, jnp.where(lane > s2, 1.0, 0.0), 0.0))
    return jnp.where(rank < SLC_TOPN, 1.0, 0.0)


def _cmp_and_select(qbd_n, kc, vc, ov, pos_base, tq, n_sel):
    nb = kc.shape[0]
    r = lax.broadcasted_iota(jnp.int32, (NSA_HEADS * tq, 1), 0)
    qpos = pos_base + (r & (tq - 1))
    blk_end = lax.broadcasted_iota(jnp.int32, (1, nb), 1) * CMP_STRIDE + (CMP_BLOCK - 1)
    visible = blk_end <= qpos
    s = jnp.where(visible, _mm_nt(qbd_n, kc) * ATT_SCALE, NEG_INF)
    e = jnp.exp(s - jnp.max(s, axis=-1, keepdims=True))
    p = jnp.where(visible, e / jnp.sum(e, axis=-1, keepdims=True), 0.0)
    o = _mm(p, vc)
    psum = jnp.sum(p.reshape(NSA_KV_HEADS, NSA_HPG, tq, nb), axis=1).reshape(NSA_KV_HEADS * tq, nb)
    imp = _mm_split(psum, ov.astype(MXU_DTYPE))
    r4 = lax.broadcasted_iota(jnp.int32, (NSA_KV_HEADS * tq, 1), 0)
    qblk = (pos_base + (r4 & (tq - 1))) // SLC_BLOCK
    sidx = lax.broadcasted_iota(jnp.int32, (1, LANES), 1)
    bonus = jnp.where(sidx == 0, FORCE_BONUS, jnp.where(sidx == qblk, FORCE_BONUS,
                                                        jnp.where(sidx == qblk - 1, FORCE_BONUS, 0.0)))
    score = jnp.where(sidx <= qblk, imp + bonus, NEG_INF)
    return o, _topk_mask(score, n_sel, tq), qpos


def _gated_sum(gates, gx, o_cmp, o_slc, o_win, bm, tq):
    gf = _mm_split(gates, gx)
    return (gf[:, 0:NSA_Q_W] * _extract(o_cmp, bm, tq) + gf[:, NSA_Q_W:2 * NSA_Q_W] * _extract(o_slc, bm, tq)
            + gf[:, 2 * NSA_Q_W:] * _extract(o_win, bm, tq))


def _nsa_prompt_body(qn_ref, qr_ref, gates_ref, kc_ref, vc_ref, kv_ref, win_ref, ov_ref, gx_ref, o_ref,
                     *, seq, n_sel, kt, wk):
    tq = Q_TILE
    t0 = pl.program_id(1) * tq
    bm = _group_masks()
    w = NSA_KV_W
    o_cmp, sel, _ = _cmp_and_select(_blockdiag(qn_ref[...], bm), kc_ref[0], vc_ref[0], ov_ref[...], t0, tq, n_sel)
    qbd = _blockdiag(qr_ref[...], bm) * ATT_SCALE
    sel_b = sel.astype(MXU_DTYPE)
    sidx = lax.broadcasted_iota(jnp.int32, (LANES, 1), 0)
    qpos_t = t0 + lax.broadcasted_iota(jnp.int32, (tq, 1), 0)
    qpos_gt = t0 + (lax.broadcasted_iota(jnp.int32, (NSA_KV_HEADS * tq, 1), 0) & (tq - 1))

    def kv_tile(jt, carry):
        m, l, acc = carry
        k0 = pl.multiple_of(jt * kt, kt)
        kpos = k0 + lax.broadcasted_iota(jnp.int32, (1, kt), 1)
        expand = jnp.where((kpos // SLC_BLOCK) == sidx, 1.0, 0.0).astype(MXU_DTYPE)
        bias = (jnp.dot(sel_b, expand, preferred_element_type=F32) - 1.0) * (-NEG_INF)
        bias = jnp.where(kpos <= qpos_gt, bias, NEG_INF)
        s = _mm_nt(qbd, kv_ref[pl.ds(k0, kt), 0:w]) + _per_head_rows(bias, tq)
        m_new = jnp.maximum(m, jnp.max(s, axis=-1, keepdims=True))
        alpha = jnp.exp(m - m_new)
        e = jnp.exp(s - m_new)
        l = alpha * l + jnp.sum(e, axis=-1, keepdims=True)
        acc = alpha * acc + _mm(e, kv_ref[pl.ds(k0, kt), w:2 * w])
        return m_new, l, acc

    rows = NSA_HEADS * tq
    n_tiles = (t0 + tq + kt - 1) // kt
    m, l, acc = lax.fori_loop(0, n_tiles, kv_tile, (jnp.full((rows, 1), SOFTMAX_M0, F32), jnp.zeros((rows, 1), F32),
                                                    jnp.zeros((rows, w), F32)))
    o_slc = acc * (1.0 / l)

    start = pl.multiple_of(jnp.maximum(t0 + tq - wk, 0), tq)
    kpos = start + lax.broadcasted_iota(jnp.int32, (1, wk), 1)
    bias = jnp.where(kpos <= qpos_t, jnp.where(kpos > qpos_t - WINDOW, 0.0, NEG_INF), NEG_INF)
    bias = jnp.broadcast_to(bias.reshape(1, tq, wk), (NSA_HEADS, tq, wk)).reshape(rows, wk)
    s = _mm_nt(qbd, win_ref[pl.ds(start, wk), 0:w]) + bias
    e = jnp.exp(s - jnp.max(s, axis=-1, keepdims=True))
    o_win = _mm(e, win_ref[pl.ds(start, wk), w:2 * w]) * (1.0 / jnp.sum(e, axis=-1, keepdims=True))
    o_ref[...] = _gated_sum(gates_ref[...], gx_ref[...], o_cmp, o_slc, o_win, bm, tq).astype(o_ref.dtype)


def nsa_attn_prompt(qn, qr, gates, kc, vc, rows_new, win_new, n_seq, seq):
    tq = Q_TILE
    nb = kc.shape[1]
    n_sel = -(-seq // SLC_BLOCK)
    kt = min(KV_TILE, seq)
    wk = min(WINDOW + tq, seq)
    ov = jnp.asarray(_overlap_matrix(nb, seq))
    gx = jnp.asarray(_gate_expand(), MXU_DTYPE)
    per = seq // tq

    def qrows(width):
        return pl.BlockSpec((tq, width), lambda n, t: (n * per + t, 0))

    def per_seq(shape, lane_block=0):
        return pl.BlockSpec(shape, lambda n, t: (n,) + (0,) * (len(shape) - 2) + (lane_block,))

    return pl.pallas_call(
        functools.partial(_nsa_prompt_body, seq=seq, n_sel=n_sel, kt=kt, wk=wk),
        grid=(n_seq, per),
        in_specs=[qrows(NSA_Q_W), qrows(NSA_Q_W), qrows(LANES), per_seq((1, nb, NSA_KV_W)), per_seq((1, nb, NSA_KV_W)),
                  per_seq((seq, 2 * NSA_KV_W), 1), per_seq((seq, 2 * NSA_KV_W)),
                  _resident((nb, LANES)), _resident((LANES, 3 * NSA_Q_W))],
        out_specs=qrows(NSA_Q_W),
        out_shape=jax.ShapeDtypeStruct((n_seq * seq, NSA_Q_W), MXU_DTYPE),
        compiler_params=_params(2),
        name="nsa_attn_prompt",
    )(qn, qr, gates, kc, vc, rows_new, win_new, ov, gx)


def _nsa_sample_body(pt_ref, *refs, n_pages, past_len, ts, n_sel, has_prev):
    del pt_ref
    page_refs = refs[:n_pages]
    (qn_ref, qr_ref, gates_ref, kc_ref, vc_ref, rnew_ref, wold_ref, wnew_ref, ov_ref, gx_ref, ex_ref,
     o_ref, wout_ref) = refs[n_pages:n_pages + 11] + refs[n_pages + 11 + int(has_prev):]
    tq = TS_PAD
    w = NSA_KV_W
    bm = _group_masks()
    o_cmp, sel, qpos = _cmp_and_select(_blockdiag(qn_ref[0].astype(F32), bm), kc_ref[0], vc_ref[0], ov_ref[...],
                                       past_len, tq, n_sel)
    qbd = _blockdiag(qr_ref[0].astype(F32), bm).astype(MXU_DTYPE)
    pad = jnp.zeros((PAGE_SIZE - tq, w), F32)

    bias = (jnp.dot(sel.astype(MXU_DTYPE), ex_ref[...], preferred_element_type=F32) - 1.0) * (-NEG_INF)
    k_new = jnp.concatenate([rnew_ref[0][:, 0:w], pad], axis=0)
    v_new = jnp.concatenate([rnew_ref[0][:, w:2 * w], pad], axis=0)
    s = jnp.concatenate([_mm(qbd, r[0, 0:w, :]) for r in page_refs] + [_mm_nt(qbd, k_new)], axis=1)
    s = s * ATT_SCALE + _per_head_rows(bias, tq)
    kpos = lax.broadcasted_iota(jnp.int32, (1, (n_pages + 1) * PAGE_SIZE), 1)
    s = jnp.where(kpos <= qpos, s, NEG_INF)
    e = jnp.exp(s - jnp.max(s, axis=-1, keepdims=True))
    acc = _mm(e[:, n_pages * PAGE_SIZE:], v_new)
    for p, r in enumerate(page_refs):
        acc = acc + _mm_nt(e[:, p * PAGE_SIZE:(p + 1) * PAGE_SIZE], r[0, w:2 * w, :])
    o_slc = acc * (1.0 / jnp.sum(e, axis=-1, keepdims=True))

    wlen = wold_ref.shape[2]
    kw_new = jnp.concatenate([wnew_ref[0][:, 0:w], pad], axis=0)
    vw_new = jnp.concatenate([wnew_ref[0][:, w:2 * w], pad], axis=0)
    s = jnp.concatenate([_mm(qbd, wold_ref[0, 0:w, :]), _mm_nt(qbd, kw_new)], axis=1) * ATT_SCALE
    kpos = (past_len - wlen) + lax.broadcasted_iota(jnp.int32, (1, wlen + PAGE_SIZE), 1)
    s = jnp.where(kpos <= qpos, s, NEG_INF)
    s = jnp.where(kpos > qpos - WINDOW, s, NEG_INF)
    e = jnp.exp(s - jnp.max(s, axis=-1, keepdims=True))
    acc = _mm_nt(e[:, 0:wlen], wold_ref[0, w:2 * w, :]) + _mm(e[:, wlen:], vw_new)
    o_win = acc * (1.0 / jnp.sum(e, axis=-1, keepdims=True))

    o_ref[0] = _gated_sum(gates_ref[0], gx_ref[...], o_cmp, o_slc, o_win, bm, tq)

    shifted = pltpu.roll(wold_ref[0], wlen - ts, 1)
    new_rows = jnp.concatenate([wnew_ref[0], jnp.zeros((LANES - tq, 2 * w), F32)], axis=0)
    new_t = jnp.concatenate([new_rows[:, j * LANES:(j + 1) * LANES].T for j in range(2 * w // LANES)], axis=0)
    new_t = pltpu.roll(new_t, LANES - ts, 1)
    lane = lax.broadcasted_iota(jnp.int32, (1, LANES), 1)
    wout_ref[0, :, 0:wlen - LANES] = shifted[:, 0:wlen - LANES]
    wout_ref[0, :, wlen - LANES:wlen] = jnp.where(lane >= LANES - ts, new_t, shifted[:, wlen - LANES:wlen])


def nsa_attn_sample(table, pages, qn, qr, gates, kc, vc, rows_new, win_old, win_new, win_base, past_len, ts,
                    win_prev):
    n_seq, n_pages = table.shape
    nb = kc.shape[1]
    wlen = win_old.shape[2]
    length = past_len + ts
    n_sel = -(-length // SLC_BLOCK)
    n_keys = (n_pages + 1) * PAGE_SIZE
    ov = jnp.asarray(_overlap_matrix(nb, length))
    gx = jnp.asarray(_gate_expand(), MXU_DTYPE)
    ex = jnp.asarray((np.arange(n_keys)[None, :] // SLC_BLOCK == np.arange(LANES)[:, None]).astype(np.float32), MXU_DTYPE)

    def page_spec(p):
        return pl.BlockSpec((1, 2 * NSA_KV_W, PAGE_SIZE), lambda n, pt: (pt[n, p], 1, 0))

    def per_seq(shape, lane_block=0, base=0):
        return pl.BlockSpec(shape, lambda n, pt: (base + n,) + (0,) * (len(shape) - 2) + (lane_block,))

    def const(shape):
        return pl.BlockSpec(shape, lambda n, pt: (0,) * len(shape), pipeline_mode=pl.Buffered(1))

    in_specs = [page_spec(p) for p in range(n_pages)] + [
        per_seq((1, TS_PAD, NSA_Q_W)), per_seq((1, TS_PAD, NSA_Q_W)), per_seq((1, TS_PAD, LANES)),
        per_seq((1, nb, NSA_KV_W)), per_seq((1, nb, NSA_KV_W)), per_seq((1, TS_PAD, 2 * NSA_KV_W), 1),
        per_seq((1, 2 * NSA_KV_W, wlen), 0, win_base), per_seq((1, TS_PAD, 2 * NSA_KV_W)),
        const((nb, LANES)), const((LANES, 3 * NSA_Q_W)), const((LANES, n_keys))]
    operands = [table, *([pages] * n_pages), qn, qr, gates, kc, vc, rows_new, win_old, win_new, ov, gx, ex]
    aliases = {}
    if win_prev is not None:
        in_specs.append(pl.BlockSpec(memory_space=pl.ANY))
        aliases = {len(operands): 1}
        operands.append(win_prev)
    grid_spec = pltpu.PrefetchScalarGridSpec(
        num_scalar_prefetch=1, grid=(n_seq,), in_specs=in_specs,
        out_specs=[per_seq((1, TS_PAD, NSA_Q_W)), per_seq((1, 2 * NSA_KV_W, wlen), 0, win_base)])
    return pl.pallas_call(
        functools.partial(_nsa_sample_body, n_pages=n_pages, past_len=past_len, ts=ts, n_sel=n_sel,
                          has_prev=win_prev is not None),
        grid_spec=grid_spec,
        out_shape=[jax.ShapeDtypeStruct((n_seq, TS_PAD, NSA_Q_W), F32),
                   jax.ShapeDtypeStruct(win_old.shape, F32)],
        input_output_aliases=aliases,
        compiler_params=_params(), name="nsa_attn_sample",
    )(*operands)


S5_NB = 8
S5_GB = 8
S5_HALF = S5_GB * S5_STATE
S5_TIME = 256


def _s5_disc_body(lr_ref, li_ref, ldt_ref, bre_ref, bim_ref, are_ref, aim_ref, bbre_ref, bbim_ref):
    dt = jnp.exp(ldt_ref[...])
    lr = jnp.minimum(lr_ref[...], -1e-4)
    li = li_ref[...]
    mag = jnp.exp(lr * dt)
    a_re = mag * jnp.cos(li * dt)
    a_im = mag * jnp.sin(li * dt)
    den = lr * lr + li * li
    z_re = ((a_re - 1.0) * lr + a_im * li) / den
    z_im = (a_im * lr - (a_re - 1.0) * li) / den
    are_ref[...] = a_re
    aim_ref[...] = a_im
    bbre_ref[...] = z_re * bre_ref[...] - z_im * bim_ref[...]
    bbim_ref[...] = z_re * bim_ref[...] + z_im * bre_ref[...]


def s5_discretize(lam_re, lam_im, log_dt, b_re, b_im):
    rows = S5_GROUPS * S5_GROUP_CH

    def per_channel(a):
        return jnp.broadcast_to(a[:, None, :], (S5_GROUPS, S5_GROUP_CH, S5_STATE)).reshape(rows, S5_STATE)

    args = (per_channel(lam_re), per_channel(lam_im), per_channel(jnp.broadcast_to(log_dt[:, None], lam_re.shape)),
            b_re.transpose(0, 2, 1).reshape(rows, S5_STATE), b_im.transpose(0, 2, 1).reshape(rows, S5_STATE))
    out = jax.ShapeDtypeStruct((rows, S5_STATE), F32)
    a_re, a_im, bb_re, bb_im = pl.pallas_call(_s5_disc_body, out_shape=[out] * 4, name="s5_discretize")(*args)
    shape = (S5_GROUPS, S5_GROUP_CH, S5_STATE)
    return a_re.reshape(shape)[:, 0], a_im.reshape(shape)[:, 0], bb_re.reshape(shape), bb_im.reshape(shape)


def _s5_scan_body(u_ref, bb_ref, cc_ref, a_ref, h0_ref, y_ref, hout_ref, bu_scr, hs_scr, h_scr, *, tc, use_h0):
    t = pl.program_id(2)

    @pl.when(t == 0)
    def _():
        h_scr[...] = h0_ref[0, 0] if use_h0 else jnp.zeros(h_scr.shape, F32)

    bu_scr[...] = _mm(u_ref[...], bb_ref[0])
    ar = a_ref[0][:, :S5_HALF]
    ai = a_ref[0][:, S5_HALF:]

    def step(i, carry):
        hr, hi = carry
        r = pl.multiple_of(i * S5_NB, S5_NB)
        nhr = ar * hr - ai * hi + bu_scr[pl.ds(r, S5_NB), :S5_HALF]
        nhi = ar * hi + ai * hr + bu_scr[pl.ds(r, S5_NB), S5_HALF:]
        hs_scr[pl.ds(r, S5_NB), :S5_HALF] = nhr
        hs_scr[pl.ds(r, S5_NB), S5_HALF:] = nhi
        return nhr, nhi

    hr, hi = lax.fori_loop(0, tc, step, (h_scr[:, :S5_HALF], h_scr[:, S5_HALF:]), unroll=min(tc, 8))
    h_scr[:, :S5_HALF] = hr
    h_scr[:, S5_HALF:] = hi
    y_ref[...] = _mm(hs_scr[...], cc_ref[0])

    @pl.when(t == pl.num_programs(2) - 1)
    def _():
        hout_ref[0, 0] = h_scr[...]


def s5_scan(u, h0, a_re, a_im, bb_re, bb_im, c_re, c_im):
    n_real, t_len, _ = u.shape
    n = -(-n_real // S5_NB) * S5_NB
    if n != n_real:
        u = jnp.pad(u, ((0, n - n_real), (0, 0), (0, 0)))
        h0 = None if h0 is None else jnp.pad(h0, ((0, n - n_real), (0, 0), (0, 0), (0, 0)))
    nb = n // S5_NB
    ngb = S5_GROUPS // S5_GB
    tc = min(S5_TIME, t_len)
    eye = jnp.eye(S5_GB, dtype=F32)

    def blockdiag_in(bb):
        return jnp.einsum('ab,xacp->xacbp', eye, bb.reshape(ngb, S5_GB, S5_GROUP_CH, S5_STATE)).reshape(
            ngb, S5_GB * S5_GROUP_CH, S5_HALF)

    def blockdiag_out(cc):
        return jnp.einsum('ab,xbcp->xapbc', eye, cc.reshape(ngb, S5_GB, S5_GROUP_CH, S5_STATE)).reshape(
            ngb, S5_HALF, S5_GB * S5_GROUP_CH)

    bb = jnp.concatenate([blockdiag_in(bb_re), blockdiag_in(bb_im)], axis=2).astype(MXU_DTYPE)
    cc = jnp.concatenate([blockdiag_out(c_re), blockdiag_out(-c_im)], axis=1).astype(MXU_DTYPE)
    a = jnp.concatenate([a_re.reshape(ngb, S5_HALF), a_im.reshape(ngb, S5_HALF)], axis=1)
    a = jnp.broadcast_to(a[:, None, :], (ngb, S5_NB, 2 * S5_HALF))
    use_h0 = h0 is not None
    if use_h0:
        h0b = h0.reshape(nb, S5_NB, 2, ngb, S5_HALF).transpose(0, 3, 1, 2, 4).reshape(nb, ngb, S5_NB, 2 * S5_HALF)
    else:
        h0b = jnp.zeros((1, 1, S5_NB, 2 * S5_HALF), F32)
    ub = u.reshape(nb, S5_NB, t_len, D_MODEL).transpose(0, 2, 1, 3).reshape(nb * t_len * S5_NB, D_MODEL)
    per = t_len // tc
    lanes_u = S5_GB * S5_GROUP_CH

    rows_spec = pl.BlockSpec((tc * S5_NB, lanes_u), lambda b, g, t: (b * per + t, g))
    state_spec = pl.BlockSpec((1, 1, S5_NB, 2 * S5_HALF), lambda b, g, t: (b, g, 0, 0))
    h0_spec = state_spec if use_h0 else pl.BlockSpec((1, 1, S5_NB, 2 * S5_HALF), lambda b, g, t: (0, 0, 0, 0))
    y, h_last = pl.pallas_call(
        functools.partial(_s5_scan_body, tc=tc, use_h0=use_h0),
        grid=(nb, ngb, per),
        in_specs=[rows_spec, pl.BlockSpec((1, lanes_u, 2 * S5_HALF), lambda b, g, t: (g, 0, 0)),
                  pl.BlockSpec((1, 2 * S5_HALF, lanes_u), lambda b, g, t: (g, 0, 0)),
                  pl.BlockSpec((1, S5_NB, 2 * S5_HALF), lambda b, g, t: (g, 0, 0)), h0_spec],
        out_specs=[rows_spec, state_spec],
        out_shape=[jax.ShapeDtypeStruct((nb * t_len * S5_NB, D_MODEL), F32),
                   jax.ShapeDtypeStruct((nb, ngb, S5_NB, 2 * S5_HALF), F32)],
        scratch_shapes=[pltpu.VMEM((tc * S5_NB, 2 * S5_HALF), F32), pltpu.VMEM((tc * S5_NB, 2 * S5_HALF), F32),
                        pltpu.VMEM((S5_NB, 2 * S5_HALF), F32)],
        compiler_params=pltpu.CompilerParams(dimension_semantics=("parallel", "parallel", "arbitrary"),
                                             vmem_limit_bytes=VMEM_LIMIT),
        name="s5_scan",
    )(ub, bb, cc, a, h0b)
    y = y.reshape(nb, t_len, S5_NB, D_MODEL).transpose(0, 2, 1, 3).reshape(n, t_len, D_MODEL)
    h_last = h_last.reshape(nb, ngb, S5_NB, 2, S5_GB, S5_STATE).transpose(0, 2, 3, 1, 4, 5).reshape(
        n, 2, S5_GROUPS, S5_STATE)
    return y[:n_real], h_last[:n_real]


def _s5_out_body(x_ref, y_ref, u_ref, d_ref, w_ref, o_ref):
    z = jax.nn.gelu(y_ref[...] + d_ref[...] * u_ref[...])
    ab = _mm(z, w_ref[...])
    o_ref[...] = x_ref[...] + ab[:, :D_MODEL] * jax.nn.sigmoid(ab[:, D_MODEL:])


def s5_out(x, y, u, d_skip, w_glu):
    rows = x.shape[0]
    tm = _row_tile(rows)
    return pl.pallas_call(
        _s5_out_body,
        grid=(rows // tm,),
        in_specs=[_rows(tm, D_MODEL), _rows(tm, D_MODEL), _rows(tm, D_MODEL), _resident((1, D_MODEL)),
                  _resident((D_MODEL, 2 * D_MODEL))],
        out_specs=_rows(tm, D_MODEL),
        out_shape=jax.ShapeDtypeStruct((rows, D_MODEL), F32),
        compiler_params=_params(),
        name="s5_out",
    )(x, y, u, d_skip.reshape(1, -1), w_glu.astype(MXU_DTYPE))


HG_SUB = 16
HG_TIME = 256
HG_HEADS_PER_STEP = 2


def _mm_exact(l01, x):
    x1 = x.astype(MXU_DTYPE)
    r1 = x - x1.astype(F32)
    x2 = r1.astype(MXU_DTYPE)
    x3 = (r1 - x2.astype(F32)).astype(MXU_DTYPE)
    dot = functools.partial(jnp.dot, preferred_element_type=F32)
    return dot(l01, x1) + dot(l01, x2) + dot(l01, x3)


def _hgrn_body(q_ref, fz_ref, v_ref, g_ref, lb_ref, og_ref, s0_ref, o_ref, sout_ref, s_scr,
               *, tb, chunk, sub, valid, use_s0, hps):
    tblk = pl.program_id(2)

    @pl.when(tblk == 0)
    def _():
        s_scr[...] = s0_ref[0] if use_s0 else jnp.zeros(s_scr.shape, F32)

    og = og_ref[...]
    eye = (lax.broadcasted_iota(jnp.int32, (HG_DK, HG_DK), 0) == lax.broadcasted_iota(jnp.int32, (HG_DK, HG_DK), 1))
    tril = jnp.where(lax.broadcasted_iota(jnp.int32, (chunk, chunk), 0)
                     >= lax.broadcasted_iota(jnp.int32, (chunk, chunk), 1), 1.0, 0.0).astype(MXU_DTYPE)
    trow = lax.broadcasted_iota(jnp.int32, (sub, 1), 0)
    nsub = chunk // sub

    for hh, ci in itertools.product(range(hps), range(tb // chunk)):
        r0 = ci * chunk
        cols = slice(hh * HG_DK, (hh + 1) * HG_DK)
        lb = lb_ref[:, cols]
        q = q_ref[r0:r0 + chunk, cols]
        v = v_ref[r0:r0 + chunk, cols]
        f = lb + (1.0 - lb) * jax.nn.sigmoid(fz_ref[r0:r0 + chunk, cols])
        k = 1.0 - f
        lf = jnp.log(f)
        if valid < tb:
            live = (r0 + lax.broadcasted_iota(jnp.int32, (chunk, 1), 0)) < valid
            k = jnp.where(live, k, 0.0)
            lf = jnp.where(live, lf, 0.0)
        gcum = _mm_exact(tril, lf)
        state = s_scr[hh]
        o_inter = _mm(q * jnp.exp(gcum), state)
        o_blocks = [o_inter[i * sub:(i + 1) * sub] for i in range(nsub)]

        for j in range(nsub - 1):
            lo, hi = j * sub, (j + 1) * sub
            g_ref_row = gcum[hi - 1:hi, :]
            k_t = k[lo:hi] * jnp.exp(g_ref_row - gcum[lo:hi])
            q_t = q[hi:] * jnp.exp(gcum[hi:] - g_ref_row)
            contrib = _mm(_mm_nt(q_t, k_t), v[lo:hi])
            for i in range(j + 1, nsub):
                o_blocks[i] = o_blocks[i] + contrib[(i - j - 1) * sub:(i - j) * sub]

        for i in range(nsub):
            lo, hi = i * sub, (i + 1) * sub
            q_i, k_i, v_i, g_i = q[lo:hi], k[lo:hi], v[lo:hi], gcum[lo:hi]
            acc = jnp.zeros((sub, HG_DV), F32)
            for s in range(sub):
                decay = jnp.exp(jnp.minimum(g_i - g_i[s:s + 1], 0.0))
                wgt = jnp.sum(q_i * k_i[s:s + 1] * decay, axis=1, keepdims=True)
                acc = acc + jnp.where(trow >= s, wgt, 0.0) * v_i[s:s + 1]
            o_blocks[i] = o_blocks[i] + acc

        g_last = gcum[chunk - 1:chunk, :]
        k_t = k * jnp.exp(g_last - gcum)
        decay_col = jnp.sum(jnp.where(eye, jnp.exp(g_last), 0.0), axis=1, keepdims=True)
        kv = lax.dot_general(k_t.astype(MXU_DTYPE), v.astype(MXU_DTYPE), (((0,), (0,)), ((), ())),
                             preferred_element_type=F32)
        s_scr[hh] = decay_col * state + kv

        o = jnp.concatenate(o_blocks, axis=0)
        o = o * lax.rsqrt(jnp.mean(o * o, axis=-1, keepdims=True) + RMS_EPS) * og
        gate = g_ref[r0:r0 + chunk, cols]
        o_ref[r0:r0 + chunk, cols] = o * (gate * jax.nn.sigmoid(gate))

    @pl.when(tblk == pl.num_programs(2) - 1)
    def _():
        sout_ref[0] = s_scr[...]


def hgrn_scan(pr, s0, o_gain, lb, n_seq, t_rows, valid):
    tb = min(HG_TIME, t_rows)
    chunk = min(HG_CHUNK, tb)
    sub = min(HG_SUB, chunk)
    per = t_rows // tb
    hps = HG_HEADS if tb < HG_TIME else HG_HEADS_PER_STEP
    hblocks = HG_HEADS // hps
    use_s0 = s0 is not None
    if not use_s0:
        s0 = jnp.zeros((1, hps, HG_DK, HG_DV), F32)

    def part(idx):
        return pl.BlockSpec((tb, hps * HG_DK), lambda n, h, t: (n * per + t, idx * hblocks + h))

    head_vec = pl.BlockSpec((1, hps * HG_DK), lambda n, h, t: (0, h))
    state_spec = pl.BlockSpec((1, hps, HG_DK, HG_DV), lambda n, h, t: (n, h, 0, 0))
    s0_spec = state_spec if use_s0 else pl.BlockSpec((1, hps, HG_DK, HG_DV), lambda n, h, t: (0, 0, 0, 0))
    return pl.pallas_call(
        functools.partial(_hgrn_body, tb=tb, chunk=chunk, sub=sub, valid=valid, use_s0=use_s0, hps=hps),
        grid=(n_seq, hblocks, per),
        in_specs=[part(0), part(1), part(2), part(3), head_vec, pl.BlockSpec((1, HG_DV), lambda n, h, t: (0, 0)), s0_spec],
        out_specs=[pl.BlockSpec((tb, hps * HG_DV), lambda n, h, t: (n * per + t, h)), state_spec],
        out_shape=[jax.ShapeDtypeStruct((n_seq * t_rows, D_MODEL), F32),
                   jax.ShapeDtypeStruct((n_seq, HG_HEADS, HG_DK, HG_DV), F32)],
        scratch_shapes=[pltpu.VMEM((hps, HG_DK, HG_DV), F32)],
        compiler_params=pltpu.CompilerParams(dimension_semantics=("parallel", "parallel", "arbitrary"),
                                             vmem_limit_bytes=VMEM_LIMIT),
        name="hgrn_scan",
    )(pr, pr, pr, pr, lb.reshape(1, -1), o_gain.reshape(1, -1), s0)


def kernel(x_prompt, x_sample, cache_nsa, state_nsa_win, state_s5, state_hgrn, page_table, p_prompt, p_sample, norm_gain, ffn_w_in, ffn_w_out, ple_w_gate, ple_w_proj, nsa_w_in, nsa_w_out, nsa_qk_gain, nsa_cmp_pe, nsa_cmp_w, s5_lam_re, s5_lam_im, s5_log_dt, s5_b_re, s5_b_im, s5_c_re, s5_c_im, s5_d, s5_w_glu, hg_w_in, hg_w_out, hg_o_gain, hg_lb_raw):
    B, T, _ = x_prompt.shape
    Bs, Ts, _ = x_sample.shape
    rp = B * T
    rs = Bs * Ts
    n_pages = page_table.shape[1]
    past_len = n_pages * PAGE_SIZE
    n_phys = cache_nsa.shape[1]
    wlen = state_nsa_win.shape[2]
    assert T % Q_TILE == 0 and T % PAGE_SIZE == 0 and T % min(KV_TILE, T) == 0 and Ts <= TS_PAD
    lb_sm = jax.nn.softmax(hg_lb_raw.astype(F32), axis=0)
    lower_bounds = jnp.cumsum(lb_sm, axis=0) - lb_sm[0]

    x = jnp.concatenate([x_prompt.reshape(rp, D_MODEL), x_sample.reshape(rs, D_MODEL)], axis=0)
    p_all = jnp.concatenate([p_prompt.reshape(DEPTH, rp, PLE_DIM), p_sample.reshape(DEPTH, rs, PLE_DIM)], axis=1)
    pos = jnp.concatenate([jnp.tile(jnp.arange(T), B), jnp.tile(past_len + jnp.arange(Ts), Bs)])
    cache_pages = cache_nsa.transpose(0, 1, 3, 4, 5, 2).reshape(-1, 4 * NSA_KV_W, PAGE_SIZE)
    win_state = state_nsa_win.transpose(0, 1, 3, 4, 5, 2).reshape(-1, 2 * NSA_KV_W, wlen)
    prompt_table = (jnp.arange(B, dtype=jnp.int32)[:, None] * (T // PAGE_SIZE)
                    + jnp.arange(T // PAGE_SIZE, dtype=jnp.int32)[None, :])
    out_perm = _head_perm()

    def split(a):
        return a[:rp].reshape(B, T, -1), a[rp:].reshape(Bs, Ts, -1)

    def join(a, b):
        return jnp.concatenate([a.reshape(rp, -1), b.reshape(rs, -1)], axis=0)

    def sample_pad(a):
        return jnp.pad(a[rp:].reshape(Bs, Ts, -1), ((0, 0), (0, TS_PAD - Ts), (0, 0)))

    outs_p = {0: [], 1: [], 2: [], 3: []}
    outs_s = {0: [], 1: [], 2: [], 3: []}
    win_all = None
    for i in range(DEPTH):
        kind = LAYER_KIND[i]
        j = LAYER_SLOT[i]
        g = norm_gain[i]
        x, xn = ffn_step(x, g[0], g[1], ffn_w_in[i, 0], ffn_w_out[i, 0])
        if kind == 0:
            qn, qr, rows_new, win_new, gates = nsa_proj(xn, nsa_w_in[j], nsa_qk_gain[j], pos)
            cmp_args = (nsa_cmp_pe[j], nsa_cmp_w[j], nsa_qk_gain[j, 1])
            kc_p, vc_p = nsa_compress(rows_new.reshape(-1, PAGE_SIZE, 4 * NSA_KV_W), prompt_table, *cmp_args, False)
            o_p = nsa_attn_prompt(qn, qr, gates, kc_p, vc_p, rows_new, win_new, B, T)
            table = page_table.astype(jnp.int32) + j * n_phys
            kc_s, vc_s = nsa_compress(cache_pages, table, *cmp_args, True)
            o_s, win_all = nsa_attn_sample(table, cache_pages, sample_pad(qn), sample_pad(qr), sample_pad(gates), kc_s,
                                           vc_s, sample_pad(rows_new), win_state, sample_pad(win_new), j * Bs, past_len,
                                           Ts, win_all)
            r_p, r_s = split(rows_new)
            outs_p[0].append(r_p.reshape(B, T, 4, NSA_KV_HEADS, HEAD_DIM))
            outs_s[0].append(r_s.reshape(Bs, Ts, 4, NSA_KV_HEADS, HEAD_DIM))
            buf = min(WINDOW, T)
            outs_p[1].append(win_new[:rp].reshape(B, T, 2, NSA_KV_HEADS, HEAD_DIM)[:, T - buf:])
            o_all = jnp.concatenate([o_p, o_s[:, :Ts].reshape(rs, NSA_Q_W).astype(o_p.dtype)], axis=0)
            x = resid_proj(x, o_all, nsa_w_out[j][out_perm])
        elif kind == 1:
            u_p, u_s = split(xn)
            disc = s5_discretize(s5_lam_re[j], s5_lam_im[j], s5_log_dt[j], s5_b_re[j], s5_b_im[j])
            y_p, h_p = s5_scan(u_p, None, *disc, s5_c_re[j], s5_c_im[j])
            y_s, h_s = s5_scan(u_s, state_s5[j], *disc, s5_c_re[j], s5_c_im[j])
            outs_p[2].append(h_p); outs_s[2].append(h_s)
            x = s5_out(x, join(y_p, y_s), xn, s5_d[j], s5_w_glu[j])
        else:
            pr = proj(xn, hg_w_in[j])
            pr_s = jnp.pad(pr[rp:].reshape(Bs, Ts, -1), ((0, 0), (0, TS_PAD - Ts), (0, 0))).reshape(Bs * TS_PAD, -1)
            o_p, s_p = hgrn_scan(pr, None, hg_o_gain[j], lower_bounds[i], B, T, T)
            o_s, s_s = hgrn_scan(pr_s, state_hgrn[j], hg_o_gain[j], lower_bounds[i], Bs, TS_PAD, Ts)
            outs_p[3].append(s_p); outs_s[3].append(s_s)
            o_s = o_s.reshape(Bs, TS_PAD, D_MODEL)[:, :Ts].reshape(rs, D_MODEL)
            x = resid_proj(x, jnp.concatenate([o_p, o_s], axis=0), hg_w_out[j])
        x = ffn_ple_step(x, g[2], g[3], ffn_w_in[i, 1], ffn_w_out[i, 1], p_all[i], ple_w_gate[i], ple_w_proj[i])

    y_p, y_s = split(x)
    win_out = win_all.reshape(-1, Bs, 2, NSA_KV_HEADS, HEAD_DIM, wlen).transpose(0, 1, 5, 2, 3, 4)
    return (y_p, y_s,
            jnp.stack(outs_p[0]), jnp.stack(outs_p[1]), jnp.stack(outs_p[2]), jnp.stack(outs_p[3]),
            jnp.stack(outs_s[0]), win_out, jnp.stack(outs_s[2]), jnp.stack(outs_s[3]))
```

```python
import functools
import itertools

import numpy as np
import jax
import jax.numpy as jnp
from jax import lax
from jax.experimental import pallas as pl
from jax.experimental.pallas import tpu as pltpu

F32 = jnp.float32
MXU_DTYPE = jnp.bfloat16

D_MODEL = 1024
DEPTH = 4
PAGE_SIZE = 128
D_FF = 2816
PLE_DIM = 256
RMS_EPS = 1e-6
LAYER_KIND = (0, 1, 2, 0)
LAYER_SLOT = (0, 0, 0, 1)

NSA_HEADS = 16
NSA_KV_HEADS = 4
HEAD_DIM = 64
NSA_HPG = NSA_HEADS // NSA_KV_HEADS
NSA_KV_W = NSA_KV_HEADS * HEAD_DIM
NSA_Q_W = NSA_HEADS * HEAD_DIM
NSA_IN = NSA_Q_W + 6 * NSA_KV_W + 3 * NSA_HEADS
CMP_BLOCK = 32
CMP_STRIDE = 16
SLC_BLOCK = 64
SLC_TOPN = 16
WINDOW = 512
FORCE_BONUS = 1e4
NEG_INF = -1e30
ROPE_THETA = 500000.0
ROPE_DIMS = HEAD_DIM // 4
ATT_SCALE = HEAD_DIM ** -0.5

S5_GROUP_CH = 16
S5_GROUPS = D_MODEL // S5_GROUP_CH
S5_STATE = 64

HG_DK = 128
HG_HEADS = D_MODEL // HG_DK
HG_DV = D_MODEL // HG_HEADS
HG_CHUNK = 128

V7X_VMEM_BYTES = 64 * 1024 * 1024
VMEM_LIMIT = V7X_VMEM_BYTES - 8 * 1024 * 1024
LANES = 128
ROW_TILE = 512
FF_CHUNK = 256
Q_TILE = 128
KV_TILE = 512
TS_PAD = 8
CMP_PITCH = 20
SOFTMAX_M0 = -1e29


def _resident(shape):
    return pl.BlockSpec(shape, lambda *_: (0,) * len(shape), pipeline_mode=pl.Buffered(1))


def _row_tile(rows):
    return max(t for t in range(8, ROW_TILE + 1, 8) if rows % t == 0)


def _rows(tm, width):
    return pl.BlockSpec((tm, width), lambda i: (i, 0))


def _params(n_axes=1):
    return pltpu.CompilerParams(dimension_semantics=("parallel",) * n_axes, vmem_limit_bytes=VMEM_LIMIT)


def _rms(x, g):
    return x * lax.rsqrt(jnp.mean(x * x, axis=-1, keepdims=True) + RMS_EPS) * g


def _mm(a, b):
    return jnp.dot(a.astype(MXU_DTYPE), b.astype(MXU_DTYPE), preferred_element_type=F32)


def _mm_nt(a, b):
    return lax.dot_general(a.astype(MXU_DTYPE), b.astype(MXU_DTYPE), (((1,), (1,)), ((), ())),
                           preferred_element_type=F32)


def _mm_split(a, b):
    hi = a.astype(MXU_DTYPE)
    lo = (a - hi.astype(F32)).astype(MXU_DTYPE)
    return (jnp.dot(hi, b, preferred_element_type=F32) + jnp.dot(lo, b, preferred_element_type=F32))


def _ffn_body(x_ref, g_ref, gn_ref, win_ref, wout_ref, o_ref, on_ref, h_ref):
    x = x_ref[...]
    xb = _rms(x, g_ref[...]).astype(MXU_DTYPE)
    for c in range(D_FF // FF_CHUNK):
        lo = c * FF_CHUNK
        a = _mm(xb, win_ref[:, lo:lo + FF_CHUNK])
        b = _mm(xb, win_ref[:, D_FF + lo:D_FF + lo + FF_CHUNK])
        h_ref[:, lo:lo + FF_CHUNK] = (a * jax.nn.sigmoid(a) * b).astype(MXU_DTYPE)
    y = x + 0.5 * _mm(h_ref[...], wout_ref[...])
    o_ref[...] = y
    on_ref[...] = _rms(y, gn_ref[...])


def ffn_step(x, g, g_next, w_in, w_out):
    rows = x.shape[0]
    tm = _row_tile(rows)
    out = jax.ShapeDtypeStruct((rows, D_MODEL), F32)
    return pl.pallas_call(
        _ffn_body,
        grid=(rows // tm,),
        in_specs=[_rows(tm, D_MODEL), _resident((1, D_MODEL)), _resident((1, D_MODEL)),
                  _resident((D_MODEL, 2 * D_FF)), _resident((D_FF, D_MODEL))],
        out_specs=[_rows(tm, D_MODEL), _rows(tm, D_MODEL)],
        out_shape=[out, out],
        scratch_shapes=[pltpu.VMEM((tm, D_FF), MXU_DTYPE)],
        compiler_params=_params(),
        name="ffn_step",
    )(x, g.reshape(1, -1), g_next.reshape(1, -1), w_in.astype(MXU_DTYPE), w_out.astype(MXU_DTYPE))


def _ffn_ple_body(x_ref, g_ref, gn_ref, win_ref, wout_ref, p_ref, wg_ref, wp_ref, o_ref, h_ref):
    x = x_ref[...]
    xb = _rms(x, g_ref[...]).astype(MXU_DTYPE)
    for c in range(D_FF // FF_CHUNK):
        lo = c * FF_CHUNK
        a = _mm(xb, win_ref[:, lo:lo + FF_CHUNK])
        b = _mm(xb, win_ref[:, D_FF + lo:D_FF + lo + FF_CHUNK])
        h_ref[:, lo:lo + FF_CHUNK] = (a * jax.nn.sigmoid(a) * b).astype(MXU_DTYPE)
    y = x + 0.5 * _mm(h_ref[...], wout_ref[...])
    gate = jax.nn.sigmoid(_mm(_rms(y, gn_ref[...]), wg_ref[...]))
    o_ref[...] = y + gate * _mm(p_ref[...], wp_ref[...])


def ffn_ple_step(x, g, g_ple, w_in, w_out, p, w_gate, w_proj):
    rows = x.shape[0]
    tm = _row_tile(rows)
    return pl.pallas_call(
        _ffn_ple_body,
        grid=(rows // tm,),
        in_specs=[_rows(tm, D_MODEL), _resident((1, D_MODEL)), _resident((1, D_MODEL)),
                  _resident((D_MODEL, 2 * D_FF)), _resident((D_FF, D_MODEL)), _rows(tm, PLE_DIM),
                  _resident((D_MODEL, D_MODEL)), _resident((PLE_DIM, D_MODEL))],
        out_specs=_rows(tm, D_MODEL),
        out_shape=jax.ShapeDtypeStruct((rows, D_MODEL), F32),
        scratch_shapes=[pltpu.VMEM((tm, D_FF), MXU_DTYPE)],
        compiler_params=_params(),
        name="ffn_ple_step",
    )(x, g.reshape(1, -1), g_ple.reshape(1, -1), w_in.astype(MXU_DTYPE), w_out.astype(MXU_DTYPE), p,
      w_gate.astype(MXU_DTYPE), w_proj.astype(MXU_DTYPE))


def _proj_body(a_ref, w_ref, o_ref):
    o_ref[...] = _mm(a_ref[...], w_ref[...])


def proj(a, w):
    rows, k = a.shape
    n = w.shape[1]
    tm = _row_tile(rows)
    return pl.pallas_call(
        _proj_body,
        grid=(rows // tm,),
        in_specs=[_rows(tm, k), _resident((k, n))],
        out_specs=_rows(tm, n),
        out_shape=jax.ShapeDtypeStruct((rows, n), F32),
        compiler_params=_params(),
        name="proj",
    )(a, w.astype(MXU_DTYPE))


def _resid_body(x_ref, a_ref, w_ref, o_ref):
    o_ref[...] = x_ref[...] + _mm(a_ref[...], w_ref[...])


def resid_proj(x, a, w):
    rows, k = a.shape
    tm = _row_tile(rows)
    return pl.pallas_call(
        _resid_body,
        grid=(rows // tm,),
        in_specs=[_rows(tm, D_MODEL), _rows(tm, k), _resident((k, D_MODEL))],
        out_specs=_rows(tm, D_MODEL),
        out_shape=jax.ShapeDtypeStruct((rows, D_MODEL), F32),
        compiler_params=_params(),
        name="resid_proj",
    )(x, a, w.astype(MXU_DTYPE))


def _head_perm():
    idx = np.arange(NSA_Q_W).reshape(NSA_KV_HEADS, NSA_HPG, HEAD_DIM)
    return idx.transpose(1, 0, 2).reshape(-1)


def _gate_expand():
    x = np.zeros((LANES, 3 * NSA_Q_W), np.float32)
    for b in range(3):
        for g in range(NSA_KV_HEADS):
            for j in range(NSA_HPG):
                h = g * NSA_HPG + j
                c0 = b * NSA_Q_W + (j * NSA_KV_HEADS + g) * HEAD_DIM
                x[b * NSA_HEADS + h, c0:c0 + HEAD_DIM] = 1.0
    return x


def _overlap_matrix(nb, length):
    n_sel = -(-length // SLC_BLOCK)
    c0 = np.arange(nb)[:, None] * CMP_STRIDE
    s0 = np.arange(LANES)[None, :] * SLC_BLOCK
    ov = np.clip(np.minimum(c0 + CMP_BLOCK, s0 + SLC_BLOCK) - np.maximum(c0, s0), 0, None) / CMP_BLOCK
    ov = np.where(np.arange(LANES)[None, :] < n_sel, ov, 0.0)
    return ov.astype(np.float32)


def _rope_tables(pos):
    half = ROPE_DIMS // 2
    inv = ROPE_THETA ** (-jnp.arange(half, dtype=F32) / half)
    ang = pos.astype(F32)[:, None] * inv[None, :]
    cos, sin = jnp.cos(ang), jnp.sin(ang)
    ones = jnp.ones((pos.shape[0], HEAD_DIM - ROPE_DIMS), F32)
    zeros = jnp.zeros((pos.shape[0], HEAD_DIM - ROPE_DIMS), F32)
    zh = jnp.zeros_like(sin)
    c = jnp.concatenate([cos, cos, ones], axis=1)
    sa = jnp.concatenate([-sin, zh, zeros], axis=1)
    sb = jnp.concatenate([zh, sin, zeros], axis=1)
    rep = LANES // HEAD_DIM
    return jnp.tile(c, (1, rep)), jnp.tile(sa, (1, rep)), jnp.tile(sb, (1, rep))


def _nsa_proj_body(xn_ref, wq_ref, wkv_ref, wgl_ref, b64_ref, gq_ref, gk_ref, c_ref, sa_ref, sb_ref,
                   qn_ref, qr_ref, rows_ref, win_ref, gates_ref):
    xb = xn_ref[...].astype(MXU_DTYPE)
    c, sa, sb = c_ref[...], sa_ref[...], sb_ref[...]

    def head_norm(v, gain):
        w = v.shape[1]
        ms = _mm_split(v * v, b64_ref[:w, :w])
        return v * lax.rsqrt(ms + RMS_EPS) * gain

    def rope(v):
        w = v.shape[1]
        rep = w // LANES
        ct, sat, sbt = (jnp.concatenate([t] * rep, axis=1) for t in (c, sa, sb))
        return v * ct + pltpu.roll(v, w - ROPE_DIMS // 2, 1) * sat + pltpu.roll(v, ROPE_DIMS // 2, 1) * sbt

    qn = head_norm(_mm(xb, wq_ref[...]), gq_ref[...])
    qn_ref[...] = qn.astype(qn_ref.dtype)
    qr_ref[...] = rope(qn).astype(qr_ref.dtype)
    kv = _mm(xb, wkv_ref[...])
    w = NSA_KV_W
    rows_ref[:, 0:2 * w] = kv[:, 0:2 * w]
    rows_ref[:, 2 * w:3 * w] = rope(head_norm(kv[:, 2 * w:3 * w], gk_ref[0:1, :]))
    rows_ref[:, 3 * w:4 * w] = kv[:, 3 * w:4 * w]
    win_ref[:, 0:w] = rope(head_norm(kv[:, 4 * w:5 * w], gk_ref[1:2, :]))
    win_ref[:, w:2 * w] = kv[:, 5 * w:6 * w]
    gates_ref[...] = jax.nn.sigmoid(_mm(xb, wgl_ref[...]))


def nsa_proj(xn, w_in, qk_gain, pos):
    rows = xn.shape[0]
    tm = _row_tile(rows)
    kvw = 6 * NSA_KV_W
    wq = w_in[:, :NSA_Q_W][:, _head_perm()].astype(MXU_DTYPE)
    wkv = w_in[:, NSA_Q_W:NSA_Q_W + kvw].astype(MXU_DTYPE)
    wgl = jnp.pad(w_in[:, NSA_Q_W + kvw:], ((0, 0), (0, LANES - 3 * NSA_HEADS))).astype(MXU_DTYPE)
    b64 = jnp.asarray(np.kron(np.eye(NSA_HEADS), np.full((HEAD_DIM, HEAD_DIM), 1.0 / HEAD_DIM)), MXU_DTYPE)
    gq = jnp.tile(qk_gain[0], NSA_HEADS).reshape(1, -1)
    gk = jnp.stack([jnp.tile(qk_gain[2], NSA_KV_HEADS), jnp.tile(qk_gain[3], NSA_KV_HEADS)])
    c, sa, sb = _rope_tables(pos)
    return pl.pallas_call(
        _nsa_proj_body,
        grid=(rows // tm,),
        in_specs=[_rows(tm, D_MODEL), _resident((D_MODEL, NSA_Q_W)), _resident((D_MODEL, kvw)),
                  _resident((D_MODEL, LANES)), _resident((NSA_Q_W, NSA_Q_W)), _resident((1, NSA_Q_W)),
                  _resident((2, NSA_KV_W)), _rows(tm, LANES), _rows(tm, LANES), _rows(tm, LANES)],
        out_specs=[_rows(tm, NSA_Q_W), _rows(tm, NSA_Q_W), _rows(tm, 4 * NSA_KV_W), _rows(tm, 2 * NSA_KV_W),
                   _rows(tm, LANES)],
        out_shape=[jax.ShapeDtypeStruct((rows, NSA_Q_W), MXU_DTYPE), jax.ShapeDtypeStruct((rows, NSA_Q_W), MXU_DTYPE),
                   jax.ShapeDtypeStruct((rows, 4 * NSA_KV_W), F32), jax.ShapeDtypeStruct((rows, 2 * NSA_KV_W), F32),
                   jax.ShapeDtypeStruct((rows, LANES), F32)],
        compiler_params=_params(),
        name="nsa_proj",
    )(xn, wq, wkv, wgl, b64, gq, gk, c, sa, sb)


def _nsa_compress_body(pt_ref, *refs, n_pages, seqs, transposed):
    del pt_ref
    page_refs = refs[:seqs * n_pages]
    wk_ref, wv_ref, pek_ref, pev_ref, b64_ref, gk_ref, kc_ref, vc_ref, stage = refs[seqs * n_pages:]
    per_page = PAGE_SIZE // CMP_STRIDE
    nb_seq = n_pages * per_page
    nb = seqs * nb_seq
    tiles = 2 * NSA_KV_W // LANES

    for p, r in enumerate(page_refs):
        for j in range(tiles):
            if transposed:
                tile = r[0, j * LANES:(j + 1) * LANES, :].T
            else:
                tile = r[0, :, j * LANES:(j + 1) * LANES]
            for c in range(per_page):
                row = (p * per_page + c) * CMP_PITCH
                stage[j, row:row + CMP_STRIDE, :] = tile[c * CMP_STRIDE:(c + 1) * CMP_STRIDE]

    def compress(kind, w_ref, pe_ref):
        per_kind = NSA_KV_W // LANES
        first = jnp.zeros((nb, NSA_KV_W), F32)
        second = jnp.zeros((nb, NSA_KV_W), F32)
        for l in range(CMP_STRIDE):
            x = jnp.concatenate([stage[kind * per_kind + j, pl.ds(l, nb, stride=CMP_PITCH), :]
                                 for j in range(per_kind)], axis=1)
            first = first + _mm(x + pe_ref[l:l + 1, :], w_ref[l])
            second = second + _mm(x + pe_ref[CMP_STRIDE + l:CMP_STRIDE + l + 1, :], w_ref[CMP_STRIDE + l])
        return first + pltpu.roll(second, nb - 1, 0)

    kc = compress(0, wk_ref, pek_ref)
    ms = _mm_split(kc * kc, b64_ref[...])
    kc = kc * lax.rsqrt(ms + RMS_EPS) * gk_ref[...]
    vc = compress(1, wv_ref, pev_ref)
    for q in range(seqs):
        kc_ref[q] = kc[q * nb_seq:(q + 1) * nb_seq]
        vc_ref[q] = vc[q * nb_seq:(q + 1) * nb_seq]


def nsa_compress(pages, table, cmp_pe, cmp_w, k_gain, transposed):
    n_seq, n_pages = table.shape
    nb = n_pages * (PAGE_SIZE // CMP_STRIDE)
    eye = jnp.eye(NSA_KV_HEADS, dtype=F32)
    w4 = jnp.einsum('gh,klde->klgdhe', eye, cmp_w).reshape(2, CMP_BLOCK, NSA_KV_W, NSA_KV_W).astype(MXU_DTYPE)
    pe4 = jnp.tile(cmp_pe, (1, 1, NSA_KV_HEADS))
    b64 = jnp.asarray(np.kron(np.eye(NSA_KV_HEADS), np.full((HEAD_DIM, HEAD_DIM), 1.0 / HEAD_DIM)), MXU_DTYPE)
    gk = jnp.tile(k_gain, NSA_KV_HEADS).reshape(1, -1)

    page_block = (1, 2 * NSA_KV_W, PAGE_SIZE) if transposed else (1, PAGE_SIZE, 2 * NSA_KV_W)

    seqs = 2 if n_seq % 2 == 0 else 1

    def page_spec(q, p):
        return pl.BlockSpec(page_block, lambda n, pt: (pt[n * seqs + q, p], 0, 0))

    def const(shape):
        return pl.BlockSpec(shape, lambda n, pt: (0,) * len(shape), pipeline_mode=pl.Buffered(1))

    out = jax.ShapeDtypeStruct((n_seq, nb, NSA_KV_W), F32)
    out_spec = pl.BlockSpec((seqs, nb, NSA_KV_W), lambda n, pt: (n, 0, 0))
    grid_spec = pltpu.PrefetchScalarGridSpec(
        num_scalar_prefetch=1, grid=(n_seq // seqs,),
        in_specs=[page_spec(q, p) for q in range(seqs) for p in range(n_pages)] + [
            const((CMP_BLOCK, NSA_KV_W, NSA_KV_W)), const((CMP_BLOCK, NSA_KV_W, NSA_KV_W)),
            const((CMP_BLOCK, NSA_KV_W)), const((CMP_BLOCK, NSA_KV_W)), const((NSA_KV_W, NSA_KV_W)),
            const((1, NSA_KV_W))],
        out_specs=[out_spec, out_spec],
        scratch_shapes=[pltpu.VMEM((2 * NSA_KV_W // LANES, seqs * nb * CMP_PITCH, LANES), F32)])
    return pl.pallas_call(
        functools.partial(_nsa_compress_body, n_pages=n_pages, seqs=seqs, transposed=transposed),
        grid_spec=grid_spec, out_shape=[out, out], compiler_params=_params(), name="nsa_compress",
    )(table, *([pages] * (seqs * n_pages)), w4[0], w4[1], pe4[0], pe4[1], b64, gk)


def _group_masks():
    lane = lax.broadcasted_iota(jnp.int32, (1, NSA_KV_W), 1)
    return [(lane // HEAD_DIM) == g for g in range(NSA_KV_HEADS)]


def _blockdiag(q, bm):
    zero = jnp.zeros((), q.dtype)
    return jnp.concatenate([jnp.where(bm[g], q[:, NSA_KV_W * j:NSA_KV_W * (j + 1)], zero)
                            for g in range(NSA_KV_HEADS) for j in range(NSA_HPG)], axis=0)


def _extract(obd, bm, tq):
    outs = []
    for j in range(NSA_HPG):
        z = jnp.zeros((tq, NSA_KV_W), F32)
        for g in range(NSA_KV_HEADS):
            r0 = (g * NSA_HPG + j) * tq
            z = z + jnp.where(bm[g], obd[r0:r0 + tq], 0.0)
        outs.append(z)
    return jnp.concatenate(outs, axis=1)


def _per_head_rows(a, tq):
    k = a.shape[1]
    a4 = jnp.broadcast_to(a.reshape(NSA_KV_HEADS, 1, tq, k), (NSA_KV_HEADS, NSA_HPG, tq, k))
    return a4.reshape(NSA_HEADS * tq, k)


def _topk_mask(score, n_sel, tq):
    if tq == LANES:
        nsp = -(-n_sel // 8) * 8
        sub = lax.broadcasted_iota(jnp.int32, (nsp, 1), 0)
        outs = []
        for g in range(NSA_KV_HEADS):
            st = score[g * tq:(g + 1) * tq].T[:nsp]
            rank = jnp.zeros(st.shape, F32)
            for s2 in range(n_sel):
                row = st[s2:s2 + 1, :]
                rank = rank + jnp.where(row > st, 1.0, jnp.where(row == st, jnp.where(sub > s2, 1.0, 0.0), 0.0))
            sel_t = jnp.where(rank < SLC_TOPN, 1.0, 0.0)
            sel_t = jnp.concatenate([sel_t, jnp.zeros((LANES - nsp, tq), F32)], axis=0)
            outs.append(sel_t.T)
        return jnp.concatenate(outs, axis=0)
    lane = lax.broadcasted_iota(jnp.int32, (1, LANES), 1)
    rank = jnp.zeros(score.shape, F32)
    for s2 in range(n_sel):
        col = score[:, s2:s2 + 1]
        rank = rank + jnp.where(col > score, 1.0, jnp.where(col == score, jnp.where(lane > s2, 1.0, 0.0), 0.0))
    return jnp.where(rank < SLC_TOPN, 1.0, 0.0)


def _cmp_and_select(qbd_n, kc, vc, ov, pos_base, tq, n_sel):
    nb = kc.shape[0]
    r = lax.broadcasted_iota(jnp.int32, (NSA_HEADS * tq, 1), 0)
    qpos = pos_base + (r & (tq - 1))
    blk_end = lax.broadcasted_iota(jnp.int32, (1, nb), 1) * CMP_STRIDE + (CMP_BLOCK - 1)
    visible = blk_end <= qpos
    s = jnp.where(visible, _mm_nt(qbd_n, kc) * ATT_SCALE, NEG_INF)
    e = jnp.exp(s - jnp.max(s, axis=-1, keepdims=True))
    p = jnp.where(visible, e / jnp.sum(e, axis=-1, keepdims=True), 0.0)
    o = _mm(p, vc)
    psum = jnp.sum(p.reshape(NSA_KV_HEADS, NSA_HPG, tq, nb), axis=1).reshape(NSA_KV_HEADS * tq, nb)
    imp = _mm_split(psum, ov.astype(MXU_DTYPE))
    r4 = lax.broadcasted_iota(jnp.int32, (NSA_KV_HEADS * tq, 1), 0)
    qblk = (pos_base + (r4 & (tq - 1))) // SLC_BLOCK
    sidx = lax.broadcasted_iota(jnp.int32, (1, LANES), 1)
    bonus = jnp.where(sidx == 0, FORCE_BONUS, jnp.where(sidx == qblk, FORCE_BONUS,
                                                        jnp.where(sidx == qblk - 1, FORCE_BONUS, 0.0)))
    score = jnp.where(sidx <= qblk, imp + bonus, NEG_INF)
    return o, _topk_mask(score, n_sel, tq), qpos


def _gated_sum(gates, gx, o_cmp, o_slc, o_win, bm, tq):
    gf = _mm_split(gates, gx)
    return (gf[:, 0:NSA_Q_W] * _extract(o_cmp, bm, tq) + gf[:, NSA_Q_W:2 * NSA_Q_W] * _extract(o_slc, bm, tq)
            + gf[:, 2 * NSA_Q_W:] * _extract(o_win, bm, tq))


def _nsa_prompt_body(qn_ref, qr_ref, gates_ref, kc_ref, vc_ref, kv_ref, win_ref, ov_ref, gx_ref, o_ref,
                     *, seq, n_sel, kt, wk):
    tq = Q_TILE
    t0 = pl.program_id(1) * tq
    bm = _group_masks()
    w = NSA_KV_W
    o_cmp, sel, _ = _cmp_and_select(_blockdiag(qn_ref[...], bm), kc_ref[0], vc_ref[0], ov_ref[...], t0, tq, n_sel)
    qbd = _blockdiag(qr_ref[...], bm) * ATT_SCALE
    sel_b = sel.astype(MXU_DTYPE)
    sidx = lax.broadcasted_iota(jnp.int32, (LANES, 1), 0)
    qpos_t = t0 + lax.broadcasted_iota(jnp.int32, (tq, 1), 0)
    qpos_gt = t0 + (lax.broadcasted_iota(jnp.int32, (NSA_KV_HEADS * tq, 1), 0) & (tq - 1))

    def kv_tile(jt, carry):
        m, l, acc = carry
        k0 = pl.multiple_of(jt * kt, kt)
        kpos = k0 + lax.broadcasted_iota(jnp.int32, (1, kt), 1)
        expand = jnp.where((kpos // SLC_BLOCK) == sidx, 1.0, 0.0).astype(MXU_DTYPE)
        bias = (jnp.dot(sel_b, expand, preferred_element_type=F32) - 1.0) * (-NEG_INF)
        bias = jnp.where(kpos <= qpos_gt, bias, NEG_INF)
        s = _mm_nt(qbd, kv_ref[pl.ds(k0, kt), 0:w]) + _per_head_rows(bias, tq)
        m_new = jnp.maximum(m, jnp.max(s, axis=-1, keepdims=True))
        alpha = jnp.exp(m - m_new)
        e = jnp.exp(s - m_new)
        l = alpha * l + jnp.sum(e, axis=-1, keepdims=True)
        acc = alpha * acc + _mm(e, kv_ref[pl.ds(k0, kt), w:2 * w])
        return m_new, l, acc

    rows = NSA_HEADS * tq
    n_tiles = (t0 + tq + kt - 1) // kt
    m, l, acc = lax.fori_loop(0, n_tiles, kv_tile, (jnp.full((rows, 1), SOFTMAX_M0, F32), jnp.zeros((rows, 1), F32),
                                                    jnp.zeros((rows, w), F32)))
    o_slc = acc * (1.0 / l)

    start = pl.multiple_of(jnp.maximum(t0 + tq - wk, 0), tq)
    kpos = start + lax.broadcasted_iota(jnp.int32, (1, wk), 1)
    bias = jnp.where(kpos <= qpos_t, jnp.where(kpos > qpos_t - WINDOW, 0.0, NEG_INF), NEG_INF)
    bias = jnp.broadcast_to(bias.reshape(1, tq, wk), (NSA_HEADS, tq, wk)).reshape(rows, wk)
    s = _mm_nt(qbd, win_ref[pl.ds(start, wk), 0:w]) + bias
    e = jnp.exp(s - jnp.max(s, axis=-1, keepdims=True))
    o_win = _mm(e, win_ref[pl.ds(start, wk), w:2 * w]) * (1.0 / jnp.sum(e, axis=-1, keepdims=True))
    o_ref[...] = _gated_sum(gates_ref[...], gx_ref[...], o_cmp, o_slc, o_win, bm, tq).astype(o_ref.dtype)


def nsa_attn_prompt(qn, qr, gates, kc, vc, rows_new, win_new, n_seq, seq):
    tq = Q_TILE
    nb = kc.shape[1]
    n_sel = -(-seq // SLC_BLOCK)
    kt = min(KV_TILE, seq)
    wk = min(WINDOW + tq, seq)
    ov = jnp.asarray(_overlap_matrix(nb, seq))
    gx = jnp.asarray(_gate_expand(), MXU_DTYPE)
    per = seq // tq

    def qrows(width):
        return pl.BlockSpec((tq, width), lambda n, t: (n * per + t, 0))

    def per_seq(shape, lane_block=0):
        return pl.BlockSpec(shape, lambda n, t: (n,) + (0,) * (len(shape) - 2) + (lane_block,))

    return pl.pallas_call(
        functools.partial(_nsa_prompt_body, seq=seq, n_sel=n_sel, kt=kt, wk=wk),
        grid=(n_seq, per),
        in_specs=[qrows(NSA_Q_W), qrows(NSA_Q_W), qrows(LANES), per_seq((1, nb, NSA_KV_W)), per_seq((1, nb, NSA_KV_W)),
                  per_seq((seq, 2 * NSA_KV_W), 1), per_seq((seq, 2 * NSA_KV_W)),
                  _resident((nb, LANES)), _resident((LANES, 3 * NSA_Q_W))],
        out_specs=qrows(NSA_Q_W),
        out_shape=jax.ShapeDtypeStruct((n_seq * seq, NSA_Q_W), MXU_DTYPE),
        compiler_params=_params(2),
        name="nsa_attn_prompt",
    )(qn, qr, gates, kc, vc, rows_new, win_new, ov, gx)


def _nsa_sample_body(pt_ref, *refs, n_pages, past_len, ts, n_sel, has_prev):
    del pt_ref
    page_refs = refs[:n_pages]
    (qn_ref, qr_ref, gates_ref, kc_ref, vc_ref, rnew_ref, wold_ref, wnew_ref, ov_ref, gx_ref, ex_ref,
     o_ref, wout_ref) = refs[n_pages:n_pages + 11] + refs[n_pages + 11 + int(has_prev):]
    tq = TS_PAD
    w = NSA_KV_W
    bm = _group_masks()
    o_cmp, sel, qpos = _cmp_and_select(_blockdiag(qn_ref[0].astype(F32), bm), kc_ref[0], vc_ref[0], ov_ref[...],
                                       past_len, tq, n_sel)
    qbd = _blockdiag(qr_ref[0].astype(F32), bm).astype(MXU_DTYPE)
    pad = jnp.zeros((PAGE_SIZE - tq, w), F32)

    bias = (jnp.dot(sel.astype(MXU_DTYPE), ex_ref[...], preferred_element_type=F32) - 1.0) * (-NEG_INF)
    k_new = jnp.concatenate([rnew_ref[0][:, 0:w], pad], axis=0)
    v_new = jnp.concatenate([rnew_ref[0][:, w:2 * w], pad], axis=0)
    s = jnp.concatenate([_mm(qbd, r[0, 0:w, :]) for r in page_refs] + [_mm_nt(qbd, k_new)], axis=1)
    s = s * ATT_SCALE + _per_head_rows(bias, tq)
    kpos = lax.broadcasted_iota(jnp.int32, (1, (n_pages + 1) * PAGE_SIZE), 1)
    s = jnp.where(kpos <= qpos, s, NEG_INF)
    e = jnp.exp(s - jnp.max(s, axis=-1, keepdims=True))
    acc = _mm(e[:, n_pages * PAGE_SIZE:], v_new)
    for p, r in enumerate(page_refs):
        acc = acc + _mm_nt(e[:, p * PAGE_SIZE:(p + 1) * PAGE_SIZE], r[0, w:2 * w, :])
    o_slc = acc * (1.0 / jnp.sum(e, axis=-1, keepdims=True))

    wlen = wold_ref.shape[2]
    kw_new = jnp.concatenate([wnew_ref[0][:, 0:w], pad], axis=0)
    vw_new = jnp.concatenate([wnew_ref[0][:, w:2 * w], pad], axis=0)
    s = jnp.concatenate([_mm(qbd, wold_ref[0, 0:w, :]), _mm_nt(qbd, kw_new)], axis=1) * ATT_SCALE
    kpos = (past_len - wlen) + lax.broadcasted_iota(jnp.int32, (1, wlen + PAGE_SIZE), 1)
    s = jnp.where(kpos <= qpos, s, NEG_INF)
    s = jnp.where(kpos > qpos - WINDOW, s, NEG_INF)
    e = jnp.exp(s - jnp.max(s, axis=-1, keepdims=True))
    acc = _mm_nt(e[:, 0:wlen], wold_ref[0, w:2 * w, :]) + _mm(e[:, wlen:], vw_new)
    o_win = acc * (1.0 / jnp.sum(e, axis=-1, keepdims=True))

    o_ref[0] = _gated_sum(gates_ref[0], gx_ref[...], o_cmp, o_slc, o_win, bm, tq)

    shifted = pltpu.roll(wold_ref[0], wlen - ts, 1)
    new_rows = jnp.concatenate([wnew_ref[0], jnp.zeros((LANES - tq, 2 * w), F32)], axis=0)
    new_t = jnp.concatenate([new_rows[:, j * LANES:(j + 1) * LANES].T for j in range(2 * w // LANES)], axis=0)
    new_t = pltpu.roll(new_t, LANES - ts, 1)
    lane = lax.broadcasted_iota(jnp.int32, (1, LANES), 1)
    wout_ref[0, :, 0:wlen - LANES] = shifted[:, 0:wlen - LANES]
    wout_ref[0, :, wlen - LANES:wlen] = jnp.where(lane >= LANES - ts, new_t, shifted[:, wlen - LANES:wlen])


def nsa_attn_sample(table, pages, qn, qr, gates, kc, vc, rows_new, win_old, win_new, win_base, past_len, ts,
                    win_prev):
    n_seq, n_pages = table.shape
    nb = kc.shape[1]
    wlen = win_old.shape[2]
    length = past_len + ts
    n_sel = -(-length // SLC_BLOCK)
    n_keys = (n_pages + 1) * PAGE_SIZE
    ov = jnp.asarray(_overlap_matrix(nb, length))
    gx = jnp.asarray(_gate_expand(), MXU_DTYPE)
    ex = jnp.asarray((np.arange(n_keys)[None, :] // SLC_BLOCK == np.arange(LANES)[:, None]).astype(np.float32), MXU_DTYPE)

    def page_spec(p):
        return pl.BlockSpec((1, 2 * NSA_KV_W, PAGE_SIZE), lambda n, pt: (pt[n, p], 1, 0))

    def per_seq(shape, lane_block=0, base=0):
        return pl.BlockSpec(shape, lambda n, pt: (base + n,) + (0,) * (len(shape) - 2) + (lane_block,))

    def const(shape):
        return pl.BlockSpec(shape, lambda n, pt: (0,) * len(shape), pipeline_mode=pl.Buffered(1))

    in_specs = [page_spec(p) for p in range(n_pages)] + [
        per_seq((1, TS_PAD, NSA_Q_W)), per_seq((1, TS_PAD, NSA_Q_W)), per_seq((1, TS_PAD, LANES)),
        per_seq((1, nb, NSA_KV_W)), per_seq((1, nb, NSA_KV_W)), per_seq((1, TS_PAD, 2 * NSA_KV_W), 1),
        per_seq((1, 2 * NSA_KV_W, wlen), 0, win_base), per_seq((1, TS_PAD, 2 * NSA_KV_W)),
        const((nb, LANES)), const((LANES, 3 * NSA_Q_W)), const((LANES, n_keys))]
    operands = [table, *([pages] * n_pages), qn, qr, gates, kc, vc, rows_new, win_old, win_new, ov, gx, ex]
    aliases = {}
    if win_prev is not None:
        in_specs.append(pl.BlockSpec(memory_space=pl.ANY))
        aliases = {len(operands): 1}
        operands.append(win_prev)
    grid_spec = pltpu.PrefetchScalarGridSpec(
        num_scalar_prefetch=1, grid=(n_seq,), in_specs=in_specs,
        out_specs=[per_seq((1, TS_PAD, NSA_Q_W)), per_seq((1, 2 * NSA_KV_W, wlen), 0, win_base)])
    return pl.pallas_call(
        functools.partial(_nsa_sample_body, n_pages=n_pages, past_len=past_len, ts=ts, n_sel=n_sel,
                          has_prev=win_prev is not None),
        grid_spec=grid_spec,
        out_shape=[jax.ShapeDtypeStruct((n_seq, TS_PAD, NSA_Q_W), F32),
                   jax.ShapeDtypeStruct(win_old.shape, F32)],
        input_output_aliases=aliases,
        compiler_params=_params(), name="nsa_attn_sample",
    )(*operands)


S5_NB = 8
S5_GB = 8
S5_HALF = S5_GB * S5_STATE
S5_TIME = 256


def _s5_disc_body(lr_ref, li_ref, ldt_ref, bre_ref, bim_ref, are_ref, aim_ref, bbre_ref, bbim_ref):
    dt = jnp.exp(ldt_ref[...])
    lr = jnp.minimum(lr_ref[...], -1e-4)
    li = li_ref[...]
    mag = jnp.exp(lr * dt)
    a_re = mag * jnp.cos(li * dt)
    a_im = mag * jnp.sin(li * dt)
    den = lr * lr + li * li
    z_re = ((a_re - 1.0) * lr + a_im * li) / den
    z_im = (a_im * lr - (a_re - 1.0) * li) / den
    are_ref[...] = a_re
    aim_ref[...] = a_im
    bbre_ref[...] = z_re * bre_ref[...] - z_im * bim_ref[...]
    bbim_ref[...] = z_re * bim_ref[...] + z_im * bre_ref[...]


def s5_discretize(lam_re, lam_im, log_dt, b_re, b_im):
    rows = S5_GROUPS * S5_GROUP_CH

    def per_channel(a):
        return jnp.broadcast_to(a[:, None, :], (S5_GROUPS, S5_GROUP_CH, S5_STATE)).reshape(rows, S5_STATE)

    args = (per_channel(lam_re), per_channel(lam_im), per_channel(jnp.broadcast_to(log_dt[:, None], lam_re.shape)),
            b_re.transpose(0, 2, 1).reshape(rows, S5_STATE), b_im.transpose(0, 2, 1).reshape(rows, S5_STATE))
    out = jax.ShapeDtypeStruct((rows, S5_STATE), F32)
    a_re, a_im, bb_re, bb_im = pl.pallas_call(_s5_disc_body, out_shape=[out] * 4, name="s5_discretize")(*args)
    shape = (S5_GROUPS, S5_GROUP_CH, S5_STATE)
    return a_re.reshape(shape)[:, 0], a_im.reshape(shape)[:, 0], bb_re.reshape(shape), bb_im.reshape(shape)


def _s5_scan_body(u_ref, bb_ref, cc_ref, a_ref, h0_ref, y_ref, hout_ref, bu_scr, hs_scr, h_scr, *, tc, use_h0):
    t = pl.program_id(2)

    @pl.when(t == 0)
    def _():
        h_scr[...] = h0_ref[0, 0] if use_h0 else jnp.zeros(h_scr.shape, F32)

    bu_scr[...] = _mm(u_ref[...], bb_ref[0])
    ar = a_ref[0][:, :S5_HALF]
    ai = a_ref[0][:, S5_HALF:]

    def step(i, carry):
        hr, hi = carry
        r = pl.multiple_of(i * S5_NB, S5_NB)
        nhr = ar * hr - ai * hi + bu_scr[pl.ds(r, S5_NB), :S5_HALF]
        nhi = ar * hi + ai * hr + bu_scr[pl.ds(r, S5_NB), S5_HALF:]
        hs_scr[pl.ds(r, S5_NB), :S5_HALF] = nhr
        hs_scr[pl.ds(r, S5_NB), S5_HALF:] = nhi
        return nhr, nhi

    hr, hi = lax.fori_loop(0, tc, step, (h_scr[:, :S5_HALF], h_scr[:, S5_HALF:]), unroll=min(tc, 8))
    h_scr[:, :S5_HALF] = hr
    h_scr[:, S5_HALF:] = hi
    y_ref[...] = _mm(hs_scr[...], cc_ref[0])

    @pl.when(t == pl.num_programs(2) - 1)
    def _():
        hout_ref[0, 0] = h_scr[...]


def s5_scan(u, h0, a_re, a_im, bb_re, bb_im, c_re, c_im):
    n_real, t_len, _ = u.shape
    n = -(-n_real // S5_NB) * S5_NB
    if n != n_real:
        u = jnp.pad(u, ((0, n - n_real), (0, 0), (0, 0)))
        h0 = None if h0 is None else jnp.pad(h0, ((0, n - n_real), (0, 0), (0, 0), (0, 0)))
    nb = n // S5_NB
    ngb = S5_GROUPS // S5_GB
    tc = min(S5_TIME, t_len)
    eye = jnp.eye(S5_GB, dtype=F32)

    def blockdiag_in(bb):
        return jnp.einsum('ab,xacp->xacbp', eye, bb.reshape(ngb, S5_GB, S5_GROUP_CH, S5_STATE)).reshape(
            ngb, S5_GB * S5_GROUP_CH, S5_HALF)

    def blockdiag_out(cc):
        return jnp.einsum('ab,xbcp->xapbc', eye, cc.reshape(ngb, S5_GB, S5_GROUP_CH, S5_STATE)).reshape(
            ngb, S5_HALF, S5_GB * S5_GROUP_CH)

    bb = jnp.concatenate([blockdiag_in(bb_re), blockdiag_in(bb_im)], axis=2).astype(MXU_DTYPE)
    cc = jnp.concatenate([blockdiag_out(c_re), blockdiag_out(-c_im)], axis=1).astype(MXU_DTYPE)
    a = jnp.concatenate([a_re.reshape(ngb, S5_HALF), a_im.reshape(ngb, S5_HALF)], axis=1)
    a = jnp.broadcast_to(a[:, None, :], (ngb, S5_NB, 2 * S5_HALF))
    use_h0 = h0 is not None
    if use_h0:
        h0b = h0.reshape(nb, S5_NB, 2, ngb, S5_HALF).transpose(0, 3, 1, 2, 4).reshape(nb, ngb, S5_NB, 2 * S5_HALF)
    else:
        h0b = jnp.zeros((1, 1, S5_NB, 2 * S5_HALF), F32)
    ub = u.reshape(nb, S5_NB, t_len, D_MODEL).transpose(0, 2, 1, 3).reshape(nb * t_len * S5_NB, D_MODEL)
    per = t_len // tc
    lanes_u = S5_GB * S5_GROUP_CH

    rows_spec = pl.BlockSpec((tc * S5_NB, lanes_u), lambda b, g, t: (b * per + t, g))
    state_spec = pl.BlockSpec((1, 1, S5_NB, 2 * S5_HALF), lambda b, g, t: (b, g, 0, 0))
    h0_spec = state_spec if use_h0 else pl.BlockSpec((1, 1, S5_NB, 2 * S5_HALF), lambda b, g, t: (0, 0, 0, 0))
    y, h_last = pl.pallas_call(
        functools.partial(_s5_scan_body, tc=tc, use_h0=use_h0),
        grid=(nb, ngb, per),
        in_specs=[rows_spec, pl.BlockSpec((1, lanes_u, 2 * S5_HALF), lambda b, g, t: (g, 0, 0)),
                  pl.BlockSpec((1, 2 * S5_HALF, lanes_u), lambda b, g, t: (g, 0, 0)),
                  pl.BlockSpec((1, S5_NB, 2 * S5_HALF), lambda b, g, t: (g, 0, 0)), h0_spec],
        out_specs=[rows_spec, state_spec],
        out_shape=[jax.ShapeDtypeStruct((nb * t_len * S5_NB, D_MODEL), F32),
                   jax.ShapeDtypeStruct((nb, ngb, S5_NB, 2 * S5_HALF), F32)],
        scratch_shapes=[pltpu.VMEM((tc * S5_NB, 2 * S5_HALF), F32), pltpu.VMEM((tc * S5_NB, 2 * S5_HALF), F32),
                        pltpu.VMEM((S5_NB, 2 * S5_HALF), F32)],
        compiler_params=pltpu.CompilerParams(dimension_semantics=("parallel", "parallel", "arbitrary"),
                                             vmem_limit_bytes=VMEM_LIMIT),
        name="s5_scan",
    )(ub, bb, cc, a, h0b)
    y = y.reshape(nb, t_len, S5_NB, D_MODEL).transpose(0, 2, 1, 3).reshape(n, t_len, D_MODEL)
    h_last = h_last.reshape(nb, ngb, S5_NB, 2, S5_GB, S5_STATE).transpose(0, 2, 3, 1, 4, 5).reshape(
        n, 2, S5_GROUPS, S5_STATE)
    return y[:n_real], h_last[:n_real]


def _s5_out_body(x_ref, y_ref, u_ref, d_ref, w_ref, o_ref):
    z = jax.nn.gelu(y_ref[...] + d_ref[...] * u_ref[...])
    ab = _mm(z, w_ref[...])
    o_ref[...] = x_ref[...] + ab[:, :D_MODEL] * jax.nn.sigmoid(ab[:, D_MODEL:])


def s5_out(x, y, u, d_skip, w_glu):
    rows = x.shape[0]
    tm = _row_tile(rows)
    return pl.pallas_call(
        _s5_out_body,
        grid=(rows // tm,),
        in_specs=[_rows(tm, D_MODEL), _rows(tm, D_MODEL), _rows(tm, D_MODEL), _resident((1, D_MODEL)),
                  _resident((D_MODEL, 2 * D_MODEL))],
        out_specs=_rows(tm, D_MODEL),
        out_shape=jax.ShapeDtypeStruct((rows, D_MODEL), F32),
        compiler_params=_params(),
        name="s5_out",
    )(x, y, u, d_skip.reshape(1, -1), w_glu.astype(MXU_DTYPE))


HG_SUB = 32
HG_TIME = 256
HG_HEADS_PER_STEP = 2


def _mm_exact(l01, x):
    x1 = x.astype(MXU_DTYPE)
    r1 = x - x1.astype(F32)
    x2 = r1.astype(MXU_DTYPE)
    x3 = (r1 - x2.astype(F32)).astype(MXU_DTYPE)
    dot = functools.partial(jnp.dot, preferred_element_type=F32)
    return dot(l01, x1) + dot(l01, x2) + dot(l01, x3)


def _hgrn_body(q_ref, fz_ref, v_ref, g_ref, lb_ref, og_ref, s0_ref, o_ref, sout_ref, s_scr,
               *, tb, chunk, sub, valid, use_s0, hps):
    tblk = pl.program_id(2)

    @pl.when(tblk == 0)
    def _():
        s_scr[...] = s0_ref[0] if use_s0 else jnp.zeros(s_scr.shape, F32)

    og = og_ref[...]
    eye = (lax.broadcasted_iota(jnp.int32, (HG_DK, HG_DK), 0) == lax.broadcasted_iota(jnp.int32, (HG_DK, HG_DK), 1))
    tril = jnp.where(lax.broadcasted_iota(jnp.int32, (chunk, chunk), 0)
                     >= lax.broadcasted_iota(jnp.int32, (chunk, chunk), 1), 1.0, 0.0).astype(MXU_DTYPE)
    trow = lax.broadcasted_iota(jnp.int32, (sub, 1), 0)
    nsub = chunk // sub

    for hh, ci in itertools.product(range(hps), range(tb // chunk)):
        r0 = ci * chunk
        cols = slice(hh * HG_DK, (hh + 1) * HG_DK)
        lb = lb_ref[:, cols]
        q = q_ref[r0:r0 + chunk, cols]
        v = v_ref[r0:r0 + chunk, cols]
        f = lb + (1.0 - lb) * jax.nn.sigmoid(fz_ref[r0:r0 + chunk, cols])
        k = 1.0 - f
        lf = jnp.log(f)
        if valid < tb:
            live = (r0 + lax.broadcasted_iota(jnp.int32, (chunk, 1), 0)) < valid
            k = jnp.where(live, k, 0.0)
            lf = jnp.where(live, lf, 0.0)
        gcum = _mm_exact(tril, lf)
        state = s_scr[hh]
        o_inter = _mm(q * jnp.exp(gcum), state)
        o_blocks = [o_inter[i * sub:(i + 1) * sub] for i in range(nsub)]

        for j in range(nsub - 1):
            lo, hi = j * sub, (j + 1) * sub
            g_ref_row = gcum[hi - 1:hi, :]
            k_t = k[lo:hi] * jnp.exp(g_ref_row - gcum[lo:hi])
            q_t = q[hi:] * jnp.exp(gcum[hi:] - g_ref_row)
            contrib = _mm(_mm_nt(q_t, k_t), v[lo:hi])
            for i in range(j + 1, nsub):
                o_blocks[i] = o_blocks[i] + contrib[(i - j - 1) * sub:(i - j) * sub]

        for i in range(nsub):
            lo, hi = i * sub, (i + 1) * sub
            q_i, k_i, v_i, g_i = q[lo:hi], k[lo:hi], v[lo:hi], gcum[lo:hi]
            acc = jnp.zeros((sub, HG_DV), F32)
            for s in range(sub):
                decay = jnp.exp(jnp.minimum(g_i - g_i[s:s + 1], 0.0))
                wgt = jnp.sum(q_i * k_i[s:s + 1] * decay, axis=1, keepdims=True)
                acc = acc + jnp.where(trow >= s, wgt, 0.0) * v_i[s:s + 1]
            o_blocks[i] = o_blocks[i] + acc

        g_last = gcum[chunk - 1:chunk, :]
        k_t = k * jnp.exp(g_last - gcum)
        decay_col = jnp.sum(jnp.where(eye, jnp.exp(g_last), 0.0), axis=1, keepdims=True)
        kv = lax.dot_general(k_t.astype(MXU_DTYPE), v.astype(MXU_DTYPE), (((0,), (0,)), ((), ())),
                             preferred_element_type=F32)
        s_scr[hh] = decay_col * state + kv

        o = jnp.concatenate(o_blocks, axis=0)
        o = o * lax.rsqrt(jnp.mean(o * o, axis=-1, keepdims=True) + RMS_EPS) * og
        gate = g_ref[r0:r0 + chunk, cols]
        o_ref[r0:r0 + chunk, cols] = o * (gate * jax.nn.sigmoid(gate))

    @pl.when(tblk == pl.num_programs(2) - 1)
    def _():
        sout_ref[0] = s_scr[...]


def hgrn_scan(pr, s0, o_gain, lb, n_seq, t_rows, valid):
    tb = min(HG_TIME, t_rows)
    chunk = min(HG_CHUNK, tb)
    sub = min(HG_SUB, chunk)
    per = t_rows // tb
    hps = HG_HEADS if tb < HG_TIME else HG_HEADS_PER_STEP
    hblocks = HG_HEADS // hps
    use_s0 = s0 is not None
    if not use_s0:
        s0 = jnp.zeros((1, hps, HG_DK, HG_DV), F32)

    def part(idx):
        return pl.BlockSpec((tb, hps * HG_DK), lambda n, h, t: (n * per + t, idx * hblocks + h))

    head_vec = pl.BlockSpec((1, hps * HG_DK), lambda n, h, t: (0, h))
    state_spec = pl.BlockSpec((1, hps, HG_DK, HG_DV), lambda n, h, t: (n, h, 0, 0))
    s0_spec = state_spec if use_s0 else pl.BlockSpec((1, hps, HG_DK, HG_DV), lambda n, h, t: (0, 0, 0, 0))
    return pl.pallas_call(
        functools.partial(_hgrn_body, tb=tb, chunk=chunk, sub=sub, valid=valid, use_s0=use_s0, hps=hps),
        grid=(n_seq, hblocks, per),
        in_specs=[part(0), part(1), part(2), part(3), head_vec, pl.BlockSpec((1, HG_DV), lambda n, h, t: (0, 0)), s0_spec],
        out_specs=[pl.BlockSpec((tb, hps * HG_DV), lambda n, h, t: (n * per + t, h)), state_spec],
        out_shape=[jax.ShapeDtypeStruct((n_seq * t_rows, D_MODEL), F32),
                   jax.ShapeDtypeStruct((n_seq, HG_HEADS, HG_DK, HG_DV), F32)],
        scratch_shapes=[pltpu.VMEM((hps, HG_DK, HG_DV), F32)],
        compiler_params=pltpu.CompilerParams(dimension_semantics=("parallel", "parallel", "arbitrary"),
                                             vmem_limit_bytes=VMEM_LIMIT),
        name="hgrn_scan",
    )(pr, pr, pr, pr, lb.reshape(1, -1), o_gain.reshape(1, -1), s0)


def kernel(x_prompt, x_sample, cache_nsa, state_nsa_win, state_s5, state_hgrn, page_table, p_prompt, p_sample, norm_gain, ffn_w_in, ffn_w_out, ple_w_gate, ple_w_proj, nsa_w_in, nsa_w_out, nsa_qk_gain, nsa_cmp_pe, nsa_cmp_w, s5_lam_re, s5_lam_im, s5_log_dt, s5_b_re, s5_b_im, s5_c_re, s5_c_im, s5_d, s5_w_glu, hg_w_in, hg_w_out, hg_o_gain, hg_lb_raw):
    B, T, _ = x_prompt.shape
    Bs, Ts, _ = x_sample.shape
    rp = B * T
    rs = Bs * Ts
    n_pages = page_table.shape[1]
    past_len = n_pages * PAGE_SIZE
    n_phys = cache_nsa.shape[1]
    wlen = state_nsa_win.shape[2]
    assert T % Q_TILE == 0 and T % PAGE_SIZE == 0 and T % min(KV_TILE, T) == 0 and Ts <= TS_PAD
    lb_sm = jax.nn.softmax(hg_lb_raw.astype(F32), axis=0)
    lower_bounds = jnp.cumsum(lb_sm, axis=0) - lb_sm[0]

    x = jnp.concatenate([x_prompt.reshape(rp, D_MODEL), x_sample.reshape(rs, D_MODEL)], axis=0)
    p_all = jnp.concatenate([p_prompt.reshape(DEPTH, rp, PLE_DIM), p_sample.reshape(DEPTH, rs, PLE_DIM)], axis=1)
    pos = jnp.concatenate([jnp.tile(jnp.arange(T), B), jnp.tile(past_len + jnp.arange(Ts), Bs)])
    cache_pages = cache_nsa.transpose(0, 1, 3, 4, 5, 2).reshape(-1, 4 * NSA_KV_W, PAGE_SIZE)
    win_state = state_nsa_win.transpose(0, 1, 3, 4, 5, 2).reshape(-1, 2 * NSA_KV_W, wlen)
    prompt_table = (jnp.arange(B, dtype=jnp.int32)[:, None] * (T // PAGE_SIZE)
                    + jnp.arange(T // PAGE_SIZE, dtype=jnp.int32)[None, :])
    out_perm = _head_perm()

    def split(a):
        return a[:rp].reshape(B, T, -1), a[rp:].reshape(Bs, Ts, -1)

    def join(a, b):
        return jnp.concatenate([a.reshape(rp, -1), b.reshape(rs, -1)], axis=0)

    def sample_pad(a):
        return jnp.pad(a[rp:].reshape(Bs, Ts, -1), ((0, 0), (0, TS_PAD - Ts), (0, 0)))

    outs_p = {0: [], 1: [], 2: [], 3: []}
    outs_s = {0: [], 1: [], 2: [], 3: []}
    win_all = None
    for i in range(DEPTH):
        kind = LAYER_KIND[i]
        j = LAYER_SLOT[i]
        g = norm_gain[i]
        x, xn = ffn_step(x, g[0], g[1], ffn_w_in[i, 0], ffn_w_out[i, 0])
        if kind == 0:
            qn, qr, rows_new, win_new, gates = nsa_proj(xn, nsa_w_in[j], nsa_qk_gain[j], pos)
            cmp_args = (nsa_cmp_pe[j], nsa_cmp_w[j], nsa_qk_gain[j, 1])
            kc_p, vc_p = nsa_compress(rows_new.reshape(-1, PAGE_SIZE, 4 * NSA_KV_W), prompt_table, *cmp_args, False)
            o_p = nsa_attn_prompt(qn, qr, gates, kc_p, vc_p, rows_new, win_new, B, T)
            table = page_table.astype(jnp.int32) + j * n_phys
            kc_s, vc_s = nsa_compress(cache_pages, table, *cmp_args, True)
            o_s, win_all = nsa_attn_sample(table, cache_pages, sample_pad(qn), sample_pad(qr), sample_pad(gates), kc_s,
                                           vc_s, sample_pad(rows_new), win_state, sample_pad(win_new), j * Bs, past_len,
                                           Ts, win_all)
            r_p, r_s = split(rows_new)
            outs_p[0].append(r_p.reshape(B, T, 4, NSA_KV_HEADS, HEAD_DIM))
            outs_s[0].append(r_s.reshape(Bs, Ts, 4, NSA_KV_HEADS, HEAD_DIM))
            buf = min(WINDOW, T)
            outs_p[1].append(win_new[:rp].reshape(B, T, 2, NSA_KV_HEADS, HEAD_DIM)[:, T - buf:])
            o_all = jnp.concatenate([o_p, o_s[:, :Ts].reshape(rs, NSA_Q_W).astype(o_p.dtype)], axis=0)
            x = resid_proj(x, o_all, nsa_w_out[j][out_perm])
        elif kind == 1:
            u_p, u_s = split(xn)
            disc = s5_discretize(s5_lam_re[j], s5_lam_im[j], s5_log_dt[j], s5_b_re[j], s5_b_im[j])
            y_p, h_p = s5_scan(u_p, None, *disc, s5_c_re[j], s5_c_im[j])
            y_s, h_s = s5_scan(u_s, state_s5[j], *disc, s5_c_re[j], s5_c_im[j])
            outs_p[2].append(h_p); outs_s[2].append(h_s)
            x = s5_out(x, join(y_p, y_s), xn, s5_d[j], s5_w_glu[j])
        else:
            pr = proj(xn, hg_w_in[j])
            pr_s = jnp.pad(pr[rp:].reshape(Bs, Ts, -1), ((0, 0), (0, TS_PAD - Ts), (0, 0))).reshape(Bs * TS_PAD, -1)
            o_p, s_p = hgrn_scan(pr, None, hg_o_gain[j], lower_bounds[i], B, T, T)
            o_s, s_s = hgrn_scan(pr_s, state_hgrn[j], hg_o_gain[j], lower_bounds[i], Bs, TS_PAD, Ts)
            outs_p[3].append(s_p); outs_s[3].append(s_s)
            o_s = o_s.reshape(Bs, TS_PAD, D_MODEL)[:, :Ts].reshape(rs, D_MODEL)
            x = resid_proj(x, jnp.concatenate([o_p, o_s], axis=0), hg_w_out[j])
        x = ffn_ple_step(x, g[2], g[3], ffn_w_in[i, 1], ffn_w_out[i, 1], p_all[i], ple_w_gate[i], ple_w_proj[i])

    y_p, y_s = split(x)
    win_out = win_all.reshape(-1, Bs, 2, NSA_KV_HEADS, HEAD_DIM, wlen).transpose(0, 1, 5, 2, 3, 4)
    return (y_p, y_s,
            jnp.stack(outs_p[0]), jnp.stack(outs_p[1]), jnp.stack(outs_p[2]), jnp.stack(outs_p[3]),
            jnp.stack(outs_s[0]), win_out, jnp.stack(outs_s[2]), jnp.stack(outs_s[3]))
```

```python
import functools
import itertools

import numpy as np
import jax
import jax.numpy as jnp
from jax import lax
from jax.experimental import pallas as pl
from jax.experimental.pallas import tpu as pltpu

F32 = jnp.float32
MXU_DTYPE = jnp.bfloat16

D_MODEL = 1024
DEPTH = 4
PAGE_SIZE = 128
D_FF = 2816
PLE_DIM = 256
RMS_EPS = 1e-6
LAYER_KIND = (0, 1, 2, 0)
LAYER_SLOT = (0, 0, 0, 1)

NSA_HEADS = 16
NSA_KV_HEADS = 4
HEAD_DIM = 64
NSA_HPG = NSA_HEADS // NSA_KV_HEADS
NSA_KV_W = NSA_KV_HEADS * HEAD_DIM
NSA_Q_W = NSA_HEADS * HEAD_DIM
NSA_IN = NSA_Q_W + 6 * NSA_KV_W + 3 * NSA_HEADS
CMP_BLOCK = 32
CMP_STRIDE = 16
SLC_BLOCK = 64
SLC_TOPN = 16
WINDOW = 512
FORCE_BONUS = 1e4
NEG_INF = -1e30
ROPE_THETA = 500000.0
ROPE_DIMS = HEAD_DIM // 4
ATT_SCALE = HEAD_DIM ** -0.5

S5_GROUP_CH = 16
S5_GROUPS = D_MODEL // S5_GROUP_CH
S5_STATE = 64

HG_DK = 128
HG_HEADS = D_MODEL // HG_DK
HG_DV = D_MODEL // HG_HEADS
HG_CHUNK = 128

V7X_VMEM_BYTES = 64 * 1024 * 1024
VMEM_LIMIT = V7X_VMEM_BYTES - 8 * 1024 * 1024
LANES = 128
ROW_TILE = 512
FF_CHUNK = 256
Q_TILE = 128
KV_TILE = 512
TS_PAD = 8
CMP_PITCH = 20
SOFTMAX_M0 = -1e29


def _resident(shape):
    return pl.BlockSpec(shape, lambda *_: (0,) * len(shape), pipeline_mode=pl.Buffered(1))


def _row_tile(rows):
    return max(t for t in range(8, ROW_TILE + 1, 8) if rows % t == 0)


def _rows(tm, width):
    return pl.BlockSpec((tm, width), lambda i: (i, 0))


def _params(n_axes=1):
    return pltpu.CompilerParams(dimension_semantics=("parallel",) * n_axes, vmem_limit_bytes=VMEM_LIMIT)


def _rms(x, g):
    return x * lax.rsqrt(jnp.mean(x * x, axis=-1, keepdims=True) + RMS_EPS) * g


def _mm(a, b):
    return jnp.dot(a.astype(MXU_DTYPE), b.astype(MXU_DTYPE), preferred_element_type=F32)


def _mm_nt(a, b):
    return lax.dot_general(a.astype(MXU_DTYPE), b.astype(MXU_DTYPE), (((1,), (1,)), ((), ())),
                           preferred_element_type=F32)


def _mm_split(a, b):
    hi = a.astype(MXU_DTYPE)
    lo = (a - hi.astype(F32)).astype(MXU_DTYPE)
    return (jnp.dot(hi, b, preferred_element_type=F32) + jnp.dot(lo, b, preferred_element_type=F32))


def _ffn_body(x_ref, g_ref, gn_ref, win_ref, wout_ref, o_ref, on_ref, h_ref):
    x = x_ref[...]
    xb = _rms(x, g_ref[...]).astype(MXU_DTYPE)
    for c in range(D_FF // FF_CHUNK):
        lo = c * FF_CHUNK
        a = _mm(xb, win_ref[:, lo:lo + FF_CHUNK])
        b = _mm(xb, win_ref[:, D_FF + lo:D_FF + lo + FF_CHUNK])
        h_ref[:, lo:lo + FF_CHUNK] = (a * jax.nn.sigmoid(a) * b).astype(MXU_DTYPE)
    y = x + 0.5 * _mm(h_ref[...], wout_ref[...])
    o_ref[...] = y
    on_ref[...] = _rms(y, gn_ref[...])


def ffn_step(x, g, g_next, w_in, w_out):
    rows = x.shape[0]
    tm = _row_tile(rows)
    out = jax.ShapeDtypeStruct((rows, D_MODEL), F32)
    return pl.pallas_call(
        _ffn_body,
        grid=(rows // tm,),
        in_specs=[_rows(tm, D_MODEL), _resident((1, D_MODEL)), _resident((1, D_MODEL)),
                  _resident((D_MODEL, 2 * D_FF)), _resident((D_FF, D_MODEL))],
        out_specs=[_rows(tm, D_MODEL), _rows(tm, D_MODEL)],
        out_shape=[out, out],
        scratch_shapes=[pltpu.VMEM((tm, D_FF), MXU_DTYPE)],
        compiler_params=_params(),
        name="ffn_step",
    )(x, g.reshape(1, -1), g_next.reshape(1, -1), w_in.astype(MXU_DTYPE), w_out.astype(MXU_DTYPE))


def _ffn_ple_body(x_ref, g_ref, gn_ref, win_ref, wout_ref, p_ref, wg_ref, wp_ref, o_ref, h_ref):
    x = x_ref[...]
    xb = _rms(x, g_ref[...]).astype(MXU_DTYPE)
    for c in range(D_FF // FF_CHUNK):
        lo = c * FF_CHUNK
        a = _mm(xb, win_ref[:, lo:lo + FF_CHUNK])
        b = _mm(xb, win_ref[:, D_FF + lo:D_FF + lo + FF_CHUNK])
        h_ref[:, lo:lo + FF_CHUNK] = (a * jax.nn.sigmoid(a) * b).astype(MXU_DTYPE)
    y = x + 0.5 * _mm(h_ref[...], wout_ref[...])
    gate = jax.nn.sigmoid(_mm(_rms(y, gn_ref[...]), wg_ref[...]))
    o_ref[...] = y + gate * _mm(p_ref[...], wp_ref[...])


def ffn_ple_step(x, g, g_ple, w_in, w_out, p, w_gate, w_proj):
    rows = x.shape[0]
    tm = _row_tile(rows)
    return pl.pallas_call(
        _ffn_ple_body,
        grid=(rows // tm,),
        in_specs=[_rows(tm, D_MODEL), _resident((1, D_MODEL)), _resident((1, D_MODEL)),
                  _resident((D_MODEL, 2 * D_FF)), _resident((D_FF, D_MODEL)), _rows(tm, PLE_DIM),
                  _resident((D_MODEL, D_MODEL)), _resident((PLE_DIM, D_MODEL))],
        out_specs=_rows(tm, D_MODEL),
        out_shape=jax.ShapeDtypeStruct((rows, D_MODEL), F32),
        scratch_shapes=[pltpu.VMEM((tm, D_FF), MXU_DTYPE)],
        compiler_params=_params(),
        name="ffn_ple_step",
    )(x, g.reshape(1, -1), g_ple.reshape(1, -1), w_in.astype(MXU_DTYPE), w_out.astype(MXU_DTYPE), p,
      w_gate.astype(MXU_DTYPE), w_proj.astype(MXU_DTYPE))


def _proj_body(a_ref, w_ref, o_ref):
    o_ref[...] = _mm(a_ref[...], w_ref[...])


def proj(a, w):
    rows, k = a.shape
    n = w.shape[1]
    tm = _row_tile(rows)
    return pl.pallas_call(
        _proj_body,
        grid=(rows // tm,),
        in_specs=[_rows(tm, k), _resident((k, n))],
        out_specs=_rows(tm, n),
        out_shape=jax.ShapeDtypeStruct((rows, n), F32),
        compiler_params=_params(),
        name="proj",
    )(a, w.astype(MXU_DTYPE))


def _resid_body(x_ref, a_ref, w_ref, o_ref):
    o_ref[...] = x_ref[...] + _mm(a_ref[...], w_ref[...])


def resid_proj(x, a, w):
    rows, k = a.shape
    tm = _row_tile(rows)
    return pl.pallas_call(
        _resid_body,
        grid=(rows // tm,),
        in_specs=[_rows(tm, D_MODEL), _rows(tm, k), _resident((k, D_MODEL))],
        out_specs=_rows(tm, D_MODEL),
        out_shape=jax.ShapeDtypeStruct((rows, D_MODEL), F32),
        compiler_params=_params(),
        name="resid_proj",
    )(x, a, w.astype(MXU_DTYPE))


def _head_perm():
    idx = np.arange(NSA_Q_W).reshape(NSA_KV_HEADS, NSA_HPG, HEAD_DIM)
    return idx.transpose(1, 0, 2).reshape(-1)


def _gate_expand():
    x = np.zeros((LANES, 3 * NSA_Q_W), np.float32)
    for b in range(3):
        for g in range(NSA_KV_HEADS):
            for j in range(NSA_HPG):
                h = g * NSA_HPG + j
                c0 = b * NSA_Q_W + (j * NSA_KV_HEADS + g) * HEAD_DIM
                x[b * NSA_HEADS + h, c0:c0 + HEAD_DIM] = 1.0
    return x


def _overlap_matrix(nb, length):
    n_sel = -(-length // SLC_BLOCK)
    c0 = np.arange(nb)[:, None] * CMP_STRIDE
    s0 = np.arange(LANES)[None, :] * SLC_BLOCK
    ov = np.clip(np.minimum(c0 + CMP_BLOCK, s0 + SLC_BLOCK) - np.maximum(c0, s0), 0, None) / CMP_BLOCK
    ov = np.where(np.arange(LANES)[None, :] < n_sel, ov, 0.0)
    return ov.astype(np.float32)


def _rope_tables(pos):
    half = ROPE_DIMS // 2
    inv = ROPE_THETA ** (-jnp.arange(half, dtype=F32) / half)
    ang = pos.astype(F32)[:, None] * inv[None, :]
    cos, sin = jnp.cos(ang), jnp.sin(ang)
    ones = jnp.ones((pos.shape[0], HEAD_DIM - ROPE_DIMS), F32)
    zeros = jnp.zeros((pos.shape[0], HEAD_DIM - ROPE_DIMS), F32)
    zh = jnp.zeros_like(sin)
    c = jnp.concatenate([cos, cos, ones], axis=1)
    sa = jnp.concatenate([-sin, zh, zeros], axis=1)
    sb = jnp.concatenate([zh, sin, zeros], axis=1)
    rep = LANES // HEAD_DIM
    return jnp.tile(c, (1, rep)), jnp.tile(sa, (1, rep)), jnp.tile(sb, (1, rep))


def _nsa_proj_body(xn_ref, wq_ref, wkv_ref, wgl_ref, b64_ref, gq_ref, gk_ref, c_ref, sa_ref, sb_ref,
                   qn_ref, qr_ref, rows_ref, win_ref, gates_ref, kvb_ref):
    xb = xn_ref[...].astype(MXU_DTYPE)
    c, sa, sb = c_ref[...], sa_ref[...], sb_ref[...]

    def head_norm(v, gain):
        w = v.shape[1]
        ms = _mm_split(v * v, b64_ref[:w, :w])
        return v * lax.rsqrt(ms + RMS_EPS) * gain

    def rope(v):
        w = v.shape[1]
        rep = w // LANES
        ct, sat, sbt = (jnp.concatenate([t] * rep, axis=1) for t in (c, sa, sb))
        return v * ct + pltpu.roll(v, w - ROPE_DIMS // 2, 1) * sat + pltpu.roll(v, ROPE_DIMS // 2, 1) * sbt

    qn = head_norm(_mm(xb, wq_ref[...]), gq_ref[...])
    qn_ref[...] = qn.astype(qn_ref.dtype)
    qr_ref[...] = rope(qn).astype(qr_ref.dtype)
    kv = _mm(xb, wkv_ref[...])
    w = NSA_KV_W
    k_slc = rope(head_norm(kv[:, 2 * w:3 * w], gk_ref[0:1, :]))
    k_win = rope(head_norm(kv[:, 4 * w:5 * w], gk_ref[1:2, :]))
    rows_ref[:, 0:2 * w] = kv[:, 0:2 * w]
    rows_ref[:, 2 * w:3 * w] = k_slc
    rows_ref[:, 3 * w:4 * w] = kv[:, 3 * w:4 * w]
    win_ref[:, 0:w] = k_win
    win_ref[:, w:2 * w] = kv[:, 5 * w:6 * w]
    kvb_ref[:, 0:w] = k_slc.astype(kvb_ref.dtype)
    kvb_ref[:, w:2 * w] = kv[:, 3 * w:4 * w].astype(kvb_ref.dtype)
    kvb_ref[:, 2 * w:3 * w] = k_win.astype(kvb_ref.dtype)
    kvb_ref[:, 3 * w:4 * w] = kv[:, 5 * w:6 * w].astype(kvb_ref.dtype)
    gates_ref[...] = jax.nn.sigmoid(_mm(xb, wgl_ref[...]))


def nsa_proj(xn, w_in, qk_gain, pos):
    rows = xn.shape[0]
    tm = _row_tile(rows)
    kvw = 6 * NSA_KV_W
    wq = w_in[:, :NSA_Q_W][:, _head_perm()].astype(MXU_DTYPE)
    wkv = w_in[:, NSA_Q_W:NSA_Q_W + kvw].astype(MXU_DTYPE)
    wgl = jnp.pad(w_in[:, NSA_Q_W + kvw:], ((0, 0), (0, LANES - 3 * NSA_HEADS))).astype(MXU_DTYPE)
    b64 = jnp.asarray(np.kron(np.eye(NSA_HEADS), np.full((HEAD_DIM, HEAD_DIM), 1.0 / HEAD_DIM)), MXU_DTYPE)
    gq = jnp.tile(qk_gain[0], NSA_HEADS).reshape(1, -1)
    gk = jnp.stack([jnp.tile(qk_gain[2], NSA_KV_HEADS), jnp.tile(qk_gain[3], NSA_KV_HEADS)])
    c, sa, sb = _rope_tables(pos)
    return pl.pallas_call(
        _nsa_proj_body,
        grid=(rows // tm,),
        in_specs=[_rows(tm, D_MODEL), _resident((D_MODEL, NSA_Q_W)), _resident((D_MODEL, kvw)),
                  _resident((D_MODEL, LANES)), _resident((NSA_Q_W, NSA_Q_W)), _resident((1, NSA_Q_W)),
                  _resident((2, NSA_KV_W)), _rows(tm, LANES), _rows(tm, LANES), _rows(tm, LANES)],
        out_specs=[_rows(tm, NSA_Q_W), _rows(tm, NSA_Q_W), _rows(tm, 4 * NSA_KV_W), _rows(tm, 2 * NSA_KV_W),
                   _rows(tm, LANES), _rows(tm, 4 * NSA_KV_W)],
        out_shape=[jax.ShapeDtypeStruct((rows, NSA_Q_W), MXU_DTYPE), jax.ShapeDtypeStruct((rows, NSA_Q_W), MXU_DTYPE),
                   jax.ShapeDtypeStruct((rows, 4 * NSA_KV_W), F32), jax.ShapeDtypeStruct((rows, 2 * NSA_KV_W), F32),
                   jax.ShapeDtypeStruct((rows, LANES), F32), jax.ShapeDtypeStruct((rows, 4 * NSA_KV_W), MXU_DTYPE)],
        compiler_params=_params(),
        name="nsa_proj",
    )(xn, wq, wkv, wgl, b64, gq, gk, c, sa, sb)


def _nsa_compress_body(pt_ref, *refs, n_pages, seqs, transposed):
    del pt_ref
    page_refs = refs[:seqs * n_pages]
    wk_ref, wv_ref, pek_ref, pev_ref, b64_ref, gk_ref, kc_ref, vc_ref, stage = refs[seqs * n_pages:]
    per_page = PAGE_SIZE // CMP_STRIDE
    nb_seq = n_pages * per_page
    nb = seqs * nb_seq
    tiles = 2 * NSA_KV_W // LANES

    for p, r in enumerate(page_refs):
        for j in range(tiles):
            if transposed:
                tile = r[0, j * LANES:(j + 1) * LANES, :].T
            else:
                tile = r[0, :, j * LANES:(j + 1) * LANES]
            for c in range(per_page):
                row = (p * per_page + c) * CMP_PITCH
                stage[j, row:row + CMP_STRIDE, :] = tile[c * CMP_STRIDE:(c + 1) * CMP_STRIDE]

    def compress(kind, w_ref, pe_ref):
        per_kind = NSA_KV_W // LANES
        first = jnp.zeros((nb, NSA_KV_W), F32)
        second = jnp.zeros((nb, NSA_KV_W), F32)
        for l in range(CMP_STRIDE):
            x = jnp.concatenate([stage[kind * per_kind + j, pl.ds(l, nb, stride=CMP_PITCH), :]
                                 for j in range(per_kind)], axis=1)
            first = first + _mm(x + pe_ref[l:l + 1, :], w_ref[l])
            second = second + _mm(x + pe_ref[CMP_STRIDE + l:CMP_STRIDE + l + 1, :], w_ref[CMP_STRIDE + l])
        return first + pltpu.roll(second, nb - 1, 0)

    kc = compress(0, wk_ref, pek_ref)
    ms = _mm_split(kc * kc, b64_ref[...])
    kc = kc * lax.rsqrt(ms + RMS_EPS) * gk_ref[...]
    vc = compress(1, wv_ref, pev_ref)
    for q in range(seqs):
        kc_ref[q] = kc[q * nb_seq:(q + 1) * nb_seq]
        vc_ref[q] = vc[q * nb_seq:(q + 1) * nb_seq]


def nsa_compress(pages, table, cmp_pe, cmp_w, k_gain, transposed):
    n_seq, n_pages = table.shape
    nb = n_pages * (PAGE_SIZE // CMP_STRIDE)
    eye = jnp.eye(NSA_KV_HEADS, dtype=F32)
    w4 = jnp.einsum('gh,klde->klgdhe', eye, cmp_w).reshape(2, CMP_BLOCK, NSA_KV_W, NSA_KV_W).astype(MXU_DTYPE)
    pe4 = jnp.tile(cmp_pe, (1, 1, NSA_KV_HEADS))
    b64 = jnp.asarray(np.kron(np.eye(NSA_KV_HEADS), np.full((HEAD_DIM, HEAD_DIM), 1.0 / HEAD_DIM)), MXU_DTYPE)
    gk = jnp.tile(k_gain, NSA_KV_HEADS).reshape(1, -1)

    page_block = (1, 2 * NSA_KV_W, PAGE_SIZE) if transposed else (1, PAGE_SIZE, 2 * NSA_KV_W)

    seqs = 2 if n_seq % 2 == 0 else 1

    def page_spec(q, p):
        return pl.BlockSpec(page_block, lambda n, pt: (pt[n * seqs + q, p], 0, 0))

    def const(shape):
        return pl.BlockSpec(shape, lambda n, pt: (0,) * len(shape), pipeline_mode=pl.Buffered(1))

    out = jax.ShapeDtypeStruct((n_seq, nb, NSA_KV_W), F32)
    out_spec = pl.BlockSpec((seqs, nb, NSA_KV_W), lambda n, pt: (n, 0, 0))
    grid_spec = pltpu.PrefetchScalarGridSpec(
        num_scalar_prefetch=1, grid=(n_seq // seqs,),
        in_specs=[page_spec(q, p) for q in range(seqs) for p in range(n_pages)] + [
            const((CMP_BLOCK, NSA_KV_W, NSA_KV_W)), const((CMP_BLOCK, NSA_KV_W, NSA_KV_W)),
            const((CMP_BLOCK, NSA_KV_W)), const((CMP_BLOCK, NSA_KV_W)), const((NSA_KV_W, NSA_KV_W)),
            const((1, NSA_KV_W))],
        out_specs=[out_spec, out_spec],
        scratch_shapes=[pltpu.VMEM((2 * NSA_KV_W // LANES, seqs * nb * CMP_PITCH, LANES), F32)])
    return pl.pallas_call(
        functools.partial(_nsa_compress_body, n_pages=n_pages, seqs=seqs, transposed=transposed),
        grid_spec=grid_spec, out_shape=[out, out], compiler_params=_params(), name="nsa_compress",
    )(table, *([pages] * (seqs * n_pages)), w4[0], w4[1], pe4[0], pe4[1], b64, gk)


def _group_masks():
    lane = lax.broadcasted_iota(jnp.int32, (1, NSA_KV_W), 1)
    return [(lane // HEAD_DIM) == g for g in range(NSA_KV_HEADS)]


def _blockdiag(q, bm):
    zero = jnp.zeros((), q.dtype)
    return jnp.concatenate([jnp.where(bm[g], q[:, NSA_KV_W * j:NSA_KV_W * (j + 1)], zero)
                            for g in range(NSA_KV_HEADS) for j in range(NSA_HPG)], axis=0)


def _extract(obd, bm, tq):
    outs = []
    for j in range(NSA_HPG):
        z = jnp.zeros((tq, NSA_KV_W), F32)
        for g in range(NSA_KV_HEADS):
            r0 = (g * NSA_HPG + j) * tq
            z = z + jnp.where(bm[g], obd[r0:r0 + tq], 0.0)
        outs.append(z)
    return jnp.concatenate(outs, axis=1)


def _per_head_rows(a, tq):
    k = a.shape[1]
    a4 = jnp.broadcast_to(a.reshape(NSA_KV_HEADS, 1, tq, k), (NSA_KV_HEADS, NSA_HPG, tq, k))
    return a4.reshape(NSA_HEADS * tq, k)


def _topk_mask(score, n_sel, tq):
    if tq == LANES:
        nsp = -(-n_sel // 8) * 8
        sub = lax.broadcasted_iota(jnp.int32, (nsp, 1), 0)
        outs = []
        for g in range(NSA_KV_HEADS):
            st = score[g * tq:(g + 1) * tq].T[:nsp]
            rank = jnp.zeros(st.shape, F32)
            for s2 in range(n_sel):
                row = st[s2:s2 + 1, :]
                rank = rank + jnp.where(row > st, 1.0, jnp.where(row == st, jnp.where(sub > s2, 1.0, 0.0), 0.0))
            sel_t = jnp.where(rank < SLC_TOPN, 1.0, 0.0)
            sel_t = jnp.concatenate([sel_t, jnp.zeros((LANES - nsp, tq), F32)], axis=0)
            outs.append(sel_t.T)
        return jnp.concatenate(outs, axis=0)
    lane = lax.broadcasted_iota(jnp.int32, (1, LANES), 1)
    rank = jnp.zeros(score.shape, F32)
    for s2 in range(n_sel):
        col = score[:, s2:s2 + 1]
        rank = rank + jnp.where(col > score, 1.0, jnp.where(col == score, jnp.where(lane > s2, 1.0, 0.0), 0.0))
    return jnp.where(rank < SLC_TOPN, 1.0, 0.0)


def _cmp_and_select(qbd_n, kc, vc, ov, pos_base, tq, n_sel):
    nb = kc.shape[0]
    r = lax.broadcasted_iota(jnp.int32, (NSA_HEADS * tq, 1), 0)
    qpos = pos_base + (r & (tq - 1))
    blk_end = lax.broadcasted_iota(jnp.int32, (1, nb), 1) * CMP_STRIDE + (CMP_BLOCK - 1)
    visible = blk_end <= qpos
    s = jnp.where(visible, _mm_nt(qbd_n, kc) * ATT_SCALE, NEG_INF)
    e = jnp.exp(s - jnp.max(s, axis=-1, keepdims=True))
    p = jnp.where(visible, e / jnp.sum(e, axis=-1, keepdims=True), 0.0)
    o = _mm(p, vc)
    psum = jnp.sum(p.reshape(NSA_KV_HEADS, NSA_HPG, tq, nb), axis=1).reshape(NSA_KV_HEADS * tq, nb)
    imp = _mm_split(psum, ov.astype(MXU_DTYPE))
    r4 = lax.broadcasted_iota(jnp.int32, (NSA_KV_HEADS * tq, 1), 0)
    qblk = (pos_base + (r4 & (tq - 1))) // SLC_BLOCK
    sidx = lax.broadcasted_iota(jnp.int32, (1, LANES), 1)
    bonus = jnp.where(sidx == 0, FORCE_BONUS, jnp.where(sidx == qblk, FORCE_BONUS,
                                                        jnp.where(sidx == qblk - 1, FORCE_BONUS, 0.0)))
    score = jnp.where(sidx <= qblk, imp + bonus, NEG_INF)
    return o, _topk_mask(score, n_sel, tq), qpos


def _gated_sum(gates, gx, o_cmp, o_slc, o_win, bm, tq):
    gf = _mm_split(gates, gx)
    return (gf[:, 0:NSA_Q_W] * _extract(o_cmp, bm, tq) + gf[:, NSA_Q_W:2 * NSA_Q_W] * _extract(o_slc, bm, tq)
            + gf[:, 2 * NSA_Q_W:] * _extract(o_win, bm, tq))


def _nsa_prompt_body(qn_ref, qr_ref, gates_ref, kc_ref, vc_ref, kv_ref, win_ref, ov_ref, gx_ref, o_ref,
                     *, seq, n_sel, kt, wk):
    tq = Q_TILE
    t0 = pl.program_id(1) * tq
    bm = _group_masks()
    w = NSA_KV_W
    o_cmp, sel, _ = _cmp_and_select(_blockdiag(qn_ref[...], bm), kc_ref[0], vc_ref[0], ov_ref[...], t0, tq, n_sel)
    qbd = _blockdiag(qr_ref[...], bm) * ATT_SCALE
    sel_b = sel.astype(MXU_DTYPE)
    sidx = lax.broadcasted_iota(jnp.int32, (LANES, 1), 0)
    qpos_t = t0 + lax.broadcasted_iota(jnp.int32, (tq, 1), 0)
    qpos_gt = t0 + (lax.broadcasted_iota(jnp.int32, (NSA_KV_HEADS * tq, 1), 0) & (tq - 1))

    def kv_tile(jt, carry):
        m, l, acc = carry
        k0 = pl.multiple_of(jt * kt, kt)
        kpos = k0 + lax.broadcasted_iota(jnp.int32, (1, kt), 1)
        expand = jnp.where((kpos // SLC_BLOCK) == sidx, 1.0, 0.0).astype(MXU_DTYPE)
        bias = (jnp.dot(sel_b, expand, preferred_element_type=F32) - 1.0) * (-NEG_INF)
        bias = jnp.where(kpos <= qpos_gt, bias, NEG_INF)
        s = _mm_nt(qbd, kv_ref[pl.ds(k0, kt), 0:w]) + _per_head_rows(bias, tq)
        m_new = jnp.maximum(m, jnp.max(s, axis=-1, keepdims=True))
        alpha = jnp.exp(m - m_new)
        e = jnp.exp(s - m_new)
        l = alpha * l + jnp.sum(e, axis=-1, keepdims=True)
        acc = alpha * acc + _mm(e, kv_ref[pl.ds(k0, kt), w:2 * w])
        return m_new, l, acc

    rows = NSA_HEADS * tq
    n_tiles = (t0 + tq + kt - 1) // kt
    m, l, acc = lax.fori_loop(0, n_tiles, kv_tile, (jnp.full((rows, 1), SOFTMAX_M0, F32), jnp.zeros((rows, 1), F32),
                                                    jnp.zeros((rows, w), F32)))
    o_slc = acc * (1.0 / l)

    start = pl.multiple_of(jnp.maximum(t0 + tq - wk, 0), tq)
    kpos = start + lax.broadcasted_iota(jnp.int32, (1, wk), 1)
    bias = jnp.where(kpos <= qpos_t, jnp.where(kpos > qpos_t - WINDOW, 0.0, NEG_INF), NEG_INF)
    bias = jnp.broadcast_to(bias.reshape(1, tq, wk), (NSA_HEADS, tq, wk)).reshape(rows, wk)
    s = _mm_nt(qbd, win_ref[pl.ds(start, wk), 0:w]) + bias
    e = jnp.exp(s - jnp.max(s, axis=-1, keepdims=True))
    o_win = _mm(e, win_ref[pl.ds(start, wk), w:2 * w]) * (1.0 / jnp.sum(e, axis=-1, keepdims=True))
    o_ref[...] = _gated_sum(gates_ref[...], gx_ref[...], o_cmp, o_slc, o_win, bm, tq).astype(o_ref.dtype)


def nsa_attn_prompt(qn, qr, gates, kc, vc, kvb, n_seq, seq):
    tq = Q_TILE
    nb = kc.shape[1]
    n_sel = -(-seq // SLC_BLOCK)
    kt = min(KV_TILE, seq)
    wk = min(WINDOW + tq, seq)
    ov = jnp.asarray(_overlap_matrix(nb, seq))
    gx = jnp.asarray(_gate_expand(), MXU_DTYPE)
    per = seq // tq

    def qrows(width):
        return pl.BlockSpec((tq, width), lambda n, t: (n * per + t, 0))

    def per_seq(shape, lane_block=0):
        return pl.BlockSpec(shape, lambda n, t: (n,) + (0,) * (len(shape) - 2) + (lane_block,))

    return pl.pallas_call(
        functools.partial(_nsa_prompt_body, seq=seq, n_sel=n_sel, kt=kt, wk=wk),
        grid=(n_seq, per),
        in_specs=[qrows(NSA_Q_W), qrows(NSA_Q_W), qrows(LANES), per_seq((1, nb, NSA_KV_W)), per_seq((1, nb, NSA_KV_W)),
                  per_seq((seq, 2 * NSA_KV_W), 0), per_seq((seq, 2 * NSA_KV_W), 1),
                  _resident((nb, LANES)), _resident((LANES, 3 * NSA_Q_W))],
        out_specs=qrows(NSA_Q_W),
        out_shape=jax.ShapeDtypeStruct((n_seq * seq, NSA_Q_W), MXU_DTYPE),
        compiler_params=_params(2),
        name="nsa_attn_prompt",
    )(qn, qr, gates, kc, vc, kvb, kvb, ov, gx)


def _nsa_sample_body(pt_ref, *refs, n_pages, past_len, ts, n_sel, has_prev):
    del pt_ref
    page_refs = refs[:n_pages]
    (qn_ref, qr_ref, gates_ref, kc_ref, vc_ref, rnew_ref, wold_ref, wnew_ref, ov_ref, gx_ref, ex_ref,
     o_ref, wout_ref) = refs[n_pages:n_pages + 11] + refs[n_pages + 11 + int(has_prev):]
    tq = TS_PAD
    w = NSA_KV_W
    bm = _group_masks()
    o_cmp, sel, qpos = _cmp_and_select(_blockdiag(qn_ref[0].astype(F32), bm), kc_ref[0], vc_ref[0], ov_ref[...],
                                       past_len, tq, n_sel)
    qbd = _blockdiag(qr_ref[0].astype(F32), bm).astype(MXU_DTYPE)
    pad = jnp.zeros((PAGE_SIZE - tq, w), F32)

    bias = (jnp.dot(sel.astype(MXU_DTYPE), ex_ref[...], preferred_element_type=F32) - 1.0) * (-NEG_INF)
    k_new = jnp.concatenate([rnew_ref[0][:, 0:w], pad], axis=0)
    v_new = jnp.concatenate([rnew_ref[0][:, w:2 * w], pad], axis=0)
    s = jnp.concatenate([_mm(qbd, r[0, 0:w, :]) for r in page_refs] + [_mm_nt(qbd, k_new)], axis=1)
    s = s * ATT_SCALE + _per_head_rows(bias, tq)
    kpos = lax.broadcasted_iota(jnp.int32, (1, (n_pages + 1) * PAGE_SIZE), 1)
    s = jnp.where(kpos <= qpos, s, NEG_INF)
    e = jnp.exp(s - jnp.max(s, axis=-1, keepdims=True))
    acc = _mm(e[:, n_pages * PAGE_SIZE:], v_new)
    for p, r in enumerate(page_refs):
        acc = acc + _mm_nt(e[:, p * PAGE_SIZE:(p + 1) * PAGE_SIZE], r[0, w:2 * w, :])
    o_slc = acc * (1.0 / jnp.sum(e, axis=-1, keepdims=True))

    wlen = wold_ref.shape[2]
    kw_new = jnp.concatenate([wnew_ref[0][:, 0:w], pad], axis=0)
    vw_new = jnp.concatenate([wnew_ref[0][:, w:2 * w], pad], axis=0)
    s = jnp.concatenate([_mm(qbd, wold_ref[0, 0:w, :]), _mm_nt(qbd, kw_new)], axis=1) * ATT_SCALE
    kpos = (past_len - wlen) + lax.broadcasted_iota(jnp.int32, (1, wlen + PAGE_SIZE), 1)
    s = jnp.where(kpos <= qpos, s, NEG_INF)
    s = jnp.where(kpos > qpos - WINDOW, s, NEG_INF)
    e = jnp.exp(s - jnp.max(s, axis=-1, keepdims=True))
    acc = _mm_nt(e[:, 0:wlen], wold_ref[0, w:2 * w, :]) + _mm(e[:, wlen:], vw_new)
    o_win = acc * (1.0 / jnp.sum(e, axis=-1, keepdims=True))

    o_ref[0] = _gated_sum(gates_ref[0], gx_ref[...], o_cmp, o_slc, o_win, bm, tq)

    shifted = pltpu.roll(wold_ref[0], wlen - ts, 1)
    new_rows = jnp.concatenate([wnew_ref[0], jnp.zeros((LANES - tq, 2 * w), F32)], axis=0)
    new_t = jnp.concatenate([new_rows[:, j * LANES:(j + 1) * LANES].T for j in range(2 * w // LANES)], axis=0)
    new_t = pltpu.roll(new_t, LANES - ts, 1)
    lane = lax.broadcasted_iota(jnp.int32, (1, LANES), 1)
    wout_ref[0, :, 0:wlen - LANES] = shifted[:, 0:wlen - LANES]
    wout_ref[0, :, wlen - LANES:wlen] = jnp.where(lane >= LANES - ts, new_t, shifted[:, wlen - LANES:wlen])


def nsa_attn_sample(table, pages, qn, qr, gates, kc, vc, rows_new, win_old, win_new, win_base, past_len, ts,
                    win_prev):
    n_seq, n_pages = table.shape
    nb = kc.shape[1]
    wlen = win_old.shape[2]
    length = past_len + ts
    n_sel = -(-length // SLC_BLOCK)
    n_keys = (n_pages + 1) * PAGE_SIZE
    ov = jnp.asarray(_overlap_matrix(nb, length))
    gx = jnp.asarray(_gate_expand(), MXU_DTYPE)
    ex = jnp.asarray((np.arange(n_keys)[None, :] // SLC_BLOCK == np.arange(LANES)[:, None]).astype(np.float32), MXU_DTYPE)

    def page_spec(p):
        return pl.BlockSpec((1, 2 * NSA_KV_W, PAGE_SIZE), lambda n, pt: (pt[n, p], 1, 0))

    def per_seq(shape, lane_block=0, base=0):
        return pl.BlockSpec(shape, lambda n, pt: (base + n,) + (0,) * (len(shape) - 2) + (lane_block,))

    def const(shape):
        return pl.BlockSpec(shape, lambda n, pt: (0,) * len(shape), pipeline_mode=pl.Buffered(1))

    in_specs = [page_spec(p) for p in range(n_pages)] + [
        per_seq((1, TS_PAD, NSA_Q_W)), per_seq((1, TS_PAD, NSA_Q_W)), per_seq((1, TS_PAD, LANES)),
        per_seq((1, nb, NSA_KV_W)), per_seq((1, nb, NSA_KV_W)), per_seq((1, TS_PAD, 2 * NSA_KV_W), 1),
        per_seq((1, 2 * NSA_KV_W, wlen), 0, win_base), per_seq((1, TS_PAD, 2 * NSA_KV_W)),
        const((nb, LANES)), const((LANES, 3 * NSA_Q_W)), const((LANES, n_keys))]
    operands = [table, *([pages] * n_pages), qn, qr, gates, kc, vc, rows_new, win_old, win_new, ov, gx, ex]
    aliases = {}
    if win_prev is not None:
        in_specs.append(pl.BlockSpec(memory_space=pl.ANY))
        aliases = {len(operands): 1}
        operands.append(win_prev)
    grid_spec = pltpu.PrefetchScalarGridSpec(
        num_scalar_prefetch=1, grid=(n_seq,), in_specs=in_specs,
        out_specs=[per_seq((1, TS_PAD, NSA_Q_W)), per_seq((1, 2 * NSA_KV_W, wlen), 0, win_base)])
    return pl.pallas_call(
        functools.partial(_nsa_sample_body, n_pages=n_pages, past_len=past_len, ts=ts, n_sel=n_sel,
                          has_prev=win_prev is not None),
        grid_spec=grid_spec,
        out_shape=[jax.ShapeDtypeStruct((n_seq, TS_PAD, NSA_Q_W), F32),
                   jax.ShapeDtypeStruct(win_old.shape, F32)],
        input_output_aliases=aliases,
        compiler_params=_params(), name="nsa_attn_sample",
    )(*operands)


S5_NB = 8
S5_GB = 8
S5_HALF = S5_GB * S5_STATE
S5_TIME = 256


def _s5_disc_body(lr_ref, li_ref, ldt_ref, bre_ref, bim_ref, are_ref, aim_ref, bbre_ref, bbim_ref):
    dt = jnp.exp(ldt_ref[...])
    lr = jnp.minimum(lr_ref[...], -1e-4)
    li = li_ref[...]
    mag = jnp.exp(lr * dt)
    a_re = mag * jnp.cos(li * dt)
    a_im = mag * jnp.sin(li * dt)
    den = lr * lr + li * li
    z_re = ((a_re - 1.0) * lr + a_im * li) / den
    z_im = (a_im * lr - (a_re - 1.0) * li) / den
    are_ref[...] = a_re
    aim_ref[...] = a_im
    bbre_ref[...] = z_re * bre_ref[...] - z_im * bim_ref[...]
    bbim_ref[...] = z_re * bim_ref[...] + z_im * bre_ref[...]


def s5_discretize(lam_re, lam_im, log_dt, b_re, b_im):
    rows = S5_GROUPS * S5_GROUP_CH

    def per_channel(a):
        return jnp.broadcast_to(a[:, None, :], (S5_GROUPS, S5_GROUP_CH, S5_STATE)).reshape(rows, S5_STATE)

    args = (per_channel(lam_re), per_channel(lam_im), per_channel(jnp.broadcast_to(log_dt[:, None], lam_re.shape)),
            b_re.transpose(0, 2, 1).reshape(rows, S5_STATE), b_im.transpose(0, 2, 1).reshape(rows, S5_STATE))
    out = jax.ShapeDtypeStruct((rows, S5_STATE), F32)
    a_re, a_im, bb_re, bb_im = pl.pallas_call(_s5_disc_body, out_shape=[out] * 4, name="s5_discretize")(*args)
    shape = (S5_GROUPS, S5_GROUP_CH, S5_STATE)
    return a_re.reshape(shape)[:, 0], a_im.reshape(shape)[:, 0], bb_re.reshape(shape), bb_im.reshape(shape)


def _s5_scan_body(u_ref, bb_ref, cc_ref, a_ref, h0_ref, y_ref, hout_ref, bu_scr, hs_scr, h_scr, *, tc, use_h0):
    t = pl.program_id(2)

    @pl.when(t == 0)
    def _():
        h_scr[...] = h0_ref[0, 0] if use_h0 else jnp.zeros(h_scr.shape, F32)

    bu_scr[...] = _mm(u_ref[...], bb_ref[0])
    ar = a_ref[0][:, :S5_HALF]
    ai = a_ref[0][:, S5_HALF:]

    def step(i, carry):
        hr, hi = carry
        r = pl.multiple_of(i * S5_NB, S5_NB)
        nhr = ar * hr - ai * hi + bu_scr[pl.ds(r, S5_NB), :S5_HALF]
        nhi = ar * hi + ai * hr + bu_scr[pl.ds(r, S5_NB), S5_HALF:]
        hs_scr[pl.ds(r, S5_NB), :S5_HALF] = nhr
        hs_scr[pl.ds(r, S5_NB), S5_HALF:] = nhi
        return nhr, nhi

    hr, hi = lax.fori_loop(0, tc, step, (h_scr[:, :S5_HALF], h_scr[:, S5_HALF:]), unroll=min(tc, 8))
    h_scr[:, :S5_HALF] = hr
    h_scr[:, S5_HALF:] = hi
    y_ref[...] = _mm(hs_scr[...], cc_ref[0])

    @pl.when(t == pl.num_programs(2) - 1)
    def _():
        hout_ref[0, 0] = h_scr[...]


def s5_scan(u, h0, a_re, a_im, bb_re, bb_im, c_re, c_im):
    n_real, t_len, _ = u.shape
    n = -(-n_real // S5_NB) * S5_NB
    if n != n_real:
        u = jnp.pad(u, ((0, n - n_real), (0, 0), (0, 0)))
        h0 = None if h0 is None else jnp.pad(h0, ((0, n - n_real), (0, 0), (0, 0), (0, 0)))
    nb = n // S5_NB
    ngb = S5_GROUPS // S5_GB
    tc = min(S5_TIME, t_len)
    eye = jnp.eye(S5_GB, dtype=F32)

    def blockdiag_in(bb):
        return jnp.einsum('ab,xacp->xacbp', eye, bb.reshape(ngb, S5_GB, S5_GROUP_CH, S5_STATE)).reshape(
            ngb, S5_GB * S5_GROUP_CH, S5_HALF)

    def blockdiag_out(cc):
        return jnp.einsum('ab,xbcp->xapbc', eye, cc.reshape(ngb, S5_GB, S5_GROUP_CH, S5_STATE)).reshape(
            ngb, S5_HALF, S5_GB * S5_GROUP_CH)

    bb = jnp.concatenate([blockdiag_in(bb_re), blockdiag_in(bb_im)], axis=2).astype(MXU_DTYPE)
    cc = jnp.concatenate([blockdiag_out(c_re), blockdiag_out(-c_im)], axis=1).astype(MXU_DTYPE)
    a = jnp.concatenate([a_re.reshape(ngb, S5_HALF), a_im.reshape(ngb, S5_HALF)], axis=1)
    a = jnp.broadcast_to(a[:, None, :], (ngb, S5_NB, 2 * S5_HALF))
    use_h0 = h0 is not None
    if use_h0:
        h0b = h0.reshape(nb, S5_NB, 2, ngb, S5_HALF).transpose(0, 3, 1, 2, 4).reshape(nb, ngb, S5_NB, 2 * S5_HALF)
    else:
        h0b = jnp.zeros((1, 1, S5_NB, 2 * S5_HALF), F32)
    ub = u.reshape(nb, S5_NB, t_len, D_MODEL).transpose(0, 2, 1, 3).reshape(nb * t_len * S5_NB, D_MODEL)
    per = t_len // tc
    lanes_u = S5_GB * S5_GROUP_CH

    rows_spec = pl.BlockSpec((tc * S5_NB, lanes_u), lambda b, g, t: (b * per + t, g))
    state_spec = pl.BlockSpec((1, 1, S5_NB, 2 * S5_HALF), lambda b, g, t: (b, g, 0, 0))
    h0_spec = state_spec if use_h0 else pl.BlockSpec((1, 1, S5_NB, 2 * S5_HALF), lambda b, g, t: (0, 0, 0, 0))
    y, h_last = pl.pallas_call(
        functools.partial(_s5_scan_body, tc=tc, use_h0=use_h0),
        grid=(nb, ngb, per),
        in_specs=[rows_spec, pl.BlockSpec((1, lanes_u, 2 * S5_HALF), lambda b, g, t: (g, 0, 0)),
                  pl.BlockSpec((1, 2 * S5_HALF, lanes_u), lambda b, g, t: (g, 0, 0)),
                  pl.BlockSpec((1, S5_NB, 2 * S5_HALF), lambda b, g, t: (g, 0, 0)), h0_spec],
        out_specs=[rows_spec, state_spec],
        out_shape=[jax.ShapeDtypeStruct((nb * t_len * S5_NB, D_MODEL), F32),
                   jax.ShapeDtypeStruct((nb, ngb, S5_NB, 2 * S5_HALF), F32)],
        scratch_shapes=[pltpu.VMEM((tc * S5_NB, 2 * S5_HALF), F32), pltpu.VMEM((tc * S5_NB, 2 * S5_HALF), F32),
                        pltpu.VMEM((S5_NB, 2 * S5_HALF), F32)],
        compiler_params=pltpu.CompilerParams(dimension_semantics=("parallel", "parallel", "arbitrary"),
                                             vmem_limit_bytes=VMEM_LIMIT),
        name="s5_scan",
    )(ub, bb, cc, a, h0b)
    y = y.reshape(nb, t_len, S5_NB, D_MODEL).transpose(0, 2, 1, 3).reshape(n, t_len, D_MODEL)
    h_last = h_last.reshape(nb, ngb, S5_NB, 2, S5_GB, S5_STATE).transpose(0, 2, 3, 1, 4, 5).reshape(
        n, 2, S5_GROUPS, S5_STATE)
    return y[:n_real], h_last[:n_real]


def _s5_out_body(x_ref, y_ref, u_ref, d_ref, w_ref, o_ref):
    z = jax.nn.gelu(y_ref[...] + d_ref[...] * u_ref[...])
    ab = _mm(z, w_ref[...])
    o_ref[...] = x_ref[...] + ab[:, :D_MODEL] * jax.nn.sigmoid(ab[:, D_MODEL:])


def s5_out(x, y, u, d_skip, w_glu):
    rows = x.shape[0]
    tm = _row_tile(rows)
    return pl.pallas_call(
        _s5_out_body,
        grid=(rows // tm,),
        in_specs=[_rows(tm, D_MODEL), _rows(tm, D_MODEL), _rows(tm, D_MODEL), _resident((1, D_MODEL)),
                  _resident((D_MODEL, 2 * D_MODEL))],
        out_specs=_rows(tm, D_MODEL),
        out_shape=jax.ShapeDtypeStruct((rows, D_MODEL), F32),
        compiler_params=_params(),
        name="s5_out",
    )(x, y, u, d_skip.reshape(1, -1), w_glu.astype(MXU_DTYPE))


HG_SUB = 32
HG_TIME = 256
HG_HEADS_PER_STEP = 2


def _mm_exact(l01, x):
    x1 = x.astype(MXU_DTYPE)
    r1 = x - x1.astype(F32)
    x2 = r1.astype(MXU_DTYPE)
    x3 = (r1 - x2.astype(F32)).astype(MXU_DTYPE)
    dot = functools.partial(jnp.dot, preferred_element_type=F32)
    return dot(l01, x1) + dot(l01, x2) + dot(l01, x3)


def _hgrn_body(q_ref, fz_ref, v_ref, g_ref, lb_ref, og_ref, s0_ref, o_ref, sout_ref, s_scr,
               *, tb, chunk, sub, valid, use_s0, hps):
    tblk = pl.program_id(2)

    @pl.when(tblk == 0)
    def _():
        s_scr[...] = s0_ref[0] if use_s0 else jnp.zeros(s_scr.shape, F32)

    og = og_ref[...]
    eye = (lax.broadcasted_iota(jnp.int32, (HG_DK, HG_DK), 0) == lax.broadcasted_iota(jnp.int32, (HG_DK, HG_DK), 1))
    tril = jnp.where(lax.broadcasted_iota(jnp.int32, (chunk, chunk), 0)
                     >= lax.broadcasted_iota(jnp.int32, (chunk, chunk), 1), 1.0, 0.0).astype(MXU_DTYPE)
    trow = lax.broadcasted_iota(jnp.int32, (sub, 1), 0)
    nsub = chunk // sub

    for hh, ci in itertools.product(range(hps), range(tb // chunk)):
        r0 = ci * chunk
        cols = slice(hh * HG_DK, (hh + 1) * HG_DK)
        lb = lb_ref[:, cols]
        q = q_ref[r0:r0 + chunk, cols]
        v = v_ref[r0:r0 + chunk, cols]
        f = lb + (1.0 - lb) * jax.nn.sigmoid(fz_ref[r0:r0 + chunk, cols])
        k = 1.0 - f
        lf = jnp.log(f)
        if valid < tb:
            live = (r0 + lax.broadcasted_iota(jnp.int32, (chunk, 1), 0)) < valid
            k = jnp.where(live, k, 0.0)
            lf = jnp.where(live, lf, 0.0)
        gcum = _mm_exact(tril, lf)
        state = s_scr[hh]
        o_inter = _mm(q * jnp.exp(gcum), state)
        o_blocks = [o_inter[i * sub:(i + 1) * sub] for i in range(nsub)]

        for j in range(nsub - 1):
            lo, hi = j * sub, (j + 1) * sub
            g_ref_row = gcum[hi - 1:hi, :]
            k_t = k[lo:hi] * jnp.exp(g_ref_row - gcum[lo:hi])
            q_t = q[hi:] * jnp.exp(gcum[hi:] - g_ref_row)
            contrib = _mm(_mm_nt(q_t, k_t), v[lo:hi])
            for i in range(j + 1, nsub):
                o_blocks[i] = o_blocks[i] + contrib[(i - j - 1) * sub:(i - j) * sub]

        for i in range(nsub):
            lo, hi = i * sub, (i + 1) * sub
            q_i, k_i, v_i, g_i = q[lo:hi], k[lo:hi], v[lo:hi], gcum[lo:hi]
            acc = jnp.zeros((sub, HG_DV), F32)
            for s in range(sub):
                decay = jnp.exp(jnp.minimum(g_i - g_i[s:s + 1], 0.0))
                wgt = jnp.sum(q_i * k_i[s:s + 1] * decay, axis=1, keepdims=True)
                acc = acc + jnp.where(trow >= s, wgt, 0.0) * v_i[s:s + 1]
            o_blocks[i] = o_blocks[i] + acc

        g_last = gcum[chunk - 1:chunk, :]
        k_t = k * jnp.exp(g_last - gcum)
        decay_col = jnp.sum(jnp.where(eye, jnp.exp(g_last), 0.0), axis=1, keepdims=True)
        kv = lax.dot_general(k_t.astype(MXU_DTYPE), v.astype(MXU_DTYPE), (((0,), (0,)), ((), ())),
                             preferred_element_type=F32)
        s_scr[hh] = decay_col * state + kv

        o = jnp.concatenate(o_blocks, axis=0)
        o = o * lax.rsqrt(jnp.mean(o * o, axis=-1, keepdims=True) + RMS_EPS) * og
        gate = g_ref[r0:r0 + chunk, cols]
        o_ref[r0:r0 + chunk, cols] = o * (gate * jax.nn.sigmoid(gate))

    @pl.when(tblk == pl.num_programs(2) - 1)
    def _():
        sout_ref[0] = s_scr[...]


def hgrn_scan(pr, s0, o_gain, lb, n_seq, t_rows, valid):
    tb = min(HG_TIME, t_rows)
    chunk = min(HG_CHUNK, tb)
    sub = min(HG_SUB, chunk)
    per = t_rows // tb
    hps = HG_HEADS if tb < HG_TIME else HG_HEADS_PER_STEP
    hblocks = HG_HEADS // hps
    use_s0 = s0 is not None
    if not use_s0:
        s0 = jnp.zeros((1, hps, HG_DK, HG_DV), F32)

    def part(idx):
        return pl.BlockSpec((tb, hps * HG_DK), lambda n, h, t: (n * per + t, idx * hblocks + h))

    head_vec = pl.BlockSpec((1, hps * HG_DK), lambda n, h, t: (0, h))
    state_spec = pl.BlockSpec((1, hps, HG_DK, HG_DV), lambda n, h, t: (n, h, 0, 0))
    s0_spec = state_spec if use_s0 else pl.BlockSpec((1, hps, HG_DK, HG_DV), lambda n, h, t: (0, 0, 0, 0))
    return pl.pallas_call(
        functools.partial(_hgrn_body, tb=tb, chunk=chunk, sub=sub, valid=valid, use_s0=use_s0, hps=hps),
        grid=(n_seq, hblocks, per),
        in_specs=[part(0), part(1), part(2), part(3), head_vec, pl.BlockSpec((1, HG_DV), lambda n, h, t: (0, 0)), s0_spec],
        out_specs=[pl.BlockSpec((tb, hps * HG_DV), lambda n, h, t: (n * per + t, h)), state_spec],
        out_shape=[jax.ShapeDtypeStruct((n_seq * t_rows, D_MODEL), F32),
                   jax.ShapeDtypeStruct((n_seq, HG_HEADS, HG_DK, HG_DV), F32)],
        scratch_shapes=[pltpu.VMEM((hps, HG_DK, HG_DV), F32)],
        compiler_params=pltpu.CompilerParams(dimension_semantics=("parallel", "parallel", "arbitrary"),
                                             vmem_limit_bytes=VMEM_LIMIT),
        name="hgrn_scan",
    )(pr, pr, pr, pr, lb.reshape(1, -1), o_gain.reshape(1, -1), s0)


def kernel(x_prompt, x_sample, cache_nsa, state_nsa_win, state_s5, state_hgrn, page_table, p_prompt, p_sample, norm_gain, ffn_w_in, ffn_w_out, ple_w_gate, ple_w_proj, nsa_w_in, nsa_w_out, nsa_qk_gain, nsa_cmp_pe, nsa_cmp_w, s5_lam_re, s5_lam_im, s5_log_dt, s5_b_re, s5_b_im, s5_c_re, s5_c_im, s5_d, s5_w_glu, hg_w_in, hg_w_out, hg_o_gain, hg_lb_raw):
    B, T, _ = x_prompt.shape
    Bs, Ts, _ = x_sample.shape
    rp = B * T
    rs = Bs * Ts
    n_pages = page_table.shape[1]
    past_len = n_pages * PAGE_SIZE
    n_phys = cache_nsa.shape[1]
    wlen = state_nsa_win.shape[2]
    assert T % Q_TILE == 0 and T % PAGE_SIZE == 0 and T % min(KV_TILE, T) == 0 and Ts <= TS_PAD
    lb_sm = jax.nn.softmax(hg_lb_raw.astype(F32), axis=0)
    lower_bounds = jnp.cumsum(lb_sm, axis=0) - lb_sm[0]

    x = jnp.concatenate([x_prompt.reshape(rp, D_MODEL), x_sample.reshape(rs, D_MODEL)], axis=0)
    p_all = jnp.concatenate([p_prompt.reshape(DEPTH, rp, PLE_DIM), p_sample.reshape(DEPTH, rs, PLE_DIM)], axis=1)
    pos = jnp.concatenate([jnp.tile(jnp.arange(T), B), jnp.tile(past_len + jnp.arange(Ts), Bs)])
    cache_pages = cache_nsa.transpose(0, 1, 3, 4, 5, 2).reshape(-1, 4 * NSA_KV_W, PAGE_SIZE)
    win_state = state_nsa_win.transpose(0, 1, 3, 4, 5, 2).reshape(-1, 2 * NSA_KV_W, wlen)
    prompt_table = (jnp.arange(B, dtype=jnp.int32)[:, None] * (T // PAGE_SIZE)
                    + jnp.arange(T // PAGE_SIZE, dtype=jnp.int32)[None, :])
    out_perm = _head_perm()

    def split(a):
        return a[:rp].reshape(B, T, -1), a[rp:].reshape(Bs, Ts, -1)

    def join(a, b):
        return jnp.concatenate([a.reshape(rp, -1), b.reshape(rs, -1)], axis=0)

    def sample_pad(a):
        return jnp.pad(a[rp:].reshape(Bs, Ts, -1), ((0, 0), (0, TS_PAD - Ts), (0, 0)))

    outs_p = {0: [], 1: [], 2: [], 3: []}
    outs_s = {0: [], 1: [], 2: [], 3: []}
    win_all = None
    for i in range(DEPTH):
        kind = LAYER_KIND[i]
        j = LAYER_SLOT[i]
        g = norm_gain[i]
        x, xn = ffn_step(x, g[0], g[1], ffn_w_in[i, 0], ffn_w_out[i, 0])
        if kind == 0:
            qn, qr, rows_new, win_new, gates, kvb = nsa_proj(xn, nsa_w_in[j], nsa_qk_gain[j], pos)
            cmp_args = (nsa_cmp_pe[j], nsa_cmp_w[j], nsa_qk_gain[j, 1])
            kc_p, vc_p = nsa_compress(rows_new.reshape(-1, PAGE_SIZE, 4 * NSA_KV_W), prompt_table, *cmp_args, False)
            o_p = nsa_attn_prompt(qn, qr, gates, kc_p, vc_p, kvb, B, T)
            table = page_table.astype(jnp.int32) + j * n_phys
            kc_s, vc_s = nsa_compress(cache_pages, table, *cmp_args, True)
            o_s, win_all = nsa_attn_sample(table, cache_pages, sample_pad(qn), sample_pad(qr), sample_pad(gates), kc_s,
                                           vc_s, sample_pad(rows_new), win_state, sample_pad(win_new), j * Bs, past_len,
                                           Ts, win_all)
            r_p, r_s = split(rows_new)
            outs_p[0].append(r_p.reshape(B, T, 4, NSA_KV_HEADS, HEAD_DIM))
            outs_s[0].append(r_s.reshape(Bs, Ts, 4, NSA_KV_HEADS, HEAD_DIM))
            buf = min(WINDOW, T)
            outs_p[1].append(win_new[:rp].reshape(B, T, 2, NSA_KV_HEADS, HEAD_DIM)[:, T - buf:])
            o_all = jnp.concatenate([o_p, o_s[:, :Ts].reshape(rs, NSA_Q_W).astype(o_p.dtype)], axis=0)
            x = resid_proj(x, o_all, nsa_w_out[j][out_perm])
        elif kind == 1:
            u_p, u_s = split(xn)
            disc = s5_discretize(s5_lam_re[j], s5_lam_im[j], s5_log_dt[j], s5_b_re[j], s5_b_im[j])
            y_p, h_p = s5_scan(u_p, None, *disc, s5_c_re[j], s5_c_im[j])
            y_s, h_s = s5_scan(u_s, state_s5[j], *disc, s5_c_re[j], s5_c_im[j])
            outs_p[2].append(h_p); outs_s[2].append(h_s)
            x = s5_out(x, join(y_p, y_s), xn, s5_d[j], s5_w_glu[j])
        else:
            pr = proj(xn, hg_w_in[j])
            pr_s = jnp.pad(pr[rp:].reshape(Bs, Ts, -1), ((0, 0), (0, TS_PAD - Ts), (0, 0))).reshape(Bs * TS_PAD, -1)
            o_p, s_p = hgrn_scan(pr, None, hg_o_gain[j], lower_bounds[i], B, T, T)
            o_s, s_s = hgrn_scan(pr_s, state_hgrn[j], hg_o_gain[j], lower_bounds[i], Bs, TS_PAD, Ts)
            outs_p[3].append(s_p); outs_s[3].append(s_s)
            o_s = o_s.reshape(Bs, TS_PAD, D_MODEL)[:, :Ts].reshape(rs, D_MODEL)
            x = resid_proj(x, jnp.concatenate([o_p, o_s], axis=0), hg_w_out[j])
        x = ffn_ple_step(x, g[2], g[3], ffn_w_in[i, 1], ffn_w_out[i, 1], p_all[i], ple_w_gate[i], ple_w_proj[i])

    y_p, y_s = split(x)
    win_out = win_all.reshape(-1, Bs, 2, NSA_KV_HEADS, HEAD_DIM, wlen).transpose(0, 1, 5, 2, 3, 4)
    return (y_p, y_s,
            jnp.stack(outs_p[0]), jnp.stack(outs_p[1]), jnp.stack(outs_p[2]), jnp.stack(outs_p[3]),
            jnp.stack(outs_s[0]), win_out, jnp.stack(outs_s[2]), jnp.stack(outs_s[3]))
```
